```python
import math
import jax, jax.numpy as jnp
from jax import lax
import numpy as np

D_MODEL = 2048
BATCH = 4
SEQ = 2048
DEPTH = 1

ATTN_HEADS = 8
HEAD_DIM = 128
ATTN_WIDTH = ATTN_HEADS * HEAD_DIM
MOBA_BLOCK = 256
MOBA_TOPK = 3
Q_CHUNK = 32
LRU_WIDTH = 1024
LRU_BLOCKS = 8
LRU_BLOCK_DIM = LRU_WIDTH // LRU_BLOCKS
CONV_WIDTH = 4
LRU_C = 8.0
N_BRANCHES = 2
IN_WIDTH = 3 * ATTN_WIDTH + 2 * LRU_WIDTH + N_BRANCHES * D_MODEL
REL_BUCKETS = 32
REL_MAX_DIST = 128
PEER_HEADS = 8
PEER_NKEYS = 128
PEER_EXPERTS = PEER_NKEYS * PEER_NKEYS
PEER_DKEY = 256
PEER_TOPK = 16
PEER_CHUNK = 128
EPS = 1e-6
NEG = -1e30

kernel_name = 'hybrid_moba_rglru_peer'


def rmsnorm(x, g):
    xf = x.astype(jnp.float32)
    y = xf * lax.rsqrt(jnp.mean(xf * xf, axis=-1, keepdims=True) + EPS)
    return (y * g.astype(jnp.float32)).astype(x.dtype)


def rel_bucket(dist):
    n = jnp.maximum(dist, 0)
    max_exact = REL_BUCKETS // 2
    nf = jnp.maximum(n, 1).astype(jnp.float32)
    large = max_exact + (jnp.log(nf / max_exact) / math.log(REL_MAX_DIST / max_exact)
                         * (REL_BUCKETS - max_exact)).astype(jnp.int32)
    large = jnp.minimum(large, REL_BUCKETS - 1)
    return jnp.where(n < max_exact, n, large)


def moba_attention(q, k, v, rel_bias):
    B, H, S, hd = q.shape
    nb = -(-S // MOBA_BLOCK)
    s_pad = nb * MOBA_BLOCK
    topk = min(MOBA_TOPK, nb)
    pad = ((0, 0), (0, 0), (0, s_pad - S), (0, 0))
    k_pad = jnp.pad(k, pad)
    v_pad = jnp.pad(v, pad)
    k_blocks = k_pad.reshape(B, H, nb, MOBA_BLOCK, hd)
    v_blocks = v_pad.reshape(B, H, nb, MOBA_BLOCK, hd)
    k_mean = jnp.mean(k_blocks.astype(jnp.float32), axis=3)
    bias_h = rel_bias.T.astype(jnp.float32)
    scale = HEAD_DIM ** -0.5
    b_idx = jnp.arange(B)[:, None, None, None]
    h_idx = jnp.arange(H)[None, :, None, None]
    h_idx5 = jnp.arange(H)[None, :, None, None, None]
    blk_ar = jnp.arange(nb)
    slot_ar = jnp.arange(topk)
    in_blk = jnp.arange(MOBA_BLOCK)

    def chunk(start):
        qs = lax.dynamic_slice_in_dim(q, start, Q_CHUNK, axis=2).astype(jnp.float32)
        q_pos = start + jnp.arange(Q_CHUNK)
        own = start // MOBA_BLOCK
        gate = jnp.einsum('bhqd,bhnd->bhqn', qs, k_mean)
        gate = jnp.where(blk_ar < own, gate, NEG)
        _, sel = lax.top_k(gate, topk)
        slot_ok = slot_ar < own
        k_sel = k_blocks[b_idx, h_idx, sel].astype(jnp.float32)
        v_sel = v_blocks[b_idx, h_idx, sel].astype(jnp.float32)
        k_pos_sel = sel[..., None] * MOBA_BLOCK + in_blk
        dist_sel = q_pos[:, None, None] - k_pos_sel
        bias_sel = bias_h[h_idx5, rel_bucket(dist_sel)]
        logit_sel = jnp.einsum('bhqd,bhqjkd->bhqjk', qs, k_sel) * scale + bias_sel
        logit_sel = jnp.where(slot_ok[:, None], logit_sel, NEG)
        k_own = lax.dynamic_slice_in_dim(k_pad, own * MOBA_BLOCK, MOBA_BLOCK, axis=2).astype(jnp.float32)
        v_own = lax.dynamic_slice_in_dim(v_pad, own * MOBA_BLOCK, MOBA_BLOCK, axis=2).astype(jnp.float32)
        dist_own = q_pos[:, None] - (own * MOBA_BLOCK + in_blk)[None, :]
        bias_own = bias_h[:, rel_bucket(dist_own)]
        logit_own = jnp.einsum('bhqd,bhkd->bhqk', qs, k_own) * scale + bias_own
        logit_own = jnp.where(dist_own >= 0, logit_own, NEG)
        logits = jnp.concatenate(
            [logit_sel.reshape(B, H, Q_CHUNK, topk * MOBA_BLOCK), logit_own], axis=-1)
        p = jax.nn.softmax(logits, axis=-1)
        p_sel = p[..., :topk * MOBA_BLOCK].reshape(B, H, Q_CHUNK, topk, MOBA_BLOCK)
        p_own = p[..., topk * MOBA_BLOCK:]
        out = (jnp.einsum('bhqjk,bhqjkd->bhqd', p_sel, v_sel)
               + jnp.einsum('bhqk,bhkd->bhqd', p_own, v_own))
        return out.astype(q.dtype)

    starts = jnp.arange(S // Q_CHUNK, dtype=jnp.int32) * Q_CHUNK
    out = lax.map(chunk, starts)
    return out.transpose(1, 0, 3, 2, 4).reshape(B, S, H * hd)


def causal_conv(x, w, b):
    C = x.shape[-1]
    y = lax.conv_general_dilated(
        x, w[:, None, :].astype(x.dtype), window_strides=(1,),
        padding=[(CONV_WIDTH - 1, 0)], dimension_numbers=('NWC', 'WIO', 'NWC'),
        feature_group_count=C)
    return y + b.astype(x.dtype)


def rg_lru(x, w_a, b_a, w_x, b_x, lam):
    B, S, C = x.shape
    xf = x.astype(jnp.float32)
    xb = xf.reshape(B, S, LRU_BLOCKS, LRU_BLOCK_DIM)
    r = jax.nn.sigmoid(jnp.einsum('bsgi,gij->bsgj', xb, w_a.astype(jnp.float32)).reshape(B, S, C)
                       + b_a.astype(jnp.float32))
    i = jax.nn.sigmoid(jnp.einsum('bsgi,gij->bsgj', xb, w_x.astype(jnp.float32)).reshape(B, S, C)
                       + b_x.astype(jnp.float32))
    log_a = -LRU_C * r * jax.nn.softplus(-lam.astype(jnp.float32))
    a = jnp.exp(log_a)
    u = jnp.sqrt(-jnp.expm1(2.0 * log_a)) * (i * xf)

    def combine(left, right):
        a1, b1 = left
        a2, b2 = right
        return a1 * a2, a2 * b1 + b2

    _, h = lax.associative_scan(combine, (a, u), axis=1)
    return h.astype(x.dtype)


def peer(x, w_q, sub_keys, u_tab, v_tab):
    B, S, D = x.shape
    T = B * S
    K = PEER_TOPK
    xt = x.reshape(T, D)
    q = (xt @ w_q).astype(jnp.float32).reshape(T, PEER_HEADS, 2, PEER_DKEY // 2)
    s = jnp.einsum('thcd,hcnd->thcn', q, sub_keys.astype(jnp.float32))
    sv, si = lax.top_k(s, K)
    cand = sv[:, :, 0, :, None] + sv[:, :, 1, None, :]
    cv, ci = lax.top_k(cand.reshape(T, PEER_HEADS, K * K), K)
    i1 = jnp.take_along_axis(si[:, :, 0], ci // K, axis=-1)
    i2 = jnp.take_along_axis(si[:, :, 1], ci % K, axis=-1)
    idx = (i1 * PEER_NKEYS + i2).reshape(T, PEER_HEADS * K)
    g = jax.nn.softmax(cv, axis=-1).reshape(T, PEER_HEADS * K)
    n_c = T // PEER_CHUNK

    def chunk(args):
        xc, ic, gc = args
        u = u_tab[ic]
        act = jax.nn.gelu(jnp.einsum('td,ted->te', xc, u).astype(jnp.float32)) * gc
        v = v_tab[ic]
        return jnp.einsum('te,ted->td', act.astype(v.dtype), v)

    out = lax.map(chunk, (xt.reshape(n_c, PEER_CHUNK, D),
                          idx.reshape(n_c, PEER_CHUNK, PEER_HEADS * K),
                          g.reshape(n_c, PEER_CHUNK, PEER_HEADS * K)))
    return out.reshape(B, S, D).astype(x.dtype)


def setup_inputs(seed: int = 0) -> dict:
    key = jax.random.key(seed)
    ks = jax.random.split(key, 20)

    def nrm(k, shape, scale):
        return jax.random.normal(k, shape, jnp.float32) * scale

    x = nrm(ks[0], (BATCH, SEQ, D_MODEL), 1.0)
    norm_mix_g = 1.0 + nrm(ks[1], (DEPTH, D_MODEL), 0.02)
    w_in = nrm(ks[2], (DEPTH, D_MODEL, IN_WIDTH), D_MODEL ** -0.5)
    conv_w = nrm(ks[3], (DEPTH, CONV_WIDTH, LRU_WIDTH), CONV_WIDTH ** -0.5)
    conv_b = nrm(ks[4], (DEPTH, LRU_WIDTH), 0.01)
    lru_wa = nrm(ks[5], (DEPTH, LRU_BLOCKS, LRU_BLOCK_DIM, LRU_BLOCK_DIM), LRU_BLOCK_DIM ** -0.5)
    lru_ba = nrm(ks[6], (DEPTH, LRU_WIDTH), 0.01)
    lru_wx = nrm(ks[7], (DEPTH, LRU_BLOCKS, LRU_BLOCK_DIM, LRU_BLOCK_DIM), LRU_BLOCK_DIM ** -0.5)
    lru_bx = nrm(ks[8], (DEPTH, LRU_WIDTH), 0.01)
    a_c = jax.random.uniform(ks[9], (DEPTH, LRU_WIDTH), jnp.float32, 0.9, 0.999)
    p = a_c ** (1.0 / LRU_C)
    lru_lambda = jnp.log(p) - jnp.log1p(-p)
    w_branch = nrm(ks[10], (DEPTH, N_BRANCHES, ATTN_WIDTH, D_MODEL), ATTN_WIDTH ** -0.5)
    w_out = nrm(ks[11], (DEPTH, D_MODEL, D_MODEL), D_MODEL ** -0.5)
    rel_bias = nrm(ks[12], (REL_BUCKETS, ATTN_HEADS), 0.1)
    norm_ffn_g = 1.0 + nrm(ks[13], (DEPTH, D_MODEL), 0.02)
    peer_wq = nrm(ks[14], (DEPTH, D_MODEL, PEER_HEADS * PEER_DKEY), D_MODEL ** -0.5)
    peer_keys = nrm(ks[15], (DEPTH, PEER_HEADS, 2, PEER_NKEYS, PEER_DKEY // 2), (PEER_DKEY // 2) ** -0.5)
    peer_u = nrm(ks[16], (DEPTH, PEER_EXPERTS, D_MODEL), D_MODEL ** -0.5)
    peer_v = nrm(ks[17], (DEPTH, PEER_EXPERTS, D_MODEL), PEER_HEADS ** -0.5)
    norm_final_g = 1.0 + nrm(ks[18], (D_MODEL,), 0.02)
    return {'x': x, 'norm_mix_g': norm_mix_g, 'w_in': w_in, 'conv_w': conv_w, 'conv_b': conv_b,
            'lru_wa': lru_wa, 'lru_ba': lru_ba, 'lru_wx': lru_wx, 'lru_bx': lru_bx,
            'lru_lambda': lru_lambda, 'w_branch': w_branch, 'w_out': w_out, 'rel_bias': rel_bias,
            'norm_ffn_g': norm_ffn_g, 'peer_wq': peer_wq, 'peer_keys': peer_keys,
            'peer_u': peer_u, 'peer_v': peer_v, 'norm_final_g': norm_final_g}


def reference(x, norm_mix_g, w_in, conv_w, conv_b, lru_wa, lru_ba, lru_wx, lru_bx, lru_lambda,
              w_branch, w_out, rel_bias, norm_ffn_g, peer_wq, peer_keys, peer_u, peer_v, norm_final_g):
    B, S, _ = x.shape
    splits = [ATTN_WIDTH, 2 * ATTN_WIDTH, 3 * ATTN_WIDTH,
              3 * ATTN_WIDTH + LRU_WIDTH, 3 * ATTN_WIDTH + 2 * LRU_WIDTH]
    for l in range(DEPTH):
        h = rmsnorm(x, norm_mix_g[l])
        proj = h @ w_in[l]
        q, k, v, xr, yr, gl = jnp.split(proj, splits, axis=-1)
        q = q.reshape(B, S, ATTN_HEADS, HEAD_DIM).transpose(0, 2, 1, 3)
        k = k.reshape(B, S, ATTN_HEADS, HEAD_DIM).transpose(0, 2, 1, 3)
        v = v.reshape(B, S, ATTN_HEADS, HEAD_DIM).transpose(0, 2, 1, 3)
        o_att = moba_attention(q, k, v, rel_bias)
        xr = causal_conv(xr, conv_w[l], conv_b[l])
        o_rec = rg_lru(xr, lru_wa[l], lru_ba[l], lru_wx[l], lru_bx[l], lru_lambda[l]) * jax.nn.gelu(yr)
        branches = jnp.stack([o_att, o_rec], axis=2)
        pb = jnp.einsum('bsnc,ncd->bsnd', branches, w_branch[l])
        gates = jax.nn.sigmoid(gl.astype(jnp.float32)).reshape(B, S, N_BRANCHES, D_MODEL)
        merged = jnp.sum(gates.astype(pb.dtype) * pb, axis=2)
        x = x + merged @ w_out[l]
        x = x + peer(rmsnorm(x, norm_ffn_g[l]), peer_wq[l], peer_keys[l], peer_u[l], peer_v[l])
    return rmsnorm(x, norm_final_g)
```

```python
import functools
import math

import jax
import jax.numpy as jnp
from jax import lax
from jax.experimental import pallas as pl
from jax.experimental.pallas import tpu as pltpu

D_MODEL = 2048
ATTN_HEADS = 8
HEAD_DIM = 128
ATTN_WIDTH = ATTN_HEADS * HEAD_DIM
MOBA_BLOCK = 256
MOBA_TOPK = 3
LRU_WIDTH = 1024
LRU_BLOCKS = 8
LRU_BLOCK_DIM = LRU_WIDTH // LRU_BLOCKS
CONV_WIDTH = 4
LRU_C = 8.0
REL_BUCKETS = 32
REL_MAX_DIST = 128
PEER_HEADS = 8
PEER_NKEYS = 128
PEER_EXPERTS = PEER_NKEYS * PEER_NKEYS
PEER_DKEY = 256
PEER_TOPK = 16
EPS = 1e-6
NEG = -1e30
LOWEST = -3.0e38

V7X_VMEM_LIMIT_BYTES = 56 * 1024 * 1024

F32 = jnp.float32
BF16 = jnp.bfloat16

_NT_DIMS = (((1,), (1,)), ((), ()))


def _params(semantics):
    return pltpu.CompilerParams(dimension_semantics=semantics,
                                vmem_limit_bytes=V7X_VMEM_LIMIT_BYTES)


def _gelu_tanh(x):
    c = math.sqrt(2.0 / math.pi)
    return 0.5 * x * (1.0 + jnp.tanh(c * (x + 0.044715 * (x * x * x))))


def _rms(x, g):
    ms = jnp.mean(x * x, axis=-1, keepdims=True)
    return x * lax.rsqrt(ms + EPS) * g


def _rmsnorm_kernel(x_ref, g_ref, o_ref):
    o_ref[...] = _rms(x_ref[...], g_ref[...]).astype(o_ref.dtype)


def _rmsnorm(x, g, out_dtype, tm=512):
    t, d = x.shape
    return pl.pallas_call(
        _rmsnorm_kernel,
        grid=(t // tm,),
        in_specs=[pl.BlockSpec((tm, d), lambda i: (i, 0)),
                  pl.BlockSpec((1, d), lambda i: (0, 0))],
        out_specs=pl.BlockSpec((tm, d), lambda i: (i, 0)),
        out_shape=jax.ShapeDtypeStruct((t, d), out_dtype),
        compiler_params=_params(("parallel",)),
        name="rmsnorm",
    )(x, g.reshape(1, d))


def _mm_kernel(a_ref, b_ref, o_ref):
    o_ref[...] = jnp.dot(a_ref[...], b_ref[...],
                         preferred_element_type=F32).astype(o_ref.dtype)


def _matmul(a, b, out_dtype, name, tm=1024, tn=1024):
    m, k = a.shape
    _, n = b.shape
    return pl.pallas_call(
        _mm_kernel,
        grid=(m // tm, n // tn),
        in_specs=[pl.BlockSpec((tm, k), lambda i, j: (i, 0)),
                  pl.BlockSpec((k, tn), lambda i, j: (0, j))],
        out_specs=pl.BlockSpec((tm, tn), lambda i, j: (i, j)),
        out_shape=jax.ShapeDtypeStruct((m, n), out_dtype),
        compiler_params=_params(("parallel", "arbitrary")),
        name=name,
    )(a, b)


def _vt_kernel(w_ref, h_ref, o_ref):
    res = lax.dot_general(w_ref[...], h_ref[...], _NT_DIMS, preferred_element_type=F32)
    for t in range(o_ref.shape[0]):
        o_ref[t] = res[:, t * MOBA_BLOCK:(t + 1) * MOBA_BLOCK].astype(o_ref.dtype)


def _v_transposed(w_t, h, tm=1024):
    c, k = w_t.shape
    t, _ = h.shape
    nb = tm // MOBA_BLOCK
    return pl.pallas_call(
        _vt_kernel,
        grid=(t // tm,),
        in_specs=[pl.BlockSpec((c, k), lambda i: (0, 0)),
                  pl.BlockSpec((tm, k), lambda i: (i, 0))],
        out_specs=pl.BlockSpec((nb, c, MOBA_BLOCK), lambda i: (i, 0, 0)),
        out_shape=jax.ShapeDtypeStruct((t // MOBA_BLOCK, c, MOBA_BLOCK), BF16),
        compiler_params=_params(("parallel",)),
        name="v_transposed",
    )(w_t, h)


def _rel_bucket(dist):
    n = jnp.maximum(dist, 0)
    max_exact = REL_BUCKETS // 2
    nf = jnp.maximum(n, 1).astype(F32)
    large = max_exact + (jnp.log(nf / max_exact) / math.log(REL_MAX_DIST / max_exact)
                         * (REL_BUCKETS - max_exact)).astype(jnp.int32)
    large = jnp.minimum(large, REL_BUCKETS - 1)
    return jnp.where(n < max_exact, n, large)


def _bias_kernel(relb_ref, bucket_ref, o_ref):
    h = pl.program_id(0)
    for t in range(3):
        bk = bucket_ref[t]
        acc = jnp.zeros(bk.shape, F32)
        for b in range(REL_BUCKETS):
            acc = jnp.where(bk == b, relb_ref[b, h], acc)
        if t == 0:
            key = lax.broadcasted_iota(jnp.int32, bk.shape, 0)
            qry = lax.broadcasted_iota(jnp.int32, bk.shape, 1)
            acc = jnp.where(key <= qry, acc, NEG)
        o_ref[0, t] = acc


def _bias_tiles(rel_bias):
    assert REL_MAX_DIST <= MOBA_BLOCK
    key = jnp.arange(MOBA_BLOCK, dtype=jnp.int32)[:, None]
    qry = jnp.arange(MOBA_BLOCK, dtype=jnp.int32)[None, :]
    buckets = jnp.stack([_rel_bucket(t * MOBA_BLOCK + qry - key) for t in range(3)])
    return pl.pallas_call(
        _bias_kernel,
        grid=(ATTN_HEADS,),
        in_specs=[pl.BlockSpec(memory_space=pltpu.SMEM),
                  pl.BlockSpec((3, MOBA_BLOCK, MOBA_BLOCK), lambda h: (0, 0, 0))],
        out_specs=pl.BlockSpec((1, 3, MOBA_BLOCK, MOBA_BLOCK), lambda h: (h, 0, 0, 0)),
        out_shape=jax.ShapeDtypeStruct((ATTN_HEADS, 3, MOBA_BLOCK, MOBA_BLOCK), F32),
        compiler_params=_params(("parallel",)),
        name="attn_bias_tiles",
    )(rel_bias.astype(F32), buckets)


def _attn_kernel(q_ref, k_ref, vt_ref, bias_ref, o_ref, kmean_ref, sel_ref, *, nb):
    i = pl.program_id(2)
    scale = HEAD_DIM ** -0.5

    @pl.when(i == 0)
    def _():
        for n in range(nb):
            kb = k_ref[n * MOBA_BLOCK:(n + 1) * MOBA_BLOCK, :].astype(F32)
            kmean_ref[n:n + 1, :] = jnp.mean(kb, axis=0, keepdims=True)

    q = q_ref[...]
    gate = lax.dot_general(kmean_ref[...].astype(BF16), q, _NT_DIMS,
                           preferred_element_type=F32)
    blk = lax.broadcasted_iota(jnp.int32, gate.shape, 0)
    rank = jnp.zeros(gate.shape, F32)
    for m in range(nb):
        gm = gate[m:m + 1, :]
        beats = jnp.where(gm > gate, 1.0, jnp.where(gm == gate, jnp.where(m < blk, 1.0, 0.0), 0.0))
        rank = rank + beats * jnp.where(m < i, 1.0, 0.0)
    sel_ref[...] = jnp.where(blk < i, jnp.where(rank < MOBA_TOPK, 1.0, 0.0), 0.0)

    def scores(j, tile):
        koff = pl.multiple_of(j * MOBA_BLOCK, MOBA_BLOCK)
        kj = k_ref[pl.ds(koff, MOBA_BLOCK), :]
        s = lax.dot_general(kj, q, _NT_DIMS, preferred_element_type=F32)
        return s * scale + bias_ref[0, tile]

    s = scores(i, 0)
    m0 = jnp.max(s, axis=0, keepdims=True)
    p = jnp.exp(s - m0)
    l0 = jnp.sum(p, axis=0, keepdims=True)
    acc0 = jnp.dot(vt_ref[i], p.astype(BF16), preferred_element_type=F32)

    def body(j, carry):
        m_i, l_i, acc = carry
        s = scores(j, jnp.minimum(i - j, 2))
        s = jnp.where(sel_ref[pl.ds(j, 1), :] > 0.5, s, NEG)
        m_new = jnp.maximum(m_i, jnp.max(s, axis=0, keepdims=True))
        alpha = jnp.exp(m_i - m_new)
        p = jnp.exp(s - m_new)
        l_new = alpha * l_i + jnp.sum(p, axis=0, keepdims=True)
        acc = alpha * acc + jnp.dot(vt_ref[j], p.astype(BF16), preferred_element_type=F32)
        return m_new, l_new, acc

    _, l_f, acc_f = lax.fori_loop(0, i, body, (m0, l0, acc0))
    o_ref[...] = (acc_f / l_f).T.astype(o_ref.dtype)


def _attention(qk, vt3, bias, batch, seq):
    nb = seq // MOBA_BLOCK
    t = batch * seq
    return pl.pallas_call(
        functools.partial(_attn_kernel, nb=nb),
        grid=(batch, ATTN_HEADS, nb),
        in_specs=[
            pl.BlockSpec((MOBA_BLOCK, HEAD_DIM), lambda b, h, i: (b * nb + i, h)),
            pl.BlockSpec((seq, HEAD_DIM), lambda b, h, i: (b, ATTN_HEADS + h)),
            pl.BlockSpec((nb, HEAD_DIM, MOBA_BLOCK), lambda b, h, i: (b, h, 0)),
            pl.BlockSpec((1, 3, MOBA_BLOCK, MOBA_BLOCK), lambda b, h, i: (h, 0, 0, 0)),
        ],
        out_specs=pl.BlockSpec((MOBA_BLOCK, HEAD_DIM), lambda b, h, i: (b * nb + i, h)),
        out_shape=jax.ShapeDtypeStruct((t, ATTN_WIDTH), BF16),
        scratch_shapes=[pltpu.VMEM((nb, HEAD_DIM), F32),
                        pltpu.VMEM((nb, MOBA_BLOCK), F32)],
        compiler_params=_params(("parallel", "parallel", "arbitrary")),
        name="moba_attention",
    )(qk, qk, vt3, bias)


def _rglru_kernel(xr_ref, yr_ref, cw_ref, cb_ref, wa_ref, wx_ref, ba_ref, bx_ref, lam_ref,
                  o_ref, tail_ref, h_ref):
    c = pl.program_id(1)
    tc = xr_ref.shape[0]

    @pl.when(c == 0)
    def _():
        tail_ref[...] = jnp.zeros(tail_ref.shape, F32)
        h_ref[...] = jnp.zeros(h_ref.shape, F32)

    x = xr_ref[...]
    xfull = jnp.concatenate([tail_ref[...], x], axis=0)
    conv = cb_ref[...]
    for k in range(CONV_WIDTH):
        off = 8 - (CONV_WIDTH - 1) + k
        conv = conv + cw_ref[k:k + 1, :] * xfull[off:off + tc, :]
    tail_ref[...] = x[tc - 8:, :]

    xb16 = conv.astype(BF16)
    pre_a, pre_x = [], []
    for g in range(LRU_BLOCKS):
        xs = xb16[:, g * LRU_BLOCK_DIM:(g + 1) * LRU_BLOCK_DIM]
        pre_a.append(jnp.dot(xs, wa_ref[g], preferred_element_type=F32))
        pre_x.append(jnp.dot(xs, wx_ref[g], preferred_element_type=F32))
    r_gate = jax.nn.sigmoid(jnp.concatenate(pre_a, axis=1) + ba_ref[...])
    i_gate = jax.nn.sigmoid(jnp.concatenate(pre_x, axis=1) + bx_ref[...])
    z = -lam_ref[...]
    softplus = jnp.maximum(z, 0.0) + jnp.log1p(jnp.exp(-jnp.abs(z)))
    log_a = (-LRU_C) * r_gate * softplus
    a = jnp.exp(log_a)
    th = jnp.tanh(log_a)
    u = jnp.sqrt(-2.0 * th / (1.0 - th)) * (i_gate * conv)

    row = lax.broadcasted_iota(jnp.int32, a.shape, 0)
    s = 1
    while s < tc:
        a_sh = pltpu.roll(a, s, axis=0)
        u_sh = pltpu.roll(u, s, axis=0)
        valid = row >= s
        u = jnp.where(valid, a * u_sh + u, u)
        a = jnp.where(valid, a * a_sh, a)
        s *= 2
    hh = a * h_ref[...] + u
    h_ref[...] = hh[tc - 1:tc, :]
    o_ref[...] = (hh * _gelu_tanh(yr_ref[...])).astype(o_ref.dtype)


def _rglru(r, conv_w, conv_b, wa, wx, ba, bx, lam, batch, seq, tc=256):
    t = batch * seq
    nc = seq // tc
    w = LRU_WIDTH
    row = lambda v: v.reshape(1, w).astype(F32)
    full2 = lambda shape: pl.BlockSpec(shape, lambda b, c: (0,) * len(shape))
    return pl.pallas_call(
        _rglru_kernel,
        grid=(batch, nc),
        in_specs=[pl.BlockSpec((tc, w), lambda b, c: (b * nc + c, 0)),
                  pl.BlockSpec((tc, w), lambda b, c: (b * nc + c, 1)),
                  full2((CONV_WIDTH, w)), full2((1, w)),
                  full2((LRU_BLOCKS, LRU_BLOCK_DIM, LRU_BLOCK_DIM)),
                  full2((LRU_BLOCKS, LRU_BLOCK_DIM, LRU_BLOCK_DIM)),
                  full2((1, w)), full2((1, w)), full2((1, w))],
        out_specs=pl.BlockSpec((tc, w), lambda b, c: (b * nc + c, 0)),
        out_shape=jax.ShapeDtypeStruct((t, w), BF16),
        scratch_shapes=[pltpu.VMEM((8, w), F32), pltpu.VMEM((1, w), F32)],
        compiler_params=_params(("parallel", "arbitrary")),
        name="rglru",
    )(r, r, conv_w.astype(F32), row(conv_b), wa.astype(BF16), wx.astype(BF16),
      row(ba), row(bx), row(lam))


def _merge_kernel(oa_ref, or_ref, g0_ref, g1_ref, wb0_ref, wb1_ref, wo_ref, x_ref, gn_ref,
                  x1_ref, xn_ref):
    pb0 = jnp.dot(oa_ref[...], wb0_ref[...], preferred_element_type=F32)
    pb1 = jnp.dot(or_ref[...], wb1_ref[...], preferred_element_type=F32)
    merged = jax.nn.sigmoid(g0_ref[...]) * pb0 + jax.nn.sigmoid(g1_ref[...]) * pb1
    x1 = x_ref[...] + jnp.dot(merged.astype(BF16), wo_ref[...], preferred_element_type=F32)
    x1_ref[...] = x1
    xn_ref[...] = _rms(x1, gn_ref[...]).astype(xn_ref.dtype)


def _merge_out(o_att, o_rec, gl, wb0, wb1, w_out, x, g_ffn, tm=256):
    t, d = x.shape
    cw = o_att.shape[1]
    resident = lambda shape: pl.BlockSpec(shape, lambda i: (0, 0), pipeline_mode=pl.Buffered(1))
    return pl.pallas_call(
        _merge_kernel,
        grid=(t // tm,),
        in_specs=[pl.BlockSpec((tm, cw), lambda i: (i, 0)),
                  pl.BlockSpec((tm, cw), lambda i: (i, 0)),
                  pl.BlockSpec((tm, d), lambda i: (i, 0)),
                  pl.BlockSpec((tm, d), lambda i: (i, 1)),
                  resident((cw, d)), resident((cw, d)), resident((d, d)),
                  pl.BlockSpec((tm, d), lambda i: (i, 0)),
                  pl.BlockSpec((1, d), lambda i: (0, 0))],
        out_specs=[pl.BlockSpec((tm, d), lambda i: (i, 0)),
                   pl.BlockSpec((tm, d), lambda i: (i, 0))],
        out_shape=[jax.ShapeDtypeStruct((t, d), F32), jax.ShapeDtypeStruct((t, d), BF16)],
        compiler_params=_params(("parallel",)),
        name="merge_out",
    )(o_att, o_rec, gl, gl, wb0, wb1, w_out, x, g_ffn.reshape(1, d).astype(F32))


def _peer_scores_kernel(q_ref, keys_ref, s_ref, st_ref, vals_ref, cand_ref):
    nhc = keys_ref.shape[0]
    k = PEER_TOPK
    for hc in range(nhc):
        qs = q_ref[:, hc * PEER_NKEYS:(hc + 1) * PEER_NKEYS]
        s_ref[hc] = lax.dot_general(keys_ref[hc], qs, _NT_DIMS, preferred_element_type=F32)

    def top_values(hc, carry):
        x = s_ref[hc]
        for r in range(k):
            m = jnp.max(x, axis=0, keepdims=True)
            vals_ref[hc, r:r + 1, :] = m
            x = jnp.where(x == m, LOWEST, x)
        return carry

    lax.fori_loop(0, nhc, top_values, 0)

    def head_stats(h, carry):
        v1 = vals_ref[2 * h]
        v2 = vals_ref[2 * h + 1]
        for a in range(k):
            cand_ref[a * k:(a + 1) * k, :] = v1[a:a + 1, :] + v2
        cand = cand_ref[...]
        x = cand
        tau = None
        for r in range(k):
            tau = jnp.max(x, axis=0, keepdims=True)
            x = jnp.where(x == tau, LOWEST, x)
        cmax = v1[0:1, :] + v2[0:1, :]
        z = jnp.sum(jnp.where(cand >= tau, jnp.exp(cand - cmax), 0.0), axis=0, keepdims=True)
        st_ref[h, 0:1, :] = tau
        st_ref[h, 1:2, :] = v1[0:1, :]
        st_ref[h, 2:3, :] = v2[0:1, :]
        st_ref[h, 3:4, :] = 1.0 / z
        st_ref[h, 4:8, :] = jnp.zeros((4, tau.shape[1]), F32)
        return carry

    lax.fori_loop(0, nhc // 2, head_stats, 0)


def _peer_scores(q, keys, tm=256):
    t, _ = q.shape
    nhc = keys.shape[0]
    return pl.pallas_call(
        _peer_scores_kernel,
        grid=(t // tm,),
        in_specs=[pl.BlockSpec((tm, nhc * PEER_NKEYS), lambda i: (i, 0)),
                  pl.BlockSpec(keys.shape, lambda i: (0, 0, 0))],
        out_specs=[pl.BlockSpec((nhc, PEER_NKEYS, tm), lambda i: (0, 0, i)),
                   pl.BlockSpec((PEER_HEADS, 8, tm), lambda i: (0, 0, i))],
        out_shape=[jax.ShapeDtypeStruct((nhc, PEER_NKEYS, t), F32),
                   jax.ShapeDtypeStruct((PEER_HEADS, 8, t), F32)],
        scratch_shapes=[pltpu.VMEM((nhc, PEER_TOPK, tm), F32),
                        pltpu.VMEM((PEER_TOPK * PEER_TOPK, tm), F32)],
        compiler_params=_params(("parallel",)),
        name="peer_scores",
    )(q, keys)


def _peer_dense_kernel(xn_ref, s_ref, st_ref, u_ref, vt_ref, x1_ref, g_ref, y_ref,
                       acc_ref, e1_ref, e2_ref, p_ref):
    j = pl.program_id(1)
    tn = u_ref.shape[0]
    rows = tn // PEER_NKEYS

    @pl.when(j == 0)
    def _():
        acc_ref[...] = jnp.zeros(acc_ref.shape, F32)
        for h in range(PEER_HEADS):
            e1_ref[h] = jnp.exp(s_ref[2 * h] - st_ref[h, 1:2, :])
            e2_ref[h] = jnp.exp(s_ref[2 * h + 1] - st_ref[h, 2:3, :]) * st_ref[h, 3:4, :]

    act = lax.dot_general(u_ref[...], xn_ref[...], _NT_DIMS, preferred_element_type=F32)
    for r in range(rows):
        i1 = j * rows + r
        w = jnp.zeros((PEER_NKEYS, act.shape[1]), F32)
        for h in range(PEER_HEADS):
            s1_row = s_ref[2 * h, pl.ds(i1, 1), :]
            e1_row = e1_ref[h, pl.ds(i1, 1), :]
            cand = s_ref[2 * h + 1] + s1_row
            w = w + jnp.where(cand >= st_ref[h, 0:1, :], e2_ref[h] * e1_row, 0.0)
        a_r = act[r * PEER_NKEYS:(r + 1) * PEER_NKEYS, :]
        p_ref[r * PEER_NKEYS:(r + 1) * PEER_NKEYS, :] = (_gelu_tanh(a_r) * w).astype(p_ref.dtype)
    acc_ref[...] += jnp.dot(vt_ref[...], p_ref[...], preferred_element_type=F32)

    @pl.when(j == pl.num_programs(1) - 1)
    def _():
        x2 = x1_ref[...] + acc_ref[...].T
        y_ref[...] = _rms(x2, g_ref[...])


def _peer_dense(xn, s_t, stats, u_bf, vt_bf, x1, g_final, tm=512, tn=512):
    t, d = xn.shape
    e = u_bf.shape[0]
    nhc = s_t.shape[0]
    return pl.pallas_call(
        _peer_dense_kernel,
        grid=(t // tm, e // tn),
        in_specs=[pl.BlockSpec((tm, d), lambda i, j: (i, 0)),
                  pl.BlockSpec((nhc, PEER_NKEYS, tm), lambda i, j: (0, 0, i)),
                  pl.BlockSpec((PEER_HEADS, 8, tm), lambda i, j: (0, 0, i)),
                  pl.BlockSpec((tn, d), lambda i, j: (j, 0)),
                  pl.BlockSpec((d, tn), lambda i, j: (0, j)),
                  pl.BlockSpec((tm, d), lambda i, j: (i, 0)),
                  pl.BlockSpec((1, d), lambda i, j: (0, 0))],
        out_specs=pl.BlockSpec((tm, d), lambda i, j: (i, 0)),
        out_shape=jax.ShapeDtypeStruct((t, d), F32),
        scratch_shapes=[pltpu.VMEM((d, tm), F32),
                        pltpu.VMEM((PEER_HEADS, PEER_NKEYS, tm), F32),
                        pltpu.VMEM((PEER_HEADS, PEER_NKEYS, tm), F32),
                        pltpu.VMEM((tn, tm), BF16)],
        compiler_params=_params(("parallel", "arbitrary")),
        name="peer_dense",
    )(xn, s_t, stats, u_bf, vt_bf, x1, g_final.reshape(1, d).astype(F32))


def kernel(x, norm_mix_g, w_in, conv_w, conv_b, lru_wa, lru_ba, lru_wx, lru_bx, lru_lambda,
           w_branch, w_out, rel_bias, norm_ffn_g, peer_wq, peer_keys, peer_u, peer_v, norm_final_g):
    batch, seq, d = x.shape
    t = batch * seq
    assert w_in.shape[0] == 1, "single-layer trunk: the final rmsnorm is fused into the PEER epilogue"
    xt = x.reshape(t, d)
    bias = _bias_tiles(rel_bias)
    a_w = ATTN_WIDTH
    w_l = w_in[0]
    h = _rmsnorm(xt, norm_mix_g[0], BF16)
    qk = _matmul(h, w_l[:, :2 * a_w].astype(BF16), BF16, "proj_qk")
    vt3 = _v_transposed(w_l[:, 2 * a_w:3 * a_w].T.astype(BF16), h)
    r = _matmul(h, w_l[:, 3 * a_w:3 * a_w + 2 * LRU_WIDTH].astype(BF16), F32, "proj_rec")
    gl = _matmul(h, w_l[:, 3 * a_w + 2 * LRU_WIDTH:].astype(BF16), F32, "proj_gate")
    o_att = _attention(qk, vt3, bias, batch, seq)
    o_rec = _rglru(r, conv_w[0], conv_b[0], lru_wa[0], lru_wx[0], lru_ba[0], lru_bx[0],
                   lru_lambda[0], batch, seq)
    x1, xn = _merge_out(o_att, o_rec, gl, w_branch[0, 0].astype(BF16),
                        w_branch[0, 1].astype(BF16), w_out[0].astype(BF16), xt, norm_ffn_g[0])
    q = _matmul(xn, peer_wq[0].astype(BF16), BF16, "peer_query")
    keys = peer_keys[0].reshape(PEER_HEADS * 2, PEER_NKEYS, PEER_DKEY // 2).astype(BF16)
    s_t, stats = _peer_scores(q, keys)
    y = _peer_dense(xn, s_t, stats, peer_u[0].astype(BF16), peer_v[0].T.astype(BF16),
                    x1, norm_final_g)
    return y.reshape(batch, seq, d)
```

```python
import functools
import math

import jax
import jax.numpy as jnp
from jax import lax
from jax.experimental import pallas as pl
from jax.experimental.pallas import tpu as pltpu

D_MODEL = 2048
ATTN_HEADS = 8
HEAD_DIM = 128
ATTN_WIDTH = ATTN_HEADS * HEAD_DIM
MOBA_BLOCK = 256
MOBA_TOPK = 3
LRU_WIDTH = 1024
LRU_BLOCKS = 8
LRU_BLOCK_DIM = LRU_WIDTH // LRU_BLOCKS
CONV_WIDTH = 4
LRU_C = 8.0
REL_BUCKETS = 32
REL_MAX_DIST = 128
PEER_HEADS = 8
PEER_NKEYS = 128
PEER_EXPERTS = PEER_NKEYS * PEER_NKEYS
PEER_DKEY = 256
PEER_TOPK = 16
EPS = 1e-6
NEG = -1e30
LOWEST = -3.0e38

V7X_VMEM_LIMIT_BYTES = 56 * 1024 * 1024

F32 = jnp.float32
BF16 = jnp.bfloat16

_NT_DIMS = (((1,), (1,)), ((), ()))


def _params(semantics, flags=None):
    return pltpu.CompilerParams(dimension_semantics=semantics,
                                vmem_limit_bytes=V7X_VMEM_LIMIT_BYTES, flags=flags)


def _gelu_tanh(x):
    c = math.sqrt(2.0 / math.pi)
    return (0.5 * x) * (1.0 + jnp.tanh(x * (c + (0.044715 * c) * (x * x))))


def _rms(x, g):
    ms = jnp.mean(x * x, axis=-1, keepdims=True)
    return x * lax.rsqrt(ms + EPS) * g


def _rmsnorm_kernel(x_ref, g_ref, o_ref):
    o_ref[...] = _rms(x_ref[...], g_ref[...]).astype(o_ref.dtype)


def _rmsnorm(x, g, out_dtype, tm=512):
    t, d = x.shape
    return pl.pallas_call(
        _rmsnorm_kernel,
        grid=(t // tm,),
        in_specs=[pl.BlockSpec((tm, d), lambda i: (i, 0)),
                  pl.BlockSpec((1, d), lambda i: (0, 0))],
        out_specs=pl.BlockSpec((tm, d), lambda i: (i, 0)),
        out_shape=jax.ShapeDtypeStruct((t, d), out_dtype),
        compiler_params=_params(("parallel",)),
        name="rmsnorm",
    )(x, g.reshape(1, d))


def _mm_kernel(a_ref, b_ref, o_ref):
    o_ref[...] = jnp.dot(a_ref[...], b_ref[...],
                         preferred_element_type=F32).astype(o_ref.dtype)


def _matmul(a, b, out_dtype, name, tm=1024, tn=1024):
    m, k = a.shape
    _, n = b.shape
    return pl.pallas_call(
        _mm_kernel,
        grid=(m // tm, n // tn),
        in_specs=[pl.BlockSpec((tm, k), lambda i, j: (i, 0)),
                  pl.BlockSpec((k, tn), lambda i, j: (0, j))],
        out_specs=pl.BlockSpec((tm, tn), lambda i, j: (i, j)),
        out_shape=jax.ShapeDtypeStruct((m, n), out_dtype),
        compiler_params=_params(("parallel", "arbitrary")),
        name=name,
    )(a, b)


def _vt_kernel(w_ref, h_ref, o_ref):
    res = lax.dot_general(w_ref[...], h_ref[...], _NT_DIMS, preferred_element_type=F32)
    for t in range(o_ref.shape[0]):
        o_ref[t] = res[:, t * MOBA_BLOCK:(t + 1) * MOBA_BLOCK].astype(o_ref.dtype)


def _v_transposed(w_t, h, tm=1024):
    c, k = w_t.shape
    t, _ = h.shape
    nb = tm // MOBA_BLOCK
    return pl.pallas_call(
        _vt_kernel,
        grid=(t // tm,),
        in_specs=[pl.BlockSpec((c, k), lambda i: (0, 0)),
                  pl.BlockSpec((tm, k), lambda i: (i, 0))],
        out_specs=pl.BlockSpec((nb, c, MOBA_BLOCK), lambda i: (i, 0, 0)),
        out_shape=jax.ShapeDtypeStruct((t // MOBA_BLOCK, c, MOBA_BLOCK), BF16),
        compiler_params=_params(("parallel",)),
        name="v_transposed",
    )(w_t, h)


def _rel_bucket(dist):
    n = jnp.maximum(dist, 0)
    max_exact = REL_BUCKETS // 2
    nf = jnp.maximum(n, 1).astype(F32)
    large = max_exact + (jnp.log(nf / max_exact) / math.log(REL_MAX_DIST / max_exact)
                         * (REL_BUCKETS - max_exact)).astype(jnp.int32)
    large = jnp.minimum(large, REL_BUCKETS - 1)
    return jnp.where(n < max_exact, n, large)


def _bias_kernel(relb_ref, bucket_ref, o_ref):
    h = pl.program_id(0)
    for t in range(3):
        bk = bucket_ref[t]
        acc = jnp.zeros(bk.shape, F32)
        for b in range(REL_BUCKETS):
            acc = jnp.where(bk == b, relb_ref[b, h], acc)
        if t == 0:
            key = lax.broadcasted_iota(jnp.int32, bk.shape, 0)
            qry = lax.broadcasted_iota(jnp.int32, bk.shape, 1)
            acc = jnp.where(key <= qry, acc, NEG)
        o_ref[0, t] = acc


def _bias_tiles(rel_bias):
    assert REL_MAX_DIST <= MOBA_BLOCK
    key = jnp.arange(MOBA_BLOCK, dtype=jnp.int32)[:, None]
    qry = jnp.arange(MOBA_BLOCK, dtype=jnp.int32)[None, :]
    buckets = jnp.stack([_rel_bucket(t * MOBA_BLOCK + qry - key) for t in range(3)])
    return pl.pallas_call(
        _bias_kernel,
        grid=(ATTN_HEADS,),
        in_specs=[pl.BlockSpec(memory_space=pltpu.SMEM),
                  pl.BlockSpec((3, MOBA_BLOCK, MOBA_BLOCK), lambda h: (0, 0, 0))],
        out_specs=pl.BlockSpec((1, 3, MOBA_BLOCK, MOBA_BLOCK), lambda h: (h, 0, 0, 0)),
        out_shape=jax.ShapeDtypeStruct((ATTN_HEADS, 3, MOBA_BLOCK, MOBA_BLOCK), F32),
        compiler_params=_params(("parallel",)),
        name="attn_bias_tiles",
    )(rel_bias.astype(F32), buckets)


def _attn_kernel(q_ref, k_ref, vt_ref, bias_ref, o_ref, kmean_ref, sel_ref, *, nb):
    i = pl.program_id(2)
    scale = HEAD_DIM ** -0.5

    @pl.when(i == 0)
    def _():
        for n in range(nb):
            kb = k_ref[n * MOBA_BLOCK:(n + 1) * MOBA_BLOCK, :].astype(F32)
            kmean_ref[n:n + 1, :] = jnp.mean(kb, axis=0, keepdims=True)

    q = q_ref[...]
    gate = lax.dot_general(kmean_ref[...].astype(BF16), q, _NT_DIMS,
                           preferred_element_type=F32)
    blk = lax.broadcasted_iota(jnp.int32, gate.shape, 0)
    rank = jnp.zeros(gate.shape, F32)
    for m in range(nb):
        gm = gate[m:m + 1, :]
        beats = jnp.where(gm > gate, 1.0, jnp.where(gm == gate, jnp.where(m < blk, 1.0, 0.0), 0.0))
        rank = rank + beats * jnp.where(m < i, 1.0, 0.0)
    sel_ref[...] = jnp.where(blk < i, jnp.where(rank < MOBA_TOPK, 1.0, 0.0), 0.0)

    def scores(j, tile):
        koff = pl.multiple_of(j * MOBA_BLOCK, MOBA_BLOCK)
        kj = k_ref[pl.ds(koff, MOBA_BLOCK), :]
        s = lax.dot_general(kj, q, _NT_DIMS, preferred_element_type=F32)
        return s * scale + bias_ref[0, tile]

    s = scores(i, 0)
    m0 = jnp.max(s, axis=0, keepdims=True)
    p = jnp.exp(s - m0)
    l0 = jnp.sum(p, axis=0, keepdims=True)
    acc0 = jnp.dot(vt_ref[i], p.astype(BF16), preferred_element_type=F32)

    def body(j, carry):
        m_i, l_i, acc = carry
        s = scores(j, jnp.minimum(i - j, 2))
        s = jnp.where(sel_ref[pl.ds(j, 1), :] > 0.5, s, NEG)
        m_new = jnp.maximum(m_i, jnp.max(s, axis=0, keepdims=True))
        alpha = jnp.exp(m_i - m_new)
        p = jnp.exp(s - m_new)
        l_new = alpha * l_i + jnp.sum(p, axis=0, keepdims=True)
        acc = alpha * acc + jnp.dot(vt_ref[j], p.astype(BF16), preferred_element_type=F32)
        return m_new, l_new, acc

    _, l_f, acc_f = lax.fori_loop(0, i, body, (m0, l0, acc0))
    o_ref[...] = (acc_f / l_f).T.astype(o_ref.dtype)


def _attention(qk, vt3, bias, batch, seq):
    nb = seq // MOBA_BLOCK
    t = batch * seq
    return pl.pallas_call(
        functools.partial(_attn_kernel, nb=nb),
        grid=(batch, ATTN_HEADS, nb),
        in_specs=[
            pl.BlockSpec((MOBA_BLOCK, HEAD_DIM), lambda b, h, i: (b * nb + i, h)),
            pl.BlockSpec((seq, HEAD_DIM), lambda b, h, i: (b, ATTN_HEADS + h)),
            pl.BlockSpec((nb, HEAD_DIM, MOBA_BLOCK), lambda b, h, i: (b, h, 0)),
            pl.BlockSpec((1, 3, MOBA_BLOCK, MOBA_BLOCK), lambda b, h, i: (h, 0, 0, 0)),
        ],
        out_specs=pl.BlockSpec((MOBA_BLOCK, HEAD_DIM), lambda b, h, i: (b * nb + i, h)),
        out_shape=jax.ShapeDtypeStruct((t, ATTN_WIDTH), BF16),
        scratch_shapes=[pltpu.VMEM((nb, HEAD_DIM), F32),
                        pltpu.VMEM((nb, MOBA_BLOCK), F32)],
        compiler_params=_params(("parallel", "parallel", "arbitrary")),
        name="moba_attention",
    )(qk, qk, vt3, bias)


def _rglru_kernel(xr_ref, yr_ref, cw_ref, cb_ref, wa_ref, wx_ref, ba_ref, bx_ref, lam_ref,
                  o_ref, tail_ref, h_ref):
    c = pl.program_id(1)
    tc = xr_ref.shape[0]

    @pl.when(c == 0)
    def _():
        tail_ref[...] = jnp.zeros(tail_ref.shape, F32)
        h_ref[...] = jnp.zeros(h_ref.shape, F32)

    x = xr_ref[...]
    xfull = jnp.concatenate([tail_ref[...], x], axis=0)
    conv = cb_ref[...]
    for k in range(CONV_WIDTH):
        off = 8 - (CONV_WIDTH - 1) + k
        conv = conv + cw_ref[k:k + 1, :] * xfull[off:off + tc, :]
    tail_ref[...] = x[tc - 8:, :]

    xb16 = conv.astype(BF16)
    pre_a, pre_x = [], []
    for g in range(LRU_BLOCKS):
        xs = xb16[:, g * LRU_BLOCK_DIM:(g + 1) * LRU_BLOCK_DIM]
        pre_a.append(jnp.dot(xs, wa_ref[g], preferred_element_type=F32))
        pre_x.append(jnp.dot(xs, wx_ref[g], preferred_element_type=F32))
    r_gate = jax.nn.sigmoid(jnp.concatenate(pre_a, axis=1) + ba_ref[...])
    i_gate = jax.nn.sigmoid(jnp.concatenate(pre_x, axis=1) + bx_ref[...])
    z = -lam_ref[...]
    softplus = jnp.maximum(z, 0.0) + jnp.log1p(jnp.exp(-jnp.abs(z)))
    log_a = (-LRU_C) * r_gate * softplus
    a = jnp.exp(log_a)
    th = jnp.tanh(log_a)
    u = jnp.sqrt(-2.0 * th / (1.0 - th)) * (i_gate * conv)

    row = lax.broadcasted_iota(jnp.int32, a.shape, 0)
    s = 1
    while s < tc:
        a_sh = pltpu.roll(a, s, axis=0)
        u_sh = pltpu.roll(u, s, axis=0)
        valid = row >= s
        u = jnp.where(valid, a * u_sh + u, u)
        a = jnp.where(valid, a * a_sh, a)
        s *= 2
    hh = a * h_ref[...] + u
    h_ref[...] = hh[tc - 1:tc, :]
    o_ref[...] = (hh * _gelu_tanh(yr_ref[...])).astype(o_ref.dtype)


def _rglru(r, conv_w, conv_b, wa, wx, ba, bx, lam, batch, seq, tc=256):
    t = batch * seq
    nc = seq // tc
    w = LRU_WIDTH
    row = lambda v: v.reshape(1, w).astype(F32)
    full2 = lambda shape: pl.BlockSpec(shape, lambda b, c: (0,) * len(shape))
    return pl.pallas_call(
        _rglru_kernel,
        grid=(batch, nc),
        in_specs=[pl.BlockSpec((tc, w), lambda b, c: (b * nc + c, 0)),
                  pl.BlockSpec((tc, w), lambda b, c: (b * nc + c, 1)),
                  full2((CONV_WIDTH, w)), full2((1, w)),
                  full2((LRU_BLOCKS, LRU_BLOCK_DIM, LRU_BLOCK_DIM)),
                  full2((LRU_BLOCKS, LRU_BLOCK_DIM, LRU_BLOCK_DIM)),
                  full2((1, w)), full2((1, w)), full2((1, w))],
        out_specs=pl.BlockSpec((tc, w), lambda b, c: (b * nc + c, 0)),
        out_shape=jax.ShapeDtypeStruct((t, w), BF16),
        scratch_shapes=[pltpu.VMEM((8, w), F32), pltpu.VMEM((1, w), F32)],
        compiler_params=_params(("parallel", "arbitrary")),
        name="rglru",
    )(r, r, conv_w.astype(F32), row(conv_b), wa.astype(BF16), wx.astype(BF16),
      row(ba), row(bx), row(lam))


def _merge_kernel(oa_ref, or_ref, g0_ref, g1_ref, wb0_ref, wb1_ref, wo_ref, x_ref, gn_ref,
                  x1_ref, xn_ref):
    pb0 = jnp.dot(oa_ref[...], wb0_ref[...], preferred_element_type=F32)
    pb1 = jnp.dot(or_ref[...], wb1_ref[...], preferred_element_type=F32)
    merged = jax.nn.sigmoid(g0_ref[...]) * pb0 + jax.nn.sigmoid(g1_ref[...]) * pb1
    x1 = x_ref[...] + jnp.dot(merged.astype(BF16), wo_ref[...], preferred_element_type=F32)
    x1_ref[...] = x1
    xn_ref[...] = _rms(x1, gn_ref[...]).astype(xn_ref.dtype)


def _merge_out(o_att, o_rec, gl, wb0, wb1, w_out, x, g_ffn, tm=256):
    t, d = x.shape
    cw = o_att.shape[1]
    resident = lambda shape: pl.BlockSpec(shape, lambda i: (0, 0), pipeline_mode=pl.Buffered(1))
    return pl.pallas_call(
        _merge_kernel,
        grid=(t // tm,),
        in_specs=[pl.BlockSpec((tm, cw), lambda i: (i, 0)),
                  pl.BlockSpec((tm, cw), lambda i: (i, 0)),
                  pl.BlockSpec((tm, d), lambda i: (i, 0)),
                  pl.BlockSpec((tm, d), lambda i: (i, 1)),
                  resident((cw, d)), resident((cw, d)), resident((d, d)),
                  pl.BlockSpec((tm, d), lambda i: (i, 0)),
                  pl.BlockSpec((1, d), lambda i: (0, 0))],
        out_specs=[pl.BlockSpec((tm, d), lambda i: (i, 0)),
                   pl.BlockSpec((tm, d), lambda i: (i, 0))],
        out_shape=[jax.ShapeDtypeStruct((t, d), F32), jax.ShapeDtypeStruct((t, d), BF16)],
        compiler_params=_params(("parallel",)),
        name="merge_out",
    )(o_att, o_rec, gl, gl, wb0, wb1, w_out, x, g_ffn.reshape(1, d).astype(F32))


def _peer_scores_kernel(q_ref, keys_ref, thr_ref, e1_ref, s2_ref, e2_ref, s_ref, vals_ref, cand_ref):
    nhc = keys_ref.shape[0]
    k = PEER_TOPK
    for hc in range(nhc):
        qs = q_ref[:, hc * PEER_NKEYS:(hc + 1) * PEER_NKEYS]
        s_ref[hc] = lax.dot_general(keys_ref[hc], qs, _NT_DIMS, preferred_element_type=F32)

    def top_values(hc, carry):
        x = s_ref[hc]
        for r in range(k + 1):
            m = jnp.max(x, axis=0, keepdims=True)
            vals_ref[hc, r:r + 1, :] = m
            x = jnp.where(x == m, LOWEST, x)
        return carry

    lax.fori_loop(0, nhc, top_values, 0)

    def head_stats(h, carry):
        v1 = vals_ref[2 * h, 0:k, :]
        v2 = vals_ref[2 * h + 1, 0:k, :]
        for a in range(k):
            cand_ref[a * k:(a + 1) * k, :] = v1[a:a + 1, :] + v2
        cand = cand_ref[...]
        x = cand
        kth = None
        for r in range(k):
            kth = jnp.max(x, axis=0, keepdims=True)
            x = jnp.where(x == kth, LOWEST, x)
        m1 = v1[0:1, :]
        m2 = v2[0:1, :]
        nxt = jnp.maximum(jnp.max(x, axis=0, keepdims=True),
                          jnp.maximum(vals_ref[2 * h, k:k + 1, :] + m2,
                                      m1 + vals_ref[2 * h + 1, k:k + 1, :]))
        tau = 0.5 * (kth + nxt)
        z = jnp.sum(jnp.where(cand >= tau, jnp.exp(cand - (m1 + m2)), 0.0), axis=0, keepdims=True)
        s1 = s_ref[2 * h]
        s2 = s_ref[2 * h + 1]
        thr_ref[h] = tau - s1
        e1_ref[h] = jnp.exp(s1 - m1)
        e2 = jnp.exp(s2 - m2) * (1.0 / z)
        for c in range(s2_ref.shape[1]):
            s2_ref[h, c] = s2[:, c * 128:(c + 1) * 128]
            e2_ref[h, c] = e2[:, c * 128:(c + 1) * 128]
        return carry

    lax.fori_loop(0, nhc // 2, head_stats, 0)


def _peer_scores(q, keys, tm=256):
    t, _ = q.shape
    nhc = keys.shape[0]
    out_spec = pl.BlockSpec((PEER_HEADS, PEER_NKEYS, tm), lambda i: (0, 0, i))
    out_shape = jax.ShapeDtypeStruct((PEER_HEADS, PEER_NKEYS, t), F32)
    chunk_spec = pl.BlockSpec((PEER_HEADS, tm // 128, PEER_NKEYS, 128), lambda i: (0, i, 0, 0))
    chunk_shape = jax.ShapeDtypeStruct((PEER_HEADS, t // 128, PEER_NKEYS, 128), F32)
    return pl.pallas_call(
        _peer_scores_kernel,
        grid=(t // tm,),
        in_specs=[pl.BlockSpec((tm, nhc * PEER_NKEYS), lambda i: (i, 0)),
                  pl.BlockSpec(keys.shape, lambda i: (0, 0, 0))],
        out_specs=[out_spec, out_spec, chunk_spec, chunk_spec],
        out_shape=[out_shape, out_shape, chunk_shape, chunk_shape],
        scratch_shapes=[pltpu.VMEM((nhc, PEER_NKEYS, tm), F32),
                        pltpu.VMEM((nhc, PEER_TOPK + 8, tm), F32),
                        pltpu.VMEM((PEER_TOPK * PEER_TOPK, tm), F32)],
        compiler_params=_params(("parallel",)),
        name="peer_scores",
    )(q, keys)


def _peer_dense_kernel(xn_ref, thr_ref, e1_ref, s2_ref, e2_ref, u_ref, vt_ref, acc_ref,
                       xnt_ref, act_ref, p_ref):
    j = pl.program_id(1)
    lane_chunks, tn, _ = act_ref.shape
    rows = tn // PEER_NKEYS

    @pl.when(j == 0)
    def _():
        acc_ref[...] = jnp.zeros(acc_ref.shape, F32)
        xnt_ref[...] = xn_ref[...].astype(F32).T.astype(xnt_ref.dtype)

    act = jnp.dot(u_ref[...], xnt_ref[...], preferred_element_type=F32)
    for lc in range(lane_chunks):
        act_ref[lc] = act[:, lc * 128:(lc + 1) * 128]

    def weigh(r, carry):
        i1 = j * rows + r
        roff = pl.multiple_of(r * PEER_NKEYS, PEER_NKEYS)
        thr_rows = [thr_ref[h, pl.ds(i1, 1), :] for h in range(PEER_HEADS)]
        e1_rows = [e1_ref[h, pl.ds(i1, 1), :] for h in range(PEER_HEADS)]
        for lc in range(lane_chunks):
            lanes = slice(lc * 128, (lc + 1) * 128)
            w = jnp.zeros((PEER_NKEYS, 128), F32)
            for h in range(PEER_HEADS):
                w = w + jnp.where(s2_ref[h, lc] >= thr_rows[h][:, lanes],
                                  e2_ref[h, lc] * e1_rows[h][:, lanes], 0.0)
            a = act_ref[lc, pl.ds(roff, PEER_NKEYS), :]
            p_ref[pl.ds(roff, PEER_NKEYS), lanes] = (_gelu_tanh(a) * w).astype(p_ref.dtype)
        return carry

    lax.fori_loop(0, rows, weigh, 0)
    acc_ref[...] += jnp.dot(vt_ref[...], p_ref[...], preferred_element_type=F32)


def _peer_dense(xn, thr, e1, s2, e2, u_bf, vt_bf, tm=512, tn=512):
    t, d = xn.shape
    e = u_bf.shape[0]
    route_spec = pl.BlockSpec((PEER_HEADS, PEER_NKEYS, tm), lambda i, j: (0, 0, i))
    chunk_spec = pl.BlockSpec((PEER_HEADS, tm // 128, PEER_NKEYS, 128), lambda i, j: (0, i, 0, 0))
    return pl.pallas_call(
        _peer_dense_kernel,
        grid=(t // tm, e // tn),
        in_specs=[pl.BlockSpec((tm, d), lambda i, j: (i, 0)),
                  route_spec, route_spec, chunk_spec, chunk_spec,
                  pl.BlockSpec((tn, d), lambda i, j: (j, 0)),
                  pl.BlockSpec((d, tn), lambda i, j: (0, j))],
        out_specs=pl.BlockSpec((d, tm), lambda i, j: (0, i)),
        out_shape=jax.ShapeDtypeStruct((d, t), F32),
        scratch_shapes=[pltpu.VMEM((d, tm), BF16), pltpu.VMEM((tm // 128, tn, 128), F32),
                        pltpu.VMEM((tn, tm), BF16)],
        compiler_params=_params(("parallel", "arbitrary")),
        name="peer_dense",
    )(xn, thr, e1, s2, e2, u_bf, vt_bf)


def _final_kernel(x1_ref, pt_ref, g_ref, y_ref):
    y_ref[...] = _rms(x1_ref[...] + pt_ref[...].T, g_ref[...])


def _residual_norm(x1, peer_t, g, tm=256):
    t, d = x1.shape
    return pl.pallas_call(
        _final_kernel,
        grid=(t // tm,),
        in_specs=[pl.BlockSpec((tm, d), lambda i: (i, 0)),
                  pl.BlockSpec((d, tm), lambda i: (0, i)),
                  pl.BlockSpec((1, d), lambda i: (0, 0))],
        out_specs=pl.BlockSpec((tm, d), lambda i: (i, 0)),
        out_shape=jax.ShapeDtypeStruct((t, d), F32),
        compiler_params=_params(("parallel",)),
        name="residual_norm",
    )(x1, peer_t, g.reshape(1, d).astype(F32))


def kernel(x, norm_mix_g, w_in, conv_w, conv_b, lru_wa, lru_ba, lru_wx, lru_bx, lru_lambda,
           w_branch, w_out, rel_bias, norm_ffn_g, peer_wq, peer_keys, peer_u, peer_v, norm_final_g):
    batch, seq, d = x.shape
    t = batch * seq
    assert w_in.shape[0] == 1, "single-layer trunk: the final rmsnorm is fused into the PEER epilogue"
    xt = x.reshape(t, d)
    bias = _bias_tiles(rel_bias)
    a_w = ATTN_WIDTH
    w_l = w_in[0]
    h = _rmsnorm(xt, norm_mix_g[0], BF16)
    qk = _matmul(h, w_l[:, :2 * a_w].astype(BF16), BF16, "proj_qk")
    vt3 = _v_transposed(w_l[:, 2 * a_w:3 * a_w].T.astype(BF16), h)
    r = _matmul(h, w_l[:, 3 * a_w:3 * a_w + 2 * LRU_WIDTH].astype(BF16), F32, "proj_rec")
    gl = _matmul(h, w_l[:, 3 * a_w + 2 * LRU_WIDTH:].astype(BF16), F32, "proj_gate")
    o_att = _attention(qk, vt3, bias, batch, seq)
    o_rec = _rglru(r, conv_w[0], conv_b[0], lru_wa[0], lru_wx[0], lru_ba[0], lru_bx[0],
                   lru_lambda[0], batch, seq)
    x1, xn = _merge_out(o_att, o_rec, gl, w_branch[0, 0].astype(BF16),
                        w_branch[0, 1].astype(BF16), w_out[0].astype(BF16), xt, norm_ffn_g[0])
    q = _matmul(xn, peer_wq[0].astype(BF16), BF16, "peer_query")
    keys = peer_keys[0].reshape(PEER_HEADS * 2, PEER_NKEYS, PEER_DKEY // 2).astype(BF16)
    thr, e1, s2, e2 = _peer_scores(q, keys)
    peer_t = _peer_dense(xn, thr, e1, s2, e2, peer_u[0].astype(BF16), peer_v[0].T.astype(BF16))
    y = _residual_norm(x1, peer_t, norm_final_g)
    return y.reshape(batch, seq, d)
```

```python
import functools
import math

import jax
import jax.numpy as jnp
from jax import lax
from jax.experimental import pallas as pl
from jax.experimental.pallas import tpu as pltpu

D_MODEL = 2048
ATTN_HEADS = 8
HEAD_DIM = 128
ATTN_WIDTH = ATTN_HEADS * HEAD_DIM
MOBA_BLOCK = 256
MOBA_TOPK = 3
LRU_WIDTH = 1024
LRU_BLOCKS = 8
LRU_BLOCK_DIM = LRU_WIDTH // LRU_BLOCKS
CONV_WIDTH = 4
LRU_C = 8.0
REL_BUCKETS = 32
REL_MAX_DIST = 128
PEER_HEADS = 8
PEER_NKEYS = 128
PEER_EXPERTS = PEER_NKEYS * PEER_NKEYS
PEER_DKEY = 256
PEER_TOPK = 16
EPS = 1e-6
NEG = -1e30
LOWEST = -3.0e38

V7X_VMEM_LIMIT_BYTES = 56 * 1024 * 1024

F32 = jnp.float32
BF16 = jnp.bfloat16

_NT_DIMS = (((1,), (1,)), ((), ()))


def _params(semantics, flags=None):
    return pltpu.CompilerParams(dimension_semantics=semantics,
                                vmem_limit_bytes=V7X_VMEM_LIMIT_BYTES, flags=flags)


def _gelu_tanh(x):
    c = math.sqrt(2.0 / math.pi)
    return (0.5 * x) * (1.0 + jnp.tanh(x * (c + (0.044715 * c) * (x * x))))


def _rms(x, g):
    ms = jnp.mean(x * x, axis=-1, keepdims=True)
    return x * lax.rsqrt(ms + EPS) * g


def _rmsnorm_kernel(x_ref, g_ref, o_ref):
    o_ref[...] = _rms(x_ref[...], g_ref[...]).astype(o_ref.dtype)


def _rmsnorm(x, g, out_dtype, tm=512):
    t, d = x.shape
    return pl.pallas_call(
        _rmsnorm_kernel,
        grid=(t // tm,),
        in_specs=[pl.BlockSpec((tm, d), lambda i: (i, 0)),
                  pl.BlockSpec((1, d), lambda i: (0, 0))],
        out_specs=pl.BlockSpec((tm, d), lambda i: (i, 0)),
        out_shape=jax.ShapeDtypeStruct((t, d), out_dtype),
        compiler_params=_params(("parallel",)),
        name="rmsnorm",
    )(x, g.reshape(1, d))


def _mm_kernel(a_ref, b_ref, o_ref):
    o_ref[...] = jnp.dot(a_ref[...], b_ref[...],
                         preferred_element_type=F32).astype(o_ref.dtype)


def _matmul(a, b, out_dtype, name, tm=1024, tn=1024):
    m, k = a.shape
    _, n = b.shape
    return pl.pallas_call(
        _mm_kernel,
        grid=(m // tm, n // tn),
        in_specs=[pl.BlockSpec((tm, k), lambda i, j: (i, 0)),
                  pl.BlockSpec((k, tn), lambda i, j: (0, j))],
        out_specs=pl.BlockSpec((tm, tn), lambda i, j: (i, j)),
        out_shape=jax.ShapeDtypeStruct((m, n), out_dtype),
        compiler_params=_params(("parallel", "arbitrary")),
        name=name,
    )(a, b)


def _vt_kernel(w_ref, h_ref, o_ref):
    res = lax.dot_general(w_ref[...], h_ref[...], _NT_DIMS, preferred_element_type=F32)
    for t in range(o_ref.shape[0]):
        o_ref[t] = res[:, t * MOBA_BLOCK:(t + 1) * MOBA_BLOCK].astype(o_ref.dtype)


def _v_transposed(w_t, h, tm=1024):
    c, k = w_t.shape
    t, _ = h.shape
    nb = tm // MOBA_BLOCK
    return pl.pallas_call(
        _vt_kernel,
        grid=(t // tm,),
        in_specs=[pl.BlockSpec((c, k), lambda i: (0, 0)),
                  pl.BlockSpec((tm, k), lambda i: (i, 0))],
        out_specs=pl.BlockSpec((nb, c, MOBA_BLOCK), lambda i: (i, 0, 0)),
        out_shape=jax.ShapeDtypeStruct((t // MOBA_BLOCK, c, MOBA_BLOCK), BF16),
        compiler_params=_params(("parallel",)),
        name="v_transposed",
    )(w_t, h)


def _rel_bucket(dist):
    n = jnp.maximum(dist, 0)
    max_exact = REL_BUCKETS // 2
    nf = jnp.maximum(n, 1).astype(F32)
    large = max_exact + (jnp.log(nf / max_exact) / math.log(REL_MAX_DIST / max_exact)
                         * (REL_BUCKETS - max_exact)).astype(jnp.int32)
    large = jnp.minimum(large, REL_BUCKETS - 1)
    return jnp.where(n < max_exact, n, large)


def _bias_kernel(relb_ref, bucket_ref, o_ref):
    h = pl.program_id(0)
    for t in range(3):
        bk = bucket_ref[t]
        acc = jnp.zeros(bk.shape, F32)
        for b in range(REL_BUCKETS):
            acc = jnp.where(bk == b, relb_ref[b, h], acc)
        if t == 0:
            key = lax.broadcasted_iota(jnp.int32, bk.shape, 0)
            qry = lax.broadcasted_iota(jnp.int32, bk.shape, 1)
            acc = jnp.where(key <= qry, acc, NEG)
        o_ref[0, t] = acc


def _bias_tiles(rel_bias):
    assert REL_MAX_DIST <= MOBA_BLOCK
    key = jnp.arange(MOBA_BLOCK, dtype=jnp.int32)[:, None]
    qry = jnp.arange(MOBA_BLOCK, dtype=jnp.int32)[None, :]
    buckets = jnp.stack([_rel_bucket(t * MOBA_BLOCK + qry - key) for t in range(3)])
    return pl.pallas_call(
        _bias_kernel,
        grid=(ATTN_HEADS,),
        in_specs=[pl.BlockSpec(memory_space=pltpu.SMEM),
                  pl.BlockSpec((3, MOBA_BLOCK, MOBA_BLOCK), lambda h: (0, 0, 0))],
        out_specs=pl.BlockSpec((1, 3, MOBA_BLOCK, MOBA_BLOCK), lambda h: (h, 0, 0, 0)),
        out_shape=jax.ShapeDtypeStruct((ATTN_HEADS, 3, MOBA_BLOCK, MOBA_BLOCK), F32),
        compiler_params=_params(("parallel",)),
        name="attn_bias_tiles",
    )(rel_bias.astype(F32), buckets)


def _attn_kernel(q_ref, k_ref, vt_ref, bias_ref, o_ref, kmean_ref, sel_ref, *, nb):
    i = pl.program_id(2)
    scale = HEAD_DIM ** -0.5

    @pl.when(i == 0)
    def _():
        for n in range(nb):
            kb = k_ref[n * MOBA_BLOCK:(n + 1) * MOBA_BLOCK, :].astype(F32)
            kmean_ref[n:n + 1, :] = jnp.mean(kb, axis=0, keepdims=True)

    q = q_ref[...]
    gate = lax.dot_general(kmean_ref[...].astype(BF16), q, _NT_DIMS,
                           preferred_element_type=F32)
    blk = lax.broadcasted_iota(jnp.int32, gate.shape, 0)
    rank = jnp.zeros(gate.shape, F32)
    for m in range(nb):
        gm = gate[m:m + 1, :]
        beats = jnp.where(gm > gate, 1.0, jnp.where(gm == gate, jnp.where(m < blk, 1.0, 0.0), 0.0))
        rank = rank + beats * jnp.where(m < i, 1.0, 0.0)
    sel_ref[...] = jnp.where(blk < i, jnp.where(rank < MOBA_TOPK, 1.0, 0.0), 0.0)

    def scores(j, tile):
        koff = pl.multiple_of(j * MOBA_BLOCK, MOBA_BLOCK)
        kj = k_ref[pl.ds(koff, MOBA_BLOCK), :]
        s = lax.dot_general(kj, q, _NT_DIMS, preferred_element_type=F32)
        return s * scale + bias_ref[0, tile]

    s = scores(i, 0)
    m0 = jnp.max(s, axis=0, keepdims=True)
    p = jnp.exp(s - m0)
    l0 = jnp.sum(p, axis=0, keepdims=True)
    acc0 = jnp.dot(vt_ref[i], p.astype(BF16), preferred_element_type=F32)

    def body(j, carry):
        m_i, l_i, acc = carry
        s = scores(j, jnp.minimum(i - j, 2))
        s = jnp.where(sel_ref[pl.ds(j, 1), :] > 0.5, s, NEG)
        m_new = jnp.maximum(m_i, jnp.max(s, axis=0, keepdims=True))
        alpha = jnp.exp(m_i - m_new)
        p = jnp.exp(s - m_new)
        l_new = alpha * l_i + jnp.sum(p, axis=0, keepdims=True)
        acc = alpha * acc + jnp.dot(vt_ref[j], p.astype(BF16), preferred_element_type=F32)
        return m_new, l_new, acc

    _, l_f, acc_f = lax.fori_loop(0, i, body, (m0, l0, acc0))
    o_ref[...] = (acc_f / l_f).T.astype(o_ref.dtype)


def _attention(qk, vt3, bias, batch, seq):
    nb = seq // MOBA_BLOCK
    t = batch * seq
    return pl.pallas_call(
        functools.partial(_attn_kernel, nb=nb),
        grid=(batch, ATTN_HEADS, nb),
        in_specs=[
            pl.BlockSpec((MOBA_BLOCK, HEAD_DIM), lambda b, h, i: (b * nb + i, h)),
            pl.BlockSpec((seq, HEAD_DIM), lambda b, h, i: (b, ATTN_HEADS + h)),
            pl.BlockSpec((nb, HEAD_DIM, MOBA_BLOCK), lambda b, h, i: (b, h, 0)),
            pl.BlockSpec((1, 3, MOBA_BLOCK, MOBA_BLOCK), lambda b, h, i: (h, 0, 0, 0)),
        ],
        out_specs=pl.BlockSpec((MOBA_BLOCK, HEAD_DIM), lambda b, h, i: (b * nb + i, h)),
        out_shape=jax.ShapeDtypeStruct((t, ATTN_WIDTH), BF16),
        scratch_shapes=[pltpu.VMEM((nb, HEAD_DIM), F32),
                        pltpu.VMEM((nb, MOBA_BLOCK), F32)],
        compiler_params=_params(("parallel", "parallel", "arbitrary")),
        name="moba_attention",
    )(qk, qk, vt3, bias)


def _rglru_kernel(xr_ref, yr_ref, cw_ref, cb_ref, wa_ref, wx_ref, ba_ref, bx_ref, lam_ref,
                  o_ref, tail_ref, h_ref):
    c = pl.program_id(1)
    tc = xr_ref.shape[0]

    @pl.when(c == 0)
    def _():
        tail_ref[...] = jnp.zeros(tail_ref.shape, F32)
        h_ref[...] = jnp.zeros(h_ref.shape, F32)

    x = xr_ref[...]
    xfull = jnp.concatenate([tail_ref[...], x], axis=0)
    conv = cb_ref[...]
    for k in range(CONV_WIDTH):
        off = 8 - (CONV_WIDTH - 1) + k
        conv = conv + cw_ref[k:k + 1, :] * xfull[off:off + tc, :]
    tail_ref[...] = x[tc - 8:, :]

    xb16 = conv.astype(BF16)
    pre_a, pre_x = [], []
    for g in range(LRU_BLOCKS):
        xs = xb16[:, g * LRU_BLOCK_DIM:(g + 1) * LRU_BLOCK_DIM]
        pre_a.append(jnp.dot(xs, wa_ref[g], preferred_element_type=F32))
        pre_x.append(jnp.dot(xs, wx_ref[g], preferred_element_type=F32))
    r_gate = jax.nn.sigmoid(jnp.concatenate(pre_a, axis=1) + ba_ref[...])
    i_gate = jax.nn.sigmoid(jnp.concatenate(pre_x, axis=1) + bx_ref[...])
    z = -lam_ref[...]
    softplus = jnp.maximum(z, 0.0) + jnp.log1p(jnp.exp(-jnp.abs(z)))
    log_a = (-LRU_C) * r_gate * softplus
    a = jnp.exp(log_a)
    th = jnp.tanh(log_a)
    u = jnp.sqrt(-2.0 * th / (1.0 - th)) * (i_gate * conv)

    row = lax.broadcasted_iota(jnp.int32, a.shape, 0)
    s = 1
    while s < tc:
        a_sh = pltpu.roll(a, s, axis=0)
        u_sh = pltpu.roll(u, s, axis=0)
        valid = row >= s
        u = jnp.where(valid, a * u_sh + u, u)
        a = jnp.where(valid, a * a_sh, a)
        s *= 2
    hh = a * h_ref[...] + u
    h_ref[...] = hh[tc - 1:tc, :]
    o_ref[...] = (hh * _gelu_tanh(yr_ref[...])).astype(o_ref.dtype)


def _rglru(r, conv_w, conv_b, wa, wx, ba, bx, lam, batch, seq, tc=256):
    t = batch * seq
    nc = seq // tc
    w = LRU_WIDTH
    row = lambda v: v.reshape(1, w).astype(F32)
    full2 = lambda shape: pl.BlockSpec(shape, lambda b, c: (0,) * len(shape))
    return pl.pallas_call(
        _rglru_kernel,
        grid=(batch, nc),
        in_specs=[pl.BlockSpec((tc, w), lambda b, c: (b * nc + c, 0)),
                  pl.BlockSpec((tc, w), lambda b, c: (b * nc + c, 1)),
                  full2((CONV_WIDTH, w)), full2((1, w)),
                  full2((LRU_BLOCKS, LRU_BLOCK_DIM, LRU_BLOCK_DIM)),
                  full2((LRU_BLOCKS, LRU_BLOCK_DIM, LRU_BLOCK_DIM)),
                  full2((1, w)), full2((1, w)), full2((1, w))],
        out_specs=pl.BlockSpec((tc, w), lambda b, c: (b * nc + c, 0)),
        out_shape=jax.ShapeDtypeStruct((t, w), BF16),
        scratch_shapes=[pltpu.VMEM((8, w), F32), pltpu.VMEM((1, w), F32)],
        compiler_params=_params(("parallel", "arbitrary")),
        name="rglru",
    )(r, r, conv_w.astype(F32), row(conv_b), wa.astype(BF16), wx.astype(BF16),
      row(ba), row(bx), row(lam))


def _merge_kernel(oa_ref, or_ref, g0_ref, g1_ref, wb0_ref, wb1_ref, wo_ref, x_ref, gn_ref,
                  x1_ref, xn_ref):
    pb0 = jnp.dot(oa_ref[...], wb0_ref[...], preferred_element_type=F32)
    pb1 = jnp.dot(or_ref[...], wb1_ref[...], preferred_element_type=F32)
    merged = jax.nn.sigmoid(g0_ref[...]) * pb0 + jax.nn.sigmoid(g1_ref[...]) * pb1
    x1 = x_ref[...] + jnp.dot(merged.astype(BF16), wo_ref[...], preferred_element_type=F32)
    x1_ref[...] = x1
    xn_ref[...] = _rms(x1, gn_ref[...]).astype(xn_ref.dtype)


def _merge_out(o_att, o_rec, gl, wb0, wb1, w_out, x, g_ffn, tm=256):
    t, d = x.shape
    cw = o_att.shape[1]
    resident = lambda shape: pl.BlockSpec(shape, lambda i: (0, 0), pipeline_mode=pl.Buffered(1))
    return pl.pallas_call(
        _merge_kernel,
        grid=(t // tm,),
        in_specs=[pl.BlockSpec((tm, cw), lambda i: (i, 0)),
                  pl.BlockSpec((tm, cw), lambda i: (i, 0)),
                  pl.BlockSpec((tm, d), lambda i: (i, 0)),
                  pl.BlockSpec((tm, d), lambda i: (i, 1)),
                  resident((cw, d)), resident((cw, d)), resident((d, d)),
                  pl.BlockSpec((tm, d), lambda i: (i, 0)),
                  pl.BlockSpec((1, d), lambda i: (0, 0))],
        out_specs=[pl.BlockSpec((tm, d), lambda i: (i, 0)),
                   pl.BlockSpec((tm, d), lambda i: (i, 0))],
        out_shape=[jax.ShapeDtypeStruct((t, d), F32), jax.ShapeDtypeStruct((t, d), BF16)],
        compiler_params=_params(("parallel",)),
        name="merge_out",
    )(o_att, o_rec, gl, gl, wb0, wb1, w_out, x, g_ffn.reshape(1, d).astype(F32))


def _peer_scores_kernel(q_ref, keys_ref, thr_ref, e1_ref, s2_ref, e2_ref, s_ref, vals_ref, cand_ref):
    nhc = keys_ref.shape[0]
    k = PEER_TOPK
    for hc in range(nhc):
        qs = q_ref[:, hc * PEER_NKEYS:(hc + 1) * PEER_NKEYS]
        s_ref[hc] = lax.dot_general(keys_ref[hc], qs, _NT_DIMS, preferred_element_type=F32)

    def top_values(hc, carry):
        x = s_ref[hc]
        for r in range(k + 1):
            m = jnp.max(x, axis=0, keepdims=True)
            vals_ref[hc, r:r + 1, :] = m
            x = jnp.where(x == m, LOWEST, x)
        return carry

    lax.fori_loop(0, nhc, top_values, 0)

    def head_stats(h, carry):
        v1 = vals_ref[2 * h, 0:k, :]
        v2 = vals_ref[2 * h + 1, 0:k, :]
        for a in range(k):
            cand_ref[a * k:(a + 1) * k, :] = v1[a:a + 1, :] + v2
        cand = cand_ref[...]
        x = cand
        kth = None
        for r in range(k):
            kth = jnp.max(x, axis=0, keepdims=True)
            x = jnp.where(x == kth, LOWEST, x)
        m1 = v1[0:1, :]
        m2 = v2[0:1, :]
        nxt = jnp.maximum(jnp.max(x, axis=0, keepdims=True),
                          jnp.maximum(vals_ref[2 * h, k:k + 1, :] + m2,
                                      m1 + vals_ref[2 * h + 1, k:k + 1, :]))
        tau = 0.5 * (kth + nxt)
        z = jnp.sum(jnp.where(cand >= tau, jnp.exp(cand - (m1 + m2)), 0.0), axis=0, keepdims=True)
        s1 = s_ref[2 * h]
        s2 = s_ref[2 * h + 1]
        thr_ref[h] = tau - s1
        e1_ref[h] = jnp.exp(s1 - m1)
        e2 = jnp.exp(s2 - m2) * (1.0 / z)
        for c in range(s2_ref.shape[1]):
            s2_ref[h, c] = s2[:, c * 128:(c + 1) * 128]
            e2_ref[h, c] = e2[:, c * 128:(c + 1) * 128]
        return carry

    lax.fori_loop(0, nhc // 2, head_stats, 0)


def _peer_scores(q, keys, tm=256):
    t, _ = q.shape
    nhc = keys.shape[0]
    out_spec = pl.BlockSpec((PEER_HEADS, PEER_NKEYS, tm), lambda i: (0, 0, i))
    out_shape = jax.ShapeDtypeStruct((PEER_HEADS, PEER_NKEYS, t), F32)
    chunk_spec = pl.BlockSpec((PEER_HEADS, tm // 128, PEER_NKEYS, 128), lambda i: (0, i, 0, 0))
    chunk_shape = jax.ShapeDtypeStruct((PEER_HEADS, t // 128, PEER_NKEYS, 128), F32)
    return pl.pallas_call(
        _peer_scores_kernel,
        grid=(t // tm,),
        in_specs=[pl.BlockSpec((tm, nhc * PEER_NKEYS), lambda i: (i, 0)),
                  pl.BlockSpec(keys.shape, lambda i: (0, 0, 0))],
        out_specs=[out_spec, out_spec, chunk_spec, chunk_spec],
        out_shape=[out_shape, out_shape, chunk_shape, chunk_shape],
        scratch_shapes=[pltpu.VMEM((nhc, PEER_NKEYS, tm), F32),
                        pltpu.VMEM((nhc, PEER_TOPK + 8, tm), F32),
                        pltpu.VMEM((PEER_TOPK * PEER_TOPK, tm), F32)],
        compiler_params=_params(("parallel",)),
        name="peer_scores",
    )(q, keys)


def _peer_dense_kernel(xn_ref, thr_ref, e1_ref, s2_ref, e2_ref, u_ref, vta_ref, vtb_ref, acc_ref,
                       xnt_ref, act_ref, p0_ref, p1_ref, *, n_pairs):
    g = pl.program_id(1)
    lane_chunks, tn, _ = act_ref.shape
    rows = tn // PEER_NKEYS
    d_rows = acc_ref.shape[0] // rows

    @pl.when(g == 0)
    def _():
        acc_ref[...] = jnp.zeros(acc_ref.shape, F32)
        p1_ref[...] = jnp.zeros(p1_ref.shape, p1_ref.dtype)
        xnt_ref[...] = xn_ref[...].astype(F32).T.astype(xnt_ref.dtype)

    def half_step(u_rows, tile, p_new, vt_prev_ref, p_prev):
        act = jnp.dot(u_rows, xnt_ref[...], preferred_element_type=F32)
        for lc in range(lane_chunks):
            act_ref[lc] = act[:, lc * 128:(lc + 1) * 128]

        def body(r, carry):
            i1 = tile * rows + r
            roff = pl.multiple_of(r * PEER_NKEYS, PEER_NKEYS)
            thr_rows = [thr_ref[h, pl.ds(i1, 1), :] for h in range(PEER_HEADS)]
            e1_rows = [e1_ref[h, pl.ds(i1, 1), :] for h in range(PEER_HEADS)]
            for lc in range(lane_chunks):
                lanes = slice(lc * 128, (lc + 1) * 128)
                w = jnp.zeros((PEER_NKEYS, 128), F32)
                for h in range(PEER_HEADS):
                    w = w + jnp.where(s2_ref[h, lc] >= thr_rows[h][:, lanes],
                                      e2_ref[h, lc] * e1_rows[h][:, lanes], 0.0)
                a = act_ref[lc, pl.ds(roff, PEER_NKEYS), :]
                p_new[pl.ds(roff, PEER_NKEYS), lanes] = (_gelu_tanh(a) * w).astype(p_new.dtype)
            doff = pl.multiple_of(r * d_rows, d_rows)
            acc_ref[pl.ds(doff, d_rows), :] += jnp.dot(vt_prev_ref[pl.ds(doff, d_rows), :], p_prev[...],
                                                       preferred_element_type=F32)
            return carry

        lax.fori_loop(0, rows, body, 0)

    @pl.when(g < n_pairs)
    def _():
        half_step(u_ref[0:tn, :], 2 * g, p0_ref, vta_ref, p1_ref)
        half_step(u_ref[tn:2 * tn, :], 2 * g + 1, p1_ref, vtb_ref, p0_ref)

    @pl.when(g == n_pairs)
    def _():
        acc_ref[...] += jnp.dot(vta_ref[...], p1_ref[...], preferred_element_type=F32)


def _peer_dense(xn, thr, e1, s2, e2, u_bf, vt_bf, tm=512, tn=512):
    t, d = xn.shape
    e = u_bf.shape[0]
    n_tiles = e // tn
    n_pairs = n_tiles // 2
    route_spec = pl.BlockSpec((PEER_HEADS, PEER_NKEYS, tm), lambda i, g: (0, 0, i))
    chunk_spec = pl.BlockSpec((PEER_HEADS, tm // 128, PEER_NKEYS, 128), lambda i, g: (0, i, 0, 0))
    return pl.pallas_call(
        functools.partial(_peer_dense_kernel, n_pairs=n_pairs),
        grid=(t // tm, n_pairs + 1),
        in_specs=[pl.BlockSpec((tm, d), lambda i, g: (i, 0)),
                  route_spec, route_spec, chunk_spec, chunk_spec,
                  pl.BlockSpec((2 * tn, d), lambda i, g: (jnp.minimum(g, n_pairs - 1), 0)),
                  pl.BlockSpec((d, tn), lambda i, g: (0, jnp.maximum(2 * g - 1, 0))),
                  pl.BlockSpec((d, tn), lambda i, g: (0, jnp.minimum(2 * g, n_tiles - 1)))],
        out_specs=pl.BlockSpec((d, tm), lambda i, g: (0, i)),
        out_shape=jax.ShapeDtypeStruct((d, t), F32),
        scratch_shapes=[pltpu.VMEM((d, tm), BF16), pltpu.VMEM((tm // 128, tn, 128), F32),
                        pltpu.VMEM((tn, tm), BF16), pltpu.VMEM((tn, tm), BF16)],
        compiler_params=_params(("parallel", "arbitrary")),
        name="peer_dense",
    )(xn, thr, e1, s2, e2, u_bf, vt_bf, vt_bf)


def _final_kernel(x1_ref, pt_ref, g_ref, y_ref):
    y_ref[...] = _rms(x1_ref[...] + pt_ref[...].T, g_ref[...])


def _residual_norm(x1, peer_t, g, tm=256):
    t, d = x1.shape
    return pl.pallas_call(
        _final_kernel,
        grid=(t // tm,),
        in_specs=[pl.BlockSpec((tm, d), lambda i: (i, 0)),
                  pl.BlockSpec((d, tm), lambda i: (0, i)),
                  pl.BlockSpec((1, d), lambda i: (0, 0))],
        out_specs=pl.BlockSpec((tm, d), lambda i: (i, 0)),
        out_shape=jax.ShapeDtypeStruct((t, d), F32),
        compiler_params=_params(("parallel",)),
        name="residual_norm",
    )(x1, peer_t, g.reshape(1, d).astype(F32))


def kernel(x, norm_mix_g, w_in, conv_w, conv_b, lru_wa, lru_ba, lru_wx, lru_bx, lru_lambda,
           w_branch, w_out, rel_bias, norm_ffn_g, peer_wq, peer_keys, peer_u, peer_v, norm_final_g):
    batch, seq, d = x.shape
    t = batch * seq
    assert w_in.shape[0] == 1, "single-layer trunk: the final rmsnorm is fused into the PEER epilogue"
    xt = x.reshape(t, d)
    bias = _bias_tiles(rel_bias)
    a_w = ATTN_WIDTH
    w_l = w_in[0]
    h = _rmsnorm(xt, norm_mix_g[0], BF16)
    qk = _matmul(h, w_l[:, :2 * a_w].astype(BF16), BF16, "proj_qk")
    vt3 = _v_transposed(w_l[:, 2 * a_w:3 * a_w].T.astype(BF16), h)
    r = _matmul(h, w_l[:, 3 * a_w:3 * a_w + 2 * LRU_WIDTH].astype(BF16), F32, "proj_rec")
    gl = _matmul(h, w_l[:, 3 * a_w + 2 * LRU_WIDTH:].astype(BF16), F32, "proj_gate")
    o_att = _attention(qk, vt3, bias, batch, seq)
    o_rec = _rglru(r, conv_w[0], conv_b[0], lru_wa[0], lru_wx[0], lru_ba[0], lru_bx[0],
                   lru_lambda[0], batch, seq)
    x1, xn = _merge_out(o_att, o_rec, gl, w_branch[0, 0].astype(BF16),
                        w_branch[0, 1].astype(BF16), w_out[0].astype(BF16), xt, norm_ffn_g[0])
    q = _matmul(xn, peer_wq[0].astype(BF16), BF16, "peer_query")
    keys = peer_keys[0].reshape(PEER_HEADS * 2, PEER_NKEYS, PEER_DKEY // 2).astype(BF16)
    thr, e1, s2, e2 = _peer_scores(q, keys)
    peer_t = _peer_dense(xn, thr, e1, s2, e2, peer_u[0].astype(BF16), peer_v[0].T.astype(BF16))
    y = _residual_norm(x1, peer_t, norm_final_g)
    return y.reshape(batch, seq, d)
```

```python
import functools
import math

import jax
import jax.numpy as jnp
from jax import lax
from jax.experimental import pallas as pl
from jax.experimental.pallas import tpu as pltpu

D_MODEL = 2048
ATTN_HEADS = 8
HEAD_DIM = 128
ATTN_WIDTH = ATTN_HEADS * HEAD_DIM
MOBA_BLOCK = 256
MOBA_TOPK = 3
LRU_WIDTH = 1024
LRU_BLOCKS = 8
LRU_BLOCK_DIM = LRU_WIDTH // LRU_BLOCKS
CONV_WIDTH = 4
LRU_C = 8.0
REL_BUCKETS = 32
REL_MAX_DIST = 128
PEER_HEADS = 8
PEER_NKEYS = 128
PEER_EXPERTS = PEER_NKEYS * PEER_NKEYS
PEER_DKEY = 256
PEER_TOPK = 16
EPS = 1e-6
NEG = -1e30
LOWEST = -3.0e38

V7X_VMEM_LIMIT_BYTES = 56 * 1024 * 1024

F32 = jnp.float32
BF16 = jnp.bfloat16

_NT_DIMS = (((1,), (1,)), ((), ()))


def _params(semantics, flags=None):
    return pltpu.CompilerParams(dimension_semantics=semantics,
                                vmem_limit_bytes=V7X_VMEM_LIMIT_BYTES, flags=flags)


def _gelu_tanh(x):
    c = math.sqrt(2.0 / math.pi)
    return (0.5 * x) * (1.0 + jnp.tanh(x * (c + (0.044715 * c) * (x * x))))


def _rms(x, g):
    ms = jnp.mean(x * x, axis=-1, keepdims=True)
    return x * lax.rsqrt(ms + EPS) * g


def _rmsnorm_kernel(x_ref, g_ref, o_ref):
    o_ref[...] = _rms(x_ref[...], g_ref[...]).astype(o_ref.dtype)


def _rmsnorm(x, g, out_dtype, tm=512):
    t, d = x.shape
    return pl.pallas_call(
        _rmsnorm_kernel,
        grid=(t // tm,),
        in_specs=[pl.BlockSpec((tm, d), lambda i: (i, 0)),
                  pl.BlockSpec((1, d), lambda i: (0, 0))],
        out_specs=pl.BlockSpec((tm, d), lambda i: (i, 0)),
        out_shape=jax.ShapeDtypeStruct((t, d), out_dtype),
        compiler_params=_params(("parallel",)),
        name="rmsnorm",
    )(x, g.reshape(1, d))


def _mm_kernel(a_ref, b_ref, o_ref):
    o_ref[...] = jnp.dot(a_ref[...], b_ref[...],
                         preferred_element_type=F32).astype(o_ref.dtype)


def _matmul(a, b, out_dtype, name, tm=1024, tn=1024):
    m, k = a.shape
    _, n = b.shape
    return pl.pallas_call(
        _mm_kernel,
        grid=(m // tm, n // tn),
        in_specs=[pl.BlockSpec((tm, k), lambda i, j: (i, 0)),
                  pl.BlockSpec((k, tn), lambda i, j: (0, j))],
        out_specs=pl.BlockSpec((tm, tn), lambda i, j: (i, j)),
        out_shape=jax.ShapeDtypeStruct((m, n), out_dtype),
        compiler_params=_params(("parallel", "arbitrary")),
        name=name,
    )(a, b)


def _vt_kernel(w_ref, h_ref, o_ref):
    res = lax.dot_general(w_ref[...], h_ref[...], _NT_DIMS, preferred_element_type=F32)
    for t in range(o_ref.shape[0]):
        o_ref[t] = res[:, t * MOBA_BLOCK:(t + 1) * MOBA_BLOCK].astype(o_ref.dtype)


def _v_transposed(w_t, h, tm=1024):
    c, k = w_t.shape
    t, _ = h.shape
    nb = tm // MOBA_BLOCK
    return pl.pallas_call(
        _vt_kernel,
        grid=(t // tm,),
        in_specs=[pl.BlockSpec((c, k), lambda i: (0, 0)),
                  pl.BlockSpec((tm, k), lambda i: (i, 0))],
        out_specs=pl.BlockSpec((nb, c, MOBA_BLOCK), lambda i: (i, 0, 0)),
        out_shape=jax.ShapeDtypeStruct((t // MOBA_BLOCK, c, MOBA_BLOCK), BF16),
        compiler_params=_params(("parallel",)),
        name="v_transposed",
    )(w_t, h)


def _rel_bucket(dist):
    n = jnp.maximum(dist, 0)
    max_exact = REL_BUCKETS // 2
    nf = jnp.maximum(n, 1).astype(F32)
    large = max_exact + (jnp.log(nf / max_exact) / math.log(REL_MAX_DIST / max_exact)
                         * (REL_BUCKETS - max_exact)).astype(jnp.int32)
    large = jnp.minimum(large, REL_BUCKETS - 1)
    return jnp.where(n < max_exact, n, large)


def _bias_kernel(relb_ref, bucket_ref, o_ref):
    h = pl.program_id(0)
    for t in range(3):
        bk = bucket_ref[t]
        acc = jnp.zeros(bk.shape, F32)
        for b in range(REL_BUCKETS):
            acc = jnp.where(bk == b, relb_ref[b, h], acc)
        if t == 0:
            key = lax.broadcasted_iota(jnp.int32, bk.shape, 0)
            qry = lax.broadcasted_iota(jnp.int32, bk.shape, 1)
            acc = jnp.where(key <= qry, acc, NEG)
        o_ref[0, t] = acc


def _bias_tiles(rel_bias):
    assert REL_MAX_DIST <= MOBA_BLOCK
    key = jnp.arange(MOBA_BLOCK, dtype=jnp.int32)[:, None]
    qry = jnp.arange(MOBA_BLOCK, dtype=jnp.int32)[None, :]
    buckets = jnp.stack([_rel_bucket(t * MOBA_BLOCK + qry - key) for t in range(3)])
    return pl.pallas_call(
        _bias_kernel,
        grid=(ATTN_HEADS,),
        in_specs=[pl.BlockSpec(memory_space=pltpu.SMEM),
                  pl.BlockSpec((3, MOBA_BLOCK, MOBA_BLOCK), lambda h: (0, 0, 0))],
        out_specs=pl.BlockSpec((1, 3, MOBA_BLOCK, MOBA_BLOCK), lambda h: (h, 0, 0, 0)),
        out_shape=jax.ShapeDtypeStruct((ATTN_HEADS, 3, MOBA_BLOCK, MOBA_BLOCK), F32),
        compiler_params=_params(("parallel",)),
        name="attn_bias_tiles",
    )(rel_bias.astype(F32), buckets)


ATTN_HEADS_PER_STEP = 8


def _attn_kernel(q_ref, k_ref, vt_ref, bias_ref, o_ref, kmean_ref, sel_ref, acc_ref, *, nb):
    i = pl.program_id(2)
    scale = HEAD_DIM ** -0.5
    heads = range(ATTN_HEADS_PER_STEP)
    hsl = [slice(g * HEAD_DIM, (g + 1) * HEAD_DIM) for g in heads]

    @pl.when(i == 0)
    def _():
        for g in heads:
            for n in range(nb):
                kb = k_ref[n * MOBA_BLOCK:(n + 1) * MOBA_BLOCK, hsl[g]].astype(F32)
                kmean_ref[g, n:n + 1, :] = jnp.mean(kb, axis=0, keepdims=True)

    qs = [q_ref[:, hsl[g]] for g in heads]
    for g in heads:
        gate = lax.dot_general(kmean_ref[g].astype(BF16), qs[g], _NT_DIMS,
                               preferred_element_type=F32)
        blk = lax.broadcasted_iota(jnp.int32, gate.shape, 0)
        rank = jnp.zeros(gate.shape, F32)
        for m in range(nb):
            gm = gate[m:m + 1, :]
            beats = jnp.where(gm > gate, 1.0, jnp.where(gm == gate, jnp.where(m < blk, 1.0, 0.0), 0.0))
            rank = rank + beats * jnp.where(m < i, 1.0, 0.0)
        sel_ref[g] = jnp.where(blk < i, jnp.where(rank < MOBA_TOPK, 1.0, 0.0), 0.0)

    def scores(g, j, tile):
        koff = pl.multiple_of(j * MOBA_BLOCK, MOBA_BLOCK)
        kj = k_ref[pl.ds(koff, MOBA_BLOCK), hsl[g]]
        s = lax.dot_general(kj, qs[g], _NT_DIMS, preferred_element_type=F32)
        return s * scale + bias_ref[g, tile]

    s0 = [scores(g, i, 0) for g in heads]
    m0 = [jnp.max(s0[g], axis=0, keepdims=True) for g in heads]
    p0 = [jnp.exp(s0[g] - m0[g]) for g in heads]
    l0 = [jnp.sum(p0[g], axis=0, keepdims=True) for g in heads]
    for g in heads:
        acc_ref[g] = jnp.dot(vt_ref[i, hsl[g], :], p0[g].astype(BF16), preferred_element_type=F32)

    def body(j, carry):
        ms, ls = carry
        tile = jnp.minimum(i - j, 2)
        ss = [scores(g, j, tile) for g in heads]
        ss = [jnp.where(sel_ref[g, pl.ds(j, 1), :] > 0.5, ss[g], NEG) for g in heads]
        new_m = [jnp.maximum(ms[g], jnp.max(ss[g], axis=0, keepdims=True)) for g in heads]
        alpha = [jnp.exp(ms[g] - new_m[g]) for g in heads]
        ps = [jnp.exp(ss[g] - new_m[g]) for g in heads]
        new_l = [alpha[g] * ls[g] + jnp.sum(ps[g], axis=0, keepdims=True) for g in heads]
        pv = [jnp.dot(vt_ref[j, hsl[g], :], ps[g].astype(BF16), preferred_element_type=F32) for g in heads]
        for g in heads:
            acc_ref[g] = alpha[g] * acc_ref[g] + pv[g]
        return tuple(new_m), tuple(new_l)

    _, l_f = lax.fori_loop(0, i, body, (tuple(m0), tuple(l0)))
    for g in heads:
        o_ref[:, hsl[g]] = (acc_ref[g] / l_f[g]).T.astype(o_ref.dtype)


def _attention(qk, vt3, bias, batch, seq):
    nb = seq // MOBA_BLOCK
    t = batch * seq
    gh = ATTN_HEADS_PER_STEP
    width = gh * HEAD_DIM
    groups = ATTN_HEADS // gh
    return pl.pallas_call(
        functools.partial(_attn_kernel, nb=nb),
        grid=(batch, groups, nb),
        in_specs=[
            pl.BlockSpec((MOBA_BLOCK, width), lambda b, h, i: (b * nb + i, h)),
            pl.BlockSpec((seq, width), lambda b, h, i: (b, groups + h)),
            pl.BlockSpec((nb, width, MOBA_BLOCK), lambda b, h, i: (b, h, 0)),
            pl.BlockSpec((gh, 3, MOBA_BLOCK, MOBA_BLOCK), lambda b, h, i: (h, 0, 0, 0)),
        ],
        out_specs=pl.BlockSpec((MOBA_BLOCK, width), lambda b, h, i: (b * nb + i, h)),
        out_shape=jax.ShapeDtypeStruct((t, ATTN_WIDTH), BF16),
        scratch_shapes=[pltpu.VMEM((gh, nb, HEAD_DIM), F32),
                        pltpu.VMEM((gh, nb, MOBA_BLOCK), F32),
                        pltpu.VMEM((gh, HEAD_DIM, MOBA_BLOCK), F32)],
        compiler_params=_params(("parallel", "parallel", "arbitrary")),
        name="moba_attention",
    )(qk, qk, vt3, bias)


def _rglru_kernel(xr_ref, yr_ref, cw_ref, cb_ref, wa_ref, wx_ref, ba_ref, bx_ref, lam_ref,
                  o_ref, tail_ref, h_ref):
    c = pl.program_id(1)
    tc = xr_ref.shape[0]

    @pl.when(c == 0)
    def _():
        tail_ref[...] = jnp.zeros(tail_ref.shape, F32)
        h_ref[...] = jnp.zeros(h_ref.shape, F32)

    x = xr_ref[...]
    xfull = jnp.concatenate([tail_ref[...], x], axis=0)
    conv = cb_ref[...]
    for k in range(CONV_WIDTH):
        off = 8 - (CONV_WIDTH - 1) + k
        conv = conv + cw_ref[k:k + 1, :] * xfull[off:off + tc, :]
    tail_ref[...] = x[tc - 8:, :]

    xb16 = conv.astype(BF16)
    pre_a, pre_x = [], []
    for g in range(LRU_BLOCKS):
        xs = xb16[:, g * LRU_BLOCK_DIM:(g + 1) * LRU_BLOCK_DIM]
        pre_a.append(jnp.dot(xs, wa_ref[g], preferred_element_type=F32))
        pre_x.append(jnp.dot(xs, wx_ref[g], preferred_element_type=F32))
    r_gate = jax.nn.sigmoid(jnp.concatenate(pre_a, axis=1) + ba_ref[...])
    i_gate = jax.nn.sigmoid(jnp.concatenate(pre_x, axis=1) + bx_ref[...])
    z = -lam_ref[...]
    softplus = jnp.maximum(z, 0.0) + jnp.log1p(jnp.exp(-jnp.abs(z)))
    log_a = (-LRU_C) * r_gate * softplus
    a = jnp.exp(log_a)
    th = jnp.tanh(log_a)
    u = jnp.sqrt(-2.0 * th / (1.0 - th)) * (i_gate * conv)

    row = lax.broadcasted_iota(jnp.int32, a.shape, 0)
    s = 1
    while s < tc:
        a_sh = pltpu.roll(a, s, axis=0)
        u_sh = pltpu.roll(u, s, axis=0)
        valid = row >= s
        u = jnp.where(valid, a * u_sh + u, u)
        a = jnp.where(valid, a * a_sh, a)
        s *= 2
    hh = a * h_ref[...] + u
    h_ref[...] = hh[tc - 1:tc, :]
    o_ref[...] = (hh * _gelu_tanh(yr_ref[...])).astype(o_ref.dtype)


def _rglru(r, conv_w, conv_b, wa, wx, ba, bx, lam, batch, seq, tc=256):
    t = batch * seq
    nc = seq // tc
    w = LRU_WIDTH
    row = lambda v: v.reshape(1, w).astype(F32)
    full2 = lambda shape: pl.BlockSpec(shape, lambda b, c: (0,) * len(shape))
    return pl.pallas_call(
        _rglru_kernel,
        grid=(batch, nc),
        in_specs=[pl.BlockSpec((tc, w), lambda b, c: (b * nc + c, 0)),
                  pl.BlockSpec((tc, w), lambda b, c: (b * nc + c, 1)),
                  full2((CONV_WIDTH, w)), full2((1, w)),
                  full2((LRU_BLOCKS, LRU_BLOCK_DIM, LRU_BLOCK_DIM)),
                  full2((LRU_BLOCKS, LRU_BLOCK_DIM, LRU_BLOCK_DIM)),
                  full2((1, w)), full2((1, w)), full2((1, w))],
        out_specs=pl.BlockSpec((tc, w), lambda b, c: (b * nc + c, 0)),
        out_shape=jax.ShapeDtypeStruct((t, w), BF16),
        scratch_shapes=[pltpu.VMEM((8, w), F32), pltpu.VMEM((1, w), F32)],
        compiler_params=_params(("parallel", "arbitrary")),
        name="rglru",
    )(r, r, conv_w.astype(F32), row(conv_b), wa.astype(BF16), wx.astype(BF16),
      row(ba), row(bx), row(lam))


def _merge_kernel(oa_ref, or_ref, g0_ref, g1_ref, wb0_ref, wb1_ref, wo_ref, x_ref, gn_ref,
                  x1_ref, xn_ref):
    pb0 = jnp.dot(oa_ref[...], wb0_ref[...], preferred_element_type=F32)
    pb1 = jnp.dot(or_ref[...], wb1_ref[...], preferred_element_type=F32)
    merged = jax.nn.sigmoid(g0_ref[...]) * pb0 + jax.nn.sigmoid(g1_ref[...]) * pb1
    x1 = x_ref[...] + jnp.dot(merged.astype(BF16), wo_ref[...], preferred_element_type=F32)
    x1_ref[...] = x1
    xn_ref[...] = _rms(x1, gn_ref[...]).astype(xn_ref.dtype)


def _merge_out(o_att, o_rec, gl, wb0, wb1, w_out, x, g_ffn, tm=256):
    t, d = x.shape
    cw = o_att.shape[1]
    resident = lambda shape: pl.BlockSpec(shape, lambda i: (0, 0), pipeline_mode=pl.Buffered(1))
    return pl.pallas_call(
        _merge_kernel,
        grid=(t // tm,),
        in_specs=[pl.BlockSpec((tm, cw), lambda i: (i, 0)),
                  pl.BlockSpec((tm, cw), lambda i: (i, 0)),
                  pl.BlockSpec((tm, d), lambda i: (i, 0)),
                  pl.BlockSpec((tm, d), lambda i: (i, 1)),
                  resident((cw, d)), resident((cw, d)), resident((d, d)),
                  pl.BlockSpec((tm, d), lambda i: (i, 0)),
                  pl.BlockSpec((1, d), lambda i: (0, 0))],
        out_specs=[pl.BlockSpec((tm, d), lambda i: (i, 0)),
                   pl.BlockSpec((tm, d), lambda i: (i, 0))],
        out_shape=[jax.ShapeDtypeStruct((t, d), F32), jax.ShapeDtypeStruct((t, d), BF16)],
        compiler_params=_params(("parallel",)),
        name="merge_out",
    )(o_att, o_rec, gl, gl, wb0, wb1, w_out, x, g_ffn.reshape(1, d).astype(F32))


def _oddeven_merge(lo, hi, r):
    step = r * 2
    if step < hi - lo:
        yield from _oddeven_merge(lo, hi, step)
        yield from _oddeven_merge(lo + r, hi, step)
        yield from [(i, i + r) for i in range(lo + r, hi - r, step)]
    else:
        yield (lo, lo + r)


def _oddeven_merge_sort(lo, hi):
    if hi - lo >= 1:
        mid = lo + (hi - lo) // 2
        yield from _oddeven_merge_sort(lo, mid)
        yield from _oddeven_merge_sort(mid + 1, hi)
        yield from _oddeven_merge(lo, hi, 1)


def _peer_scores_kernel(q_ref, keys_ref, thr_ref, e1_ref, s2_ref, e2_ref, s_ref, vals_ref, cand_ref):
    nhc = keys_ref.shape[0]
    k = PEER_TOPK
    for hc in range(nhc):
        qs = q_ref[:, hc * PEER_NKEYS:(hc + 1) * PEER_NKEYS]
        s_ref[hc] = lax.dot_general(keys_ref[hc], qs, _NT_DIMS, preferred_element_type=F32)

    n_lvl = PEER_NKEYS // 8
    network = list(_oddeven_merge_sort(0, n_lvl - 1))

    def top_values(hc, carry):
        x = s_ref[hc]
        lvl = [x[8 * v:8 * v + 8, :] for v in range(n_lvl)]
        for a, b in network:
            lvl[a], lvl[b] = jnp.maximum(lvl[a], lvl[b]), jnp.minimum(lvl[a], lvl[b])
        for r in range(k + 1):
            m = jnp.max(lvl[0], axis=0, keepdims=True)
            vals_ref[hc, r:r + 1, :] = m
            hit = lvl[0] == m
            for v in range(min(k - r, n_lvl)):
                below = lvl[v + 1] if v + 1 < n_lvl else LOWEST
                lvl[v] = jnp.where(hit, below, lvl[v])
        return carry

    lax.fori_loop(0, nhc, top_values, 0)

    def head_stats(h, carry):
        v1 = vals_ref[2 * h, 0:k + 1, :]
        v2 = vals_ref[2 * h + 1, 0:k + 1, :]
        cand_ref[...] = jnp.full(cand_ref.shape, LOWEST, F32)
        off = 0
        for a in range(k + 1):
            nb = (k + 1) // (a + 1)
            cand_ref[off:off + nb, :] = v1[a:a + 1, :] + v2[0:nb, :]
            off += nb
        cand = cand_ref[...]
        x = cand
        kth = None
        for r in range(k):
            kth = jnp.max(x, axis=0, keepdims=True)
            x = jnp.where(x == kth, LOWEST, x)
        nxt = jnp.max(x, axis=0, keepdims=True)
        m1 = v1[0:1, :]
        m2 = v2[0:1, :]
        tau = 0.5 * (kth + nxt)
        z = jnp.sum(jnp.where(cand >= tau, jnp.exp(cand - (m1 + m2)), 0.0), axis=0, keepdims=True)
        s1 = s_ref[2 * h]
        s2 = s_ref[2 * h + 1]
        thr_ref[h] = tau - s1
        e1_ref[h] = jnp.exp(s1 - m1)
        e2 = jnp.exp(s2 - m2) * (1.0 / z)
        for c in range(s2_ref.shape[1]):
            s2_ref[h, c] = s2[:, c * 128:(c + 1) * 128]
            e2_ref[h, c] = e2[:, c * 128:(c + 1) * 128]
        return carry

    lax.fori_loop(0, nhc // 2, head_stats, 0)


def _peer_scores(q, keys, tm=256):
    t, _ = q.shape
    nhc = keys.shape[0]
    out_spec = pl.BlockSpec((PEER_HEADS, PEER_NKEYS, tm), lambda i: (0, 0, i))
    out_shape = jax.ShapeDtypeStruct((PEER_HEADS, PEER_NKEYS, t), F32)
    chunk_spec = pl.BlockSpec((PEER_HEADS, tm // 128, PEER_NKEYS, 128), lambda i: (0, i, 0, 0))
    chunk_shape = jax.ShapeDtypeStruct((PEER_HEADS, t // 128, PEER_NKEYS, 128), F32)
    n_cand = sum((PEER_TOPK + 1) // (a + 1) for a in range(PEER_TOPK + 1))
    n_cand = -(-n_cand // 8) * 8
    return pl.pallas_call(
        _peer_scores_kernel,
        grid=(t // tm,),
        in_specs=[pl.BlockSpec((tm, nhc * PEER_NKEYS), lambda i: (i, 0)),
                  pl.BlockSpec(keys.shape, lambda i: (0, 0, 0))],
        out_specs=[out_spec, out_spec, chunk_spec, chunk_spec],
        out_shape=[out_shape, out_shape, chunk_shape, chunk_shape],
        scratch_shapes=[pltpu.VMEM((nhc, PEER_NKEYS, tm), F32),
                        pltpu.VMEM((nhc, PEER_TOPK + 8, tm), F32),
                        pltpu.VMEM((n_cand, tm), F32)],
        compiler_params=_params(("parallel",)),
        name="peer_scores",
    )(q, keys)


def _peer_dense_kernel(xn_ref, thr_ref, e1_ref, s2_ref, e2_ref, u_ref, vta_ref, vtb_ref, acc_ref,
                       xnt_ref, act_ref, p0_ref, p1_ref, *, n_pairs):
    g = pl.program_id(1)
    lane_chunks, tn, _ = act_ref.shape
    rows = tn // PEER_NKEYS
    d_rows = acc_ref.shape[0] // rows

    @pl.when(g == 0)
    def _():
        acc_ref[...] = jnp.zeros(acc_ref.shape, F32)
        p1_ref[...] = jnp.zeros(p1_ref.shape, p1_ref.dtype)
        xnt_ref[...] = xn_ref[...].astype(F32).T.astype(xnt_ref.dtype)

    def half_step(u_rows, tile, p_new, vt_prev_ref, p_prev):
        act = jnp.dot(u_rows, xnt_ref[...], preferred_element_type=F32)
        for lc in range(lane_chunks):
            act_ref[lc] = act[:, lc * 128:(lc + 1) * 128]

        def body(r, carry):
            i1 = tile * rows + r
            roff = pl.multiple_of(r * PEER_NKEYS, PEER_NKEYS)
            thr_rows = [thr_ref[h, pl.ds(i1, 1), :] for h in range(PEER_HEADS)]
            e1_rows = [e1_ref[h, pl.ds(i1, 1), :] for h in range(PEER_HEADS)]
            for lc in range(lane_chunks):
                lanes = slice(lc * 128, (lc + 1) * 128)
                w = jnp.zeros((PEER_NKEYS, 128), F32)
                for h in range(PEER_HEADS):
                    w = w + jnp.where(s2_ref[h, lc] >= thr_rows[h][:, lanes],
                                      e2_ref[h, lc] * e1_rows[h][:, lanes], 0.0)
                a = act_ref[lc, pl.ds(roff, PEER_NKEYS), :]
                p_new[pl.ds(roff, PEER_NKEYS), lanes] = (_gelu_tanh(a) * w).astype(p_new.dtype)
            doff = pl.multiple_of(r * d_rows, d_rows)
            acc_ref[pl.ds(doff, d_rows), :] += jnp.dot(vt_prev_ref[pl.ds(doff, d_rows), :], p_prev[...],
                                                       preferred_element_type=F32)
            return carry

        lax.fori_loop(0, rows, body, 0)

    @pl.when(g < n_pairs)
    def _():
        half_step(u_ref[0:tn, :], 2 * g, p0_ref, vta_ref, p1_ref)
        half_step(u_ref[tn:2 * tn, :], 2 * g + 1, p1_ref, vtb_ref, p0_ref)

    @pl.when(g == n_pairs)
    def _():
        acc_ref[...] += jnp.dot(vta_ref[...], p1_ref[...], preferred_element_type=F32)


def _peer_dense(xn, thr, e1, s2, e2, u_bf, vt_bf, tm=512, tn=512):
    t, d = xn.shape
    e = u_bf.shape[0]
    n_tiles = e // tn
    n_pairs = n_tiles // 2
    route_spec = pl.BlockSpec((PEER_HEADS, PEER_NKEYS, tm), lambda i, g: (0, 0, i))
    chunk_spec = pl.BlockSpec((PEER_HEADS, tm // 128, PEER_NKEYS, 128), lambda i, g: (0, i, 0, 0))
    return pl.pallas_call(
        functools.partial(_peer_dense_kernel, n_pairs=n_pairs),
        grid=(t // tm, n_pairs + 1),
        in_specs=[pl.BlockSpec((tm, d), lambda i, g: (i, 0)),
                  route_spec, route_spec, chunk_spec, chunk_spec,
                  pl.BlockSpec((2 * tn, d), lambda i, g: (jnp.minimum(g, n_pairs - 1), 0)),
                  pl.BlockSpec((d, tn), lambda i, g: (0, jnp.maximum(2 * g - 1, 0))),
                  pl.BlockSpec((d, tn), lambda i, g: (0, jnp.minimum(2 * g, n_tiles - 1)))],
        out_specs=pl.BlockSpec((d, tm), lambda i, g: (0, i)),
        out_shape=jax.ShapeDtypeStruct((d, t), F32),
        scratch_shapes=[pltpu.VMEM((d, tm), BF16), pltpu.VMEM((tm // 128, tn, 128), F32),
                        pltpu.VMEM((tn, tm), BF16), pltpu.VMEM((tn, tm), BF16)],
        compiler_params=_params(("parallel", "arbitrary")),
        name="peer_dense",
    )(xn, thr, e1, s2, e2, u_bf, vt_bf, vt_bf)


def _final_kernel(x1_ref, pt_ref, g_ref, y_ref):
    y_ref[...] = _rms(x1_ref[...] + pt_ref[...].T, g_ref[...])


def _residual_norm(x1, peer_t, g, tm=256):
    t, d = x1.shape
    return pl.pallas_call(
        _final_kernel,
        grid=(t // tm,),
        in_specs=[pl.BlockSpec((tm, d), lambda i: (i, 0)),
                  pl.BlockSpec((d, tm), lambda i: (0, i)),
                  pl.BlockSpec((1, d), lambda i: (0, 0))],
        out_specs=pl.BlockSpec((tm, d), lambda i: (i, 0)),
        out_shape=jax.ShapeDtypeStruct((t, d), F32),
        compiler_params=_params(("parallel",)),
        name="residual_norm",
    )(x1, peer_t, g.reshape(1, d).astype(F32))


def kernel(x, norm_mix_g, w_in, conv_w, conv_b, lru_wa, lru_ba, lru_wx, lru_bx, lru_lambda,
           w_branch, w_out, rel_bias, norm_ffn_g, peer_wq, peer_keys, peer_u, peer_v, norm_final_g):
    batch, seq, d = x.shape
    t = batch * seq
    assert w_in.shape[0] == 1, "single-layer trunk: the final rmsnorm is fused into the PEER epilogue"
    xt = x.reshape(t, d)
    bias = _bias_tiles(rel_bias)
    a_w = ATTN_WIDTH
    w_l = w_in[0]
    h = _rmsnorm(xt, norm_mix_g[0], BF16)
    qk = _matmul(h, w_l[:, :2 * a_w].astype(BF16), BF16, "proj_qk")
    vt3 = _v_transposed(w_l[:, 2 * a_w:3 * a_w].T.astype(BF16), h)
    r = _matmul(h, w_l[:, 3 * a_w:3 * a_w + 2 * LRU_WIDTH].astype(BF16), F32, "proj_rec")
    gl = _matmul(h, w_l[:, 3 * a_w + 2 * LRU_WIDTH:].astype(BF16), F32, "proj_gate")
    o_att = _attention(qk, vt3, bias, batch, seq)
    o_rec = _rglru(r, conv_w[0], conv_b[0], lru_wa[0], lru_wx[0], lru_ba[0], lru_bx[0],
                   lru_lambda[0], batch, seq)
    x1, xn = _merge_out(o_att, o_rec, gl, w_branch[0, 0].astype(BF16),
                        w_branch[0, 1].astype(BF16), w_out[0].astype(BF16), xt, norm_ffn_g[0])
    q = _matmul(xn, peer_wq[0].astype(BF16), BF16, "peer_query")
    keys = peer_keys[0].reshape(PEER_HEADS * 2, PEER_NKEYS, PEER_DKEY // 2).astype(BF16)
    thr, e1, s2, e2 = _peer_scores(q, keys)
    peer_t = _peer_dense(xn, thr, e1, s2, e2, peer_u[0].astype(BF16), peer_v[0].T.astype(BF16))
    y = _residual_norm(x1, peer_t, norm_final_g)
    return y.reshape(batch, seq, d)
```

```python
import functools
import math

import jax
import jax.numpy as jnp
from jax import lax
from jax.experimental import pallas as pl
from jax.experimental.pallas import tpu as pltpu

D_MODEL = 2048
ATTN_HEADS = 8
HEAD_DIM = 128
ATTN_WIDTH = ATTN_HEADS * HEAD_DIM
MOBA_BLOCK = 256
MOBA_TOPK = 3
LRU_WIDTH = 1024
LRU_BLOCKS = 8
LRU_BLOCK_DIM = LRU_WIDTH // LRU_BLOCKS
CONV_WIDTH = 4
LRU_C = 8.0
REL_BUCKETS = 32
REL_MAX_DIST = 128
PEER_HEADS = 8
PEER_NKEYS = 128
PEER_EXPERTS = PEER_NKEYS * PEER_NKEYS
PEER_DKEY = 256
PEER_TOPK = 16
EPS = 1e-6
NEG = -1e30
LOWEST = -3.0e38
PEER_EXPERT_TILE = 1024

V7X_VMEM_LIMIT_BYTES = 56 * 1024 * 1024

F32 = jnp.float32
BF16 = jnp.bfloat16

_NT_DIMS = (((1,), (1,)), ((), ()))


def _params(semantics, flags=None):
    return pltpu.CompilerParams(dimension_semantics=semantics,
                                vmem_limit_bytes=V7X_VMEM_LIMIT_BYTES, flags=flags)


def _gelu_tanh(x):
    c = math.sqrt(2.0 / math.pi)
    return (0.5 * x) * (1.0 + jnp.tanh(x * (c + (0.044715 * c) * (x * x))))


def _rms(x, g):
    ms = jnp.mean(x * x, axis=-1, keepdims=True)
    return x * lax.rsqrt(ms + EPS) * g


def _rmsnorm_kernel(x_ref, g_ref, o_ref):
    o_ref[...] = _rms(x_ref[...], g_ref[...]).astype(o_ref.dtype)


def _rmsnorm(x, g, out_dtype, tm=512):
    t, d = x.shape
    return pl.pallas_call(
        _rmsnorm_kernel,
        grid=(t // tm,),
        in_specs=[pl.BlockSpec((tm, d), lambda i: (i, 0)),
                  pl.BlockSpec((1, d), lambda i: (0, 0))],
        out_specs=pl.BlockSpec((tm, d), lambda i: (i, 0)),
        out_shape=jax.ShapeDtypeStruct((t, d), out_dtype),
        compiler_params=_params(("parallel",)),
        name="rmsnorm",
    )(x, g.reshape(1, d))


def _mm_kernel(a_ref, b_ref, o_ref):
    o_ref[...] = jnp.dot(a_ref[...], b_ref[...],
                         preferred_element_type=F32).astype(o_ref.dtype)


def _matmul(a, b, out_dtype, name, tm=1024, tn=1024):
    m, k = a.shape
    _, n = b.shape
    return pl.pallas_call(
        _mm_kernel,
        grid=(m // tm, n // tn),
        in_specs=[pl.BlockSpec((tm, k), lambda i, j: (i, 0)),
                  pl.BlockSpec((k, tn), lambda i, j: (0, j))],
        out_specs=pl.BlockSpec((tm, tn), lambda i, j: (i, j)),
        out_shape=jax.ShapeDtypeStruct((m, n), out_dtype),
        compiler_params=_params(("parallel", "arbitrary")),
        name=name,
    )(a, b)


def _vt_kernel(w_ref, h_ref, o_ref):
    res = lax.dot_general(w_ref[...], h_ref[...], _NT_DIMS, preferred_element_type=F32)
    for t in range(o_ref.shape[0]):
        o_ref[t] = res[:, t * MOBA_BLOCK:(t + 1) * MOBA_BLOCK].astype(o_ref.dtype)


def _v_transposed(w_t, h, tm=1024):
    c, k = w_t.shape
    t, _ = h.shape
    nb = tm // MOBA_BLOCK
    return pl.pallas_call(
        _vt_kernel,
        grid=(t // tm,),
        in_specs=[pl.BlockSpec((c, k), lambda i: (0, 0)),
                  pl.BlockSpec((tm, k), lambda i: (i, 0))],
        out_specs=pl.BlockSpec((nb, c, MOBA_BLOCK), lambda i: (i, 0, 0)),
        out_shape=jax.ShapeDtypeStruct((t // MOBA_BLOCK, c, MOBA_BLOCK), BF16),
        compiler_params=_params(("parallel",)),
        name="v_transposed",
    )(w_t, h)


def _rel_bucket(dist):
    n = jnp.maximum(dist, 0)
    max_exact = REL_BUCKETS // 2
    nf = jnp.maximum(n, 1).astype(F32)
    large = max_exact + (jnp.log(nf / max_exact) / math.log(REL_MAX_DIST / max_exact)
                         * (REL_BUCKETS - max_exact)).astype(jnp.int32)
    large = jnp.minimum(large, REL_BUCKETS - 1)
    return jnp.where(n < max_exact, n, large)


def _bias_kernel(relb_ref, bucket_ref, o_ref):
    h = pl.program_id(0)
    for t in range(3):
        bk = bucket_ref[t]
        acc = jnp.zeros(bk.shape, F32)
        for b in range(REL_BUCKETS):
            acc = jnp.where(bk == b, relb_ref[b, h], acc)
        if t == 0:
            key = lax.broadcasted_iota(jnp.int32, bk.shape, 0)
            qry = lax.broadcasted_iota(jnp.int32, bk.shape, 1)
            acc = jnp.where(key <= qry, acc, NEG)
        o_ref[0, t] = acc


def _bias_tiles(rel_bias):
    assert REL_MAX_DIST <= MOBA_BLOCK
    key = jnp.arange(MOBA_BLOCK, dtype=jnp.int32)[:, None]
    qry = jnp.arange(MOBA_BLOCK, dtype=jnp.int32)[None, :]
    buckets = jnp.stack([_rel_bucket(t * MOBA_BLOCK + qry - key) for t in range(3)])
    return pl.pallas_call(
        _bias_kernel,
        grid=(ATTN_HEADS,),
        in_specs=[pl.BlockSpec(memory_space=pltpu.SMEM),
                  pl.BlockSpec((3, MOBA_BLOCK, MOBA_BLOCK), lambda h: (0, 0, 0))],
        out_specs=pl.BlockSpec((1, 3, MOBA_BLOCK, MOBA_BLOCK), lambda h: (h, 0, 0, 0)),
        out_shape=jax.ShapeDtypeStruct((ATTN_HEADS, 3, MOBA_BLOCK, MOBA_BLOCK), F32),
        compiler_params=_params(("parallel",)),
        name="attn_bias_tiles",
    )(rel_bias.astype(F32), buckets)


ATTN_HEADS_PER_STEP = 8


def _attn_kernel(q_ref, k_ref, vt_ref, bias_ref, o_ref, kmean_ref, sel_ref, acc_ref, *, nb):
    i = pl.program_id(2)
    scale = HEAD_DIM ** -0.5
    heads = range(ATTN_HEADS_PER_STEP)
    hsl = [slice(g * HEAD_DIM, (g + 1) * HEAD_DIM) for g in heads]

    @pl.when(i == 0)
    def _():
        for g in heads:
            for n in range(nb):
                kb = k_ref[n * MOBA_BLOCK:(n + 1) * MOBA_BLOCK, hsl[g]].astype(F32)
                kmean_ref[g, n:n + 1, :] = jnp.mean(kb, axis=0, keepdims=True)

    qs = [q_ref[:, hsl[g]] for g in heads]
    for g in heads:
        gate = lax.dot_general(kmean_ref[g].astype(BF16), qs[g], _NT_DIMS,
                               preferred_element_type=F32)
        blk = lax.broadcasted_iota(jnp.int32, gate.shape, 0)
        rank = jnp.zeros(gate.shape, F32)
        for m in range(nb):
            gm = gate[m:m + 1, :]
            beats = jnp.where(gm > gate, 1.0, jnp.where(gm == gate, jnp.where(m < blk, 1.0, 0.0), 0.0))
            rank = rank + beats * jnp.where(m < i, 1.0, 0.0)
        sel_ref[g] = jnp.where(blk < i, jnp.where(rank < MOBA_TOPK, 1.0, 0.0), 0.0)

    def scores(g, j, tile):
        koff = pl.multiple_of(j * MOBA_BLOCK, MOBA_BLOCK)
        kj = k_ref[pl.ds(koff, MOBA_BLOCK), hsl[g]]
        s = lax.dot_general(kj, qs[g], _NT_DIMS, preferred_element_type=F32)
        return s * scale + bias_ref[g, tile]

    s0 = [scores(g, i, 0) for g in heads]
    m0 = [jnp.max(s0[g], axis=0, keepdims=True) for g in heads]
    p0 = [jnp.exp(s0[g] - m0[g]) for g in heads]
    l0 = [jnp.sum(p0[g], axis=0, keepdims=True) for g in heads]
    for g in heads:
        acc_ref[g] = jnp.dot(vt_ref[i, hsl[g], :], p0[g].astype(BF16), preferred_element_type=F32)

    def body(j, carry):
        ms, ls = carry
        tile = jnp.minimum(i - j, 2)
        ss = [scores(g, j, tile) for g in heads]
        ss = [jnp.where(sel_ref[g, pl.ds(j, 1), :] > 0.5, ss[g], NEG) for g in heads]
        new_m = [jnp.maximum(ms[g], jnp.max(ss[g], axis=0, keepdims=True)) for g in heads]
        alpha = [jnp.exp(ms[g] - new_m[g]) for g in heads]
        ps = [jnp.exp(ss[g] - new_m[g]) for g in heads]
        new_l = [alpha[g] * ls[g] + jnp.sum(ps[g], axis=0, keepdims=True) for g in heads]
        pv = [jnp.dot(vt_ref[j, hsl[g], :], ps[g].astype(BF16), preferred_element_type=F32) for g in heads]
        for g in heads:
            acc_ref[g] = alpha[g] * acc_ref[g] + pv[g]
        return tuple(new_m), tuple(new_l)

    _, l_f = lax.fori_loop(0, i, body, (tuple(m0), tuple(l0)))
    for g in heads:
        o_ref[:, hsl[g]] = (acc_ref[g] / l_f[g]).T.astype(o_ref.dtype)


def _attention(qk, vt3, bias, batch, seq):
    nb = seq // MOBA_BLOCK
    t = batch * seq
    gh = ATTN_HEADS_PER_STEP
    width = gh * HEAD_DIM
    groups = ATTN_HEADS // gh
    return pl.pallas_call(
        functools.partial(_attn_kernel, nb=nb),
        grid=(batch, groups, nb),
        in_specs=[
            pl.BlockSpec((MOBA_BLOCK, width), lambda b, h, i: (b * nb + i, h)),
            pl.BlockSpec((seq, width), lambda b, h, i: (b, groups + h)),
            pl.BlockSpec((nb, width, MOBA_BLOCK), lambda b, h, i: (b, h, 0)),
            pl.BlockSpec((gh, 3, MOBA_BLOCK, MOBA_BLOCK), lambda b, h, i: (h, 0, 0, 0)),
        ],
        out_specs=pl.BlockSpec((MOBA_BLOCK, width), lambda b, h, i: (b * nb + i, h)),
        out_shape=jax.ShapeDtypeStruct((t, ATTN_WIDTH), BF16),
        scratch_shapes=[pltpu.VMEM((gh, nb, HEAD_DIM), F32),
                        pltpu.VMEM((gh, nb, MOBA_BLOCK), F32),
                        pltpu.VMEM((gh, HEAD_DIM, MOBA_BLOCK), F32)],
        compiler_params=_params(("parallel", "parallel", "arbitrary")),
        name="moba_attention",
    )(qk, qk, vt3, bias)


def _rglru_kernel(xr_ref, yr_ref, cw_ref, cb_ref, wa_ref, wx_ref, ba_ref, bx_ref, lam_ref,
                  o_ref, tail_ref, h_ref):
    c = pl.program_id(1)
    tc = xr_ref.shape[0]

    @pl.when(c == 0)
    def _():
        tail_ref[...] = jnp.zeros(tail_ref.shape, F32)
        h_ref[...] = jnp.zeros(h_ref.shape, F32)

    x = xr_ref[...]
    xfull = jnp.concatenate([tail_ref[...], x], axis=0)
    conv = cb_ref[...]
    for k in range(CONV_WIDTH):
        off = 8 - (CONV_WIDTH - 1) + k
        conv = conv + cw_ref[k:k + 1, :] * xfull[off:off + tc, :]
    tail_ref[...] = x[tc - 8:, :]

    xb16 = conv.astype(BF16)
    pre_a, pre_x = [], []
    for g in range(LRU_BLOCKS):
        xs = xb16[:, g * LRU_BLOCK_DIM:(g + 1) * LRU_BLOCK_DIM]
        pre_a.append(jnp.dot(xs, wa_ref[g], preferred_element_type=F32))
        pre_x.append(jnp.dot(xs, wx_ref[g], preferred_element_type=F32))
    r_gate = jax.nn.sigmoid(jnp.concatenate(pre_a, axis=1) + ba_ref[...])
    i_gate = jax.nn.sigmoid(jnp.concatenate(pre_x, axis=1) + bx_ref[...])
    z = -lam_ref[...]
    softplus = jnp.maximum(z, 0.0) + jnp.log1p(jnp.exp(-jnp.abs(z)))
    log_a = (-LRU_C) * r_gate * softplus
    a = jnp.exp(log_a)
    th = jnp.tanh(log_a)
    u = jnp.sqrt(-2.0 * th / (1.0 - th)) * (i_gate * conv)

    row = lax.broadcasted_iota(jnp.int32, a.shape, 0)
    s = 1
    while s < tc:
        a_sh = pltpu.roll(a, s, axis=0)
        u_sh = pltpu.roll(u, s, axis=0)
        valid = row >= s
        u = jnp.where(valid, a * u_sh + u, u)
        a = jnp.where(valid, a * a_sh, a)
        s *= 2
    hh = a * h_ref[...] + u
    h_ref[...] = hh[tc - 1:tc, :]
    o_ref[...] = (hh * _gelu_tanh(yr_ref[...])).astype(o_ref.dtype)


def _rglru(r, conv_w, conv_b, wa, wx, ba, bx, lam, batch, seq, tc=256):
    t = batch * seq
    nc = seq // tc
    w = LRU_WIDTH
    row = lambda v: v.reshape(1, w).astype(F32)
    full2 = lambda shape: pl.BlockSpec(shape, lambda b, c: (0,) * len(shape))
    return pl.pallas_call(
        _rglru_kernel,
        grid=(batch, nc),
        in_specs=[pl.BlockSpec((tc, w), lambda b, c: (b * nc + c, 0)),
                  pl.BlockSpec((tc, w), lambda b, c: (b * nc + c, 1)),
                  full2((CONV_WIDTH, w)), full2((1, w)),
                  full2((LRU_BLOCKS, LRU_BLOCK_DIM, LRU_BLOCK_DIM)),
                  full2((LRU_BLOCKS, LRU_BLOCK_DIM, LRU_BLOCK_DIM)),
                  full2((1, w)), full2((1, w)), full2((1, w))],
        out_specs=pl.BlockSpec((tc, w), lambda b, c: (b * nc + c, 0)),
        out_shape=jax.ShapeDtypeStruct((t, w), BF16),
        scratch_shapes=[pltpu.VMEM((8, w), F32), pltpu.VMEM((1, w), F32)],
        compiler_params=_params(("parallel", "arbitrary")),
        name="rglru",
    )(r, r, conv_w.astype(F32), row(conv_b), wa.astype(BF16), wx.astype(BF16),
      row(ba), row(bx), row(lam))


def _merge_kernel(oa_ref, or_ref, g0_ref, g1_ref, wb0_ref, wb1_ref, wo_ref, x_ref, gn_ref,
                  x1_ref, xn_ref):
    pb0 = jnp.dot(oa_ref[...], wb0_ref[...], preferred_element_type=F32)
    pb1 = jnp.dot(or_ref[...], wb1_ref[...], preferred_element_type=F32)
    merged = jax.nn.sigmoid(g0_ref[...]) * pb0 + jax.nn.sigmoid(g1_ref[...]) * pb1
    x1 = x_ref[...] + jnp.dot(merged.astype(BF16), wo_ref[...], preferred_element_type=F32)
    x1_ref[...] = x1
    xn_ref[...] = _rms(x1, gn_ref[...]).astype(xn_ref.dtype)


def _merge_out(o_att, o_rec, gl, wb0, wb1, w_out, x, g_ffn, tm=256):
    t, d = x.shape
    cw = o_att.shape[1]
    resident = lambda shape: pl.BlockSpec(shape, lambda i: (0, 0), pipeline_mode=pl.Buffered(1))
    return pl.pallas_call(
        _merge_kernel,
        grid=(t // tm,),
        in_specs=[pl.BlockSpec((tm, cw), lambda i: (i, 0)),
                  pl.BlockSpec((tm, cw), lambda i: (i, 0)),
                  pl.BlockSpec((tm, d), lambda i: (i, 0)),
                  pl.BlockSpec((tm, d), lambda i: (i, 1)),
                  resident((cw, d)), resident((cw, d)), resident((d, d)),
                  pl.BlockSpec((tm, d), lambda i: (i, 0)),
                  pl.BlockSpec((1, d), lambda i: (0, 0))],
        out_specs=[pl.BlockSpec((tm, d), lambda i: (i, 0)),
                   pl.BlockSpec((tm, d), lambda i: (i, 0))],
        out_shape=[jax.ShapeDtypeStruct((t, d), F32), jax.ShapeDtypeStruct((t, d), BF16)],
        compiler_params=_params(("parallel",)),
        name="merge_out",
    )(o_att, o_rec, gl, gl, wb0, wb1, w_out, x, g_ffn.reshape(1, d).astype(F32))


def _oddeven_merge(lo, hi, r):
    step = r * 2
    if step < hi - lo:
        yield from _oddeven_merge(lo, hi, step)
        yield from _oddeven_merge(lo + r, hi, step)
        yield from [(i, i + r) for i in range(lo + r, hi - r, step)]
    else:
        yield (lo, lo + r)


def _oddeven_merge_sort(lo, hi):
    if hi - lo >= 1:
        mid = lo + (hi - lo) // 2
        yield from _oddeven_merge_sort(lo, mid)
        yield from _oddeven_merge_sort(mid + 1, hi)
        yield from _oddeven_merge(lo, hi, 1)


def _peer_scores_kernel(q_ref, keys_ref, thr_ref, e1_ref, s2_ref, e2_ref, s_ref, vals_ref, cand_ref):
    nhc = keys_ref.shape[0]
    k = PEER_TOPK
    for hc in range(nhc):
        qs = q_ref[:, hc * PEER_NKEYS:(hc + 1) * PEER_NKEYS]
        s_ref[hc] = lax.dot_general(keys_ref[hc], qs, _NT_DIMS, preferred_element_type=F32)

    n_lvl = PEER_NKEYS // 8
    network = list(_oddeven_merge_sort(0, n_lvl - 1))

    def top_values(hc, carry):
        x = s_ref[hc]
        lvl = [x[8 * v:8 * v + 8, :] for v in range(n_lvl)]
        for a, b in network:
            lvl[a], lvl[b] = jnp.maximum(lvl[a], lvl[b]), jnp.minimum(lvl[a], lvl[b])
        for r in range(k + 1):
            m = jnp.max(lvl[0], axis=0, keepdims=True)
            vals_ref[hc, r:r + 1, :] = m
            hit = lvl[0] == m
            for v in range(min(k - r, n_lvl)):
                below = lvl[v + 1] if v + 1 < n_lvl else LOWEST
                lvl[v] = jnp.where(hit, below, lvl[v])
        return carry

    lax.fori_loop(0, nhc, top_values, 0)

    def head_stats(h, carry):
        v1 = vals_ref[2 * h, 0:k + 1, :]
        v2 = vals_ref[2 * h + 1, 0:k + 1, :]
        cand_ref[...] = jnp.full(cand_ref.shape, LOWEST, F32)
        off = 0
        for a in range(k + 1):
            nb = (k + 1) // (a + 1)
            cand_ref[off:off + nb, :] = v1[a:a + 1, :] + v2[0:nb, :]
            off += nb
        cand = cand_ref[...]
        x = cand
        kth = None
        for r in range(k):
            kth = jnp.max(x, axis=0, keepdims=True)
            x = jnp.where(x == kth, LOWEST, x)
        nxt = jnp.max(x, axis=0, keepdims=True)
        m1 = v1[0:1, :]
        m2 = v2[0:1, :]
        tau = 0.5 * (kth + nxt)
        z = jnp.sum(jnp.where(cand >= tau, jnp.exp(cand - (m1 + m2)), 0.0), axis=0, keepdims=True)
        s1 = s_ref[2 * h]
        s2 = s_ref[2 * h + 1]
        thr_ref[h] = tau - s1
        e1_ref[h] = jnp.exp(s1 - m1)
        e2 = jnp.exp(s2 - m2) * (1.0 / z)
        for c in range(s2_ref.shape[1]):
            s2_ref[h, c] = s2[:, c * 128:(c + 1) * 128]
            e2_ref[h, c] = e2[:, c * 128:(c + 1) * 128]
        return carry

    lax.fori_loop(0, nhc // 2, head_stats, 0)


def _peer_scores(q, keys, tm=256):
    t, _ = q.shape
    nhc = keys.shape[0]
    out_spec = pl.BlockSpec((PEER_HEADS, PEER_NKEYS, tm), lambda i: (0, 0, i))
    out_shape = jax.ShapeDtypeStruct((PEER_HEADS, PEER_NKEYS, t), F32)
    chunk_spec = pl.BlockSpec((PEER_HEADS, tm // 128, PEER_NKEYS, 128), lambda i: (0, i, 0, 0))
    chunk_shape = jax.ShapeDtypeStruct((PEER_HEADS, t // 128, PEER_NKEYS, 128), F32)
    n_cand = sum((PEER_TOPK + 1) // (a + 1) for a in range(PEER_TOPK + 1))
    n_cand = -(-n_cand // 8) * 8
    return pl.pallas_call(
        _peer_scores_kernel,
        grid=(t // tm,),
        in_specs=[pl.BlockSpec((tm, nhc * PEER_NKEYS), lambda i: (i, 0)),
                  pl.BlockSpec(keys.shape, lambda i: (0, 0, 0))],
        out_specs=[out_spec, out_spec, chunk_spec, chunk_spec],
        out_shape=[out_shape, out_shape, chunk_shape, chunk_shape],
        scratch_shapes=[pltpu.VMEM((nhc, PEER_NKEYS, tm), F32),
                        pltpu.VMEM((nhc, PEER_TOPK + 8, tm), F32),
                        pltpu.VMEM((n_cand, tm), F32)],
        compiler_params=_params(("parallel",)),
        name="peer_scores",
    )(q, keys)


def _peer_dense_kernel(xn_ref, thr_ref, e1_ref, s2_ref, e2_ref, u_ref, vt_ref, acc_ref,
                       xnt_ref, act_ref, p_ref):
    j = pl.program_id(1)
    lane_chunks, tn, _ = act_ref.shape
    rows = tn // PEER_NKEYS

    @pl.when(j == 0)
    def _():
        acc_ref[...] = jnp.zeros(acc_ref.shape, F32)
        xnt_ref[...] = xn_ref[...].astype(F32).T.astype(xnt_ref.dtype)

    act = jnp.dot(u_ref[...], xnt_ref[...], preferred_element_type=F32)
    for lc in range(lane_chunks):
        act_ref[lc] = act[:, lc * 128:(lc + 1) * 128]

    def weigh(r, carry):
        i1 = j * rows + r
        roff = pl.multiple_of(r * PEER_NKEYS, PEER_NKEYS)
        thr_rows = [thr_ref[h, pl.ds(i1, 1), :] for h in range(PEER_HEADS)]
        e1_rows = [e1_ref[h, pl.ds(i1, 1), :] for h in range(PEER_HEADS)]
        for lc in range(lane_chunks):
            lanes = slice(lc * 128, (lc + 1) * 128)
            w = jnp.zeros((PEER_NKEYS, 128), F32)
            for h in range(PEER_HEADS):
                w = w + jnp.where(s2_ref[h, lc] >= thr_rows[h][:, lanes],
                                  e2_ref[h, lc] * e1_rows[h][:, lanes], 0.0)
            a = act_ref[lc, pl.ds(roff, PEER_NKEYS), :]
            p_ref[pl.ds(roff, PEER_NKEYS), lanes] = (_gelu_tanh(a) * w).astype(p_ref.dtype)
        return carry

    lax.fori_loop(0, rows, weigh, 0)
    res = jnp.dot(vt_ref[...], p_ref[...], preferred_element_type=F32)
    for lc in range(lane_chunks):
        acc_ref[lc] += res[:, lc * 128:(lc + 1) * 128]


def _peer_dense(xn, thr, e1, s2, e2, u_bf, vt_tiles, tm=512):
    t, d = xn.shape
    n_tiles, _, tn = vt_tiles.shape
    route_spec = pl.BlockSpec((PEER_HEADS, PEER_NKEYS, tm), lambda i, j: (0, 0, i))
    chunk_spec = pl.BlockSpec((PEER_HEADS, tm // 128, PEER_NKEYS, 128), lambda i, j: (0, i, 0, 0))
    return pl.pallas_call(
        _peer_dense_kernel,
        grid=(t // tm, n_tiles),
        in_specs=[pl.BlockSpec((tm, d), lambda i, j: (i, 0)),
                  route_spec, route_spec, chunk_spec, chunk_spec,
                  pl.BlockSpec((tn, d), lambda i, j: (j, 0)),
                  pl.BlockSpec((None, d, tn), lambda i, j: (j, 0, 0))],
        out_specs=pl.BlockSpec((tm // 128, d, 128), lambda i, j: (i, 0, 0)),
        out_shape=jax.ShapeDtypeStruct((t // 128, d, 128), F32),
        scratch_shapes=[pltpu.VMEM((d, tm), BF16), pltpu.VMEM((tm // 128, tn, 128), F32),
                        pltpu.VMEM((tn, tm), BF16)],
        compiler_params=_params(("parallel", "arbitrary")),
        name="peer_dense",
    )(xn, thr, e1, s2, e2, u_bf, vt_tiles)


def _final_kernel(x1_ref, pt_ref, g_ref, y_ref):
    for c in range(pt_ref.shape[0]):
        rows = slice(c * 128, (c + 1) * 128)
        y_ref[rows, :] = _rms(x1_ref[rows, :] + pt_ref[c].T, g_ref[...])


def _residual_norm(x1, peer_chunks, g, tm=256):
    t, d = x1.shape
    return pl.pallas_call(
        _final_kernel,
        grid=(t // tm,),
        in_specs=[pl.BlockSpec((tm, d), lambda i: (i, 0)),
                  pl.BlockSpec((tm // 128, d, 128), lambda i: (i, 0, 0)),
                  pl.BlockSpec((1, d), lambda i: (0, 0))],
        out_specs=pl.BlockSpec((tm, d), lambda i: (i, 0)),
        out_shape=jax.ShapeDtypeStruct((t, d), F32),
        compiler_params=_params(("parallel",)),
        name="residual_norm",
    )(x1, peer_chunks, g.reshape(1, d).astype(F32))


def kernel(x, norm_mix_g, w_in, conv_w, conv_b, lru_wa, lru_ba, lru_wx, lru_bx, lru_lambda,
           w_branch, w_out, rel_bias, norm_ffn_g, peer_wq, peer_keys, peer_u, peer_v, norm_final_g):
    batch, seq, d = x.shape
    t = batch * seq
    assert w_in.shape[0] == 1, "single-layer trunk: the final rmsnorm is fused into the PEER epilogue"
    xt = x.reshape(t, d)
    bias = _bias_tiles(rel_bias)
    a_w = ATTN_WIDTH
    w_l = w_in[0]
    h = _rmsnorm(xt, norm_mix_g[0], BF16)
    qk = _matmul(h, w_l[:, :2 * a_w].astype(BF16), BF16, "proj_qk")
    vt3 = _v_transposed(w_l[:, 2 * a_w:3 * a_w].T.astype(BF16), h)
    r = _matmul(h, w_l[:, 3 * a_w:3 * a_w + 2 * LRU_WIDTH].astype(BF16), F32, "proj_rec")
    gl = _matmul(h, w_l[:, 3 * a_w + 2 * LRU_WIDTH:].astype(BF16), F32, "proj_gate")
    o_att = _attention(qk, vt3, bias, batch, seq)
    o_rec = _rglru(r, conv_w[0], conv_b[0], lru_wa[0], lru_wx[0], lru_ba[0], lru_bx[0],
                   lru_lambda[0], batch, seq)
    x1, xn = _merge_out(o_att, o_rec, gl, w_branch[0, 0].astype(BF16),
                        w_branch[0, 1].astype(BF16), w_out[0].astype(BF16), xt, norm_ffn_g[0])
    q = _matmul(xn, peer_wq[0].astype(BF16), BF16, "peer_query")
    keys = peer_keys[0].reshape(PEER_HEADS * 2, PEER_NKEYS, PEER_DKEY // 2).astype(BF16)
    thr, e1, s2, e2 = _peer_scores(q, keys)
    n_exp = peer_v.shape[1]
    vt_tiles = peer_v[0].reshape(n_exp // PEER_EXPERT_TILE, PEER_EXPERT_TILE, d).transpose(0, 2, 1)
    peer_out = _peer_dense(xn, thr, e1, s2, e2, peer_u[0].astype(BF16), vt_tiles.astype(BF16))
    y = _residual_norm(x1, peer_out, norm_final_g)
    return y.reshape(batch, seq, d)
```

```python
import functools
import math

import jax
import jax.numpy as jnp
from jax import lax
from jax.experimental import pallas as pl
from jax.experimental.pallas import tpu as pltpu

D_MODEL = 2048
ATTN_HEADS = 8
HEAD_DIM = 128
ATTN_WIDTH = ATTN_HEADS * HEAD_DIM
MOBA_BLOCK = 256
MOBA_TOPK = 3
LRU_WIDTH = 1024
LRU_BLOCKS = 8
LRU_BLOCK_DIM = LRU_WIDTH // LRU_BLOCKS
CONV_WIDTH = 4
LRU_C = 8.0
REL_BUCKETS = 32
REL_MAX_DIST = 128
PEER_HEADS = 8
PEER_NKEYS = 128
PEER_EXPERTS = PEER_NKEYS * PEER_NKEYS
PEER_DKEY = 256
PEER_TOPK = 16
EPS = 1e-6
NEG = -1e30
LOWEST = -3.0e38
PEER_EXPERT_TILE = 1024

V7X_VMEM_LIMIT_BYTES = 56 * 1024 * 1024

F32 = jnp.float32
BF16 = jnp.bfloat16

_NT_DIMS = (((1,), (1,)), ((), ()))


def _params(semantics, flags=None):
    return pltpu.CompilerParams(dimension_semantics=semantics,
                                vmem_limit_bytes=V7X_VMEM_LIMIT_BYTES, flags=flags)


def _gelu_tanh(x):
    c = math.sqrt(2.0 / math.pi)
    return (0.5 * x) * (1.0 + jnp.tanh(x * (c + (0.044715 * c) * (x * x))))


def _rms(x, g):
    ms = jnp.mean(x * x, axis=-1, keepdims=True)
    return x * lax.rsqrt(ms + EPS) * g


def _rmsnorm_kernel(x_ref, g_ref, o_ref):
    o_ref[...] = _rms(x_ref[...], g_ref[...]).astype(o_ref.dtype)


def _rmsnorm(x, g, out_dtype, tm=512):
    t, d = x.shape
    return pl.pallas_call(
        _rmsnorm_kernel,
        grid=(t // tm,),
        in_specs=[pl.BlockSpec((tm, d), lambda i: (i, 0)),
                  pl.BlockSpec((1, d), lambda i: (0, 0))],
        out_specs=pl.BlockSpec((tm, d), lambda i: (i, 0)),
        out_shape=jax.ShapeDtypeStruct((t, d), out_dtype),
        compiler_params=_params(("parallel",)),
        name="rmsnorm",
    )(x, g.reshape(1, d))


def _mm_kernel(a_ref, b_ref, o_ref):
    o_ref[...] = jnp.dot(a_ref[...], b_ref[...],
                         preferred_element_type=F32).astype(o_ref.dtype)


def _matmul(a, b, out_dtype, name, tm=1024, tn=1024):
    m, k = a.shape
    _, n = b.shape
    return pl.pallas_call(
        _mm_kernel,
        grid=(m // tm, n // tn),
        in_specs=[pl.BlockSpec((tm, k), lambda i, j: (i, 0)),
                  pl.BlockSpec((k, tn), lambda i, j: (0, j))],
        out_specs=pl.BlockSpec((tm, tn), lambda i, j: (i, j)),
        out_shape=jax.ShapeDtypeStruct((m, n), out_dtype),
        compiler_params=_params(("parallel", "arbitrary")),
        name=name,
    )(a, b)


def _vt_kernel(w_ref, h_ref, o_ref):
    res = lax.dot_general(w_ref[...], h_ref[...], _NT_DIMS, preferred_element_type=F32)
    for t in range(o_ref.shape[0]):
        o_ref[t] = res[:, t * MOBA_BLOCK:(t + 1) * MOBA_BLOCK].astype(o_ref.dtype)


def _v_transposed(w_t, h, tm=1024):
    c, k = w_t.shape
    t, _ = h.shape
    nb = tm // MOBA_BLOCK
    return pl.pallas_call(
        _vt_kernel,
        grid=(t // tm,),
        in_specs=[pl.BlockSpec((c, k), lambda i: (0, 0)),
                  pl.BlockSpec((tm, k), lambda i: (i, 0))],
        out_specs=pl.BlockSpec((nb, c, MOBA_BLOCK), lambda i: (i, 0, 0)),
        out_shape=jax.ShapeDtypeStruct((t // MOBA_BLOCK, c, MOBA_BLOCK), BF16),
        compiler_params=_params(("parallel",)),
        name="v_transposed",
    )(w_t, h)


def _rel_bucket(dist):
    n = jnp.maximum(dist, 0)
    max_exact = REL_BUCKETS // 2
    nf = jnp.maximum(n, 1).astype(F32)
    large = max_exact + (jnp.log(nf / max_exact) / math.log(REL_MAX_DIST / max_exact)
                         * (REL_BUCKETS - max_exact)).astype(jnp.int32)
    large = jnp.minimum(large, REL_BUCKETS - 1)
    return jnp.where(n < max_exact, n, large)


def _bias_kernel(relb_ref, bucket_ref, o_ref):
    h = pl.program_id(0)
    for t in range(3):
        bk = bucket_ref[t]
        acc = jnp.zeros(bk.shape, F32)
        for b in range(REL_BUCKETS):
            acc = jnp.where(bk == b, relb_ref[b, h], acc)
        if t == 0:
            key = lax.broadcasted_iota(jnp.int32, bk.shape, 0)
            qry = lax.broadcasted_iota(jnp.int32, bk.shape, 1)
            acc = jnp.where(key <= qry, acc, NEG)
        o_ref[0, t] = acc


def _bias_tiles(rel_bias):
    assert REL_MAX_DIST <= MOBA_BLOCK
    key = jnp.arange(MOBA_BLOCK, dtype=jnp.int32)[:, None]
    qry = jnp.arange(MOBA_BLOCK, dtype=jnp.int32)[None, :]
    buckets = jnp.stack([_rel_bucket(t * MOBA_BLOCK + qry - key) for t in range(3)])
    return pl.pallas_call(
        _bias_kernel,
        grid=(ATTN_HEADS,),
        in_specs=[pl.BlockSpec(memory_space=pltpu.SMEM),
                  pl.BlockSpec((3, MOBA_BLOCK, MOBA_BLOCK), lambda h: (0, 0, 0))],
        out_specs=pl.BlockSpec((1, 3, MOBA_BLOCK, MOBA_BLOCK), lambda h: (h, 0, 0, 0)),
        out_shape=jax.ShapeDtypeStruct((ATTN_HEADS, 3, MOBA_BLOCK, MOBA_BLOCK), F32),
        compiler_params=_params(("parallel",)),
        name="attn_bias_tiles",
    )(rel_bias.astype(F32), buckets)


ATTN_HEADS_PER_STEP = 8


def _attn_kernel(q_ref, k_ref, vt_ref, bias_ref, o_ref, kmean_ref, sel_ref, acc_ref, *, nb):
    i = pl.program_id(2)
    scale = HEAD_DIM ** -0.5
    heads = range(ATTN_HEADS_PER_STEP)
    hsl = [slice(g * HEAD_DIM, (g + 1) * HEAD_DIM) for g in heads]

    @pl.when(i == 0)
    def _():
        for g in heads:
            for n in range(nb):
                kb = k_ref[n * MOBA_BLOCK:(n + 1) * MOBA_BLOCK, hsl[g]].astype(F32)
                kmean_ref[g, n:n + 1, :] = jnp.mean(kb, axis=0, keepdims=True)

    qs = [q_ref[:, hsl[g]] for g in heads]
    for g in heads:
        gate = lax.dot_general(kmean_ref[g].astype(BF16), qs[g], _NT_DIMS,
                               preferred_element_type=F32)
        blk = lax.broadcasted_iota(jnp.int32, gate.shape, 0)
        rank = jnp.zeros(gate.shape, F32)
        for m in range(nb):
            gm = gate[m:m + 1, :]
            beats = jnp.where(gm > gate, 1.0, jnp.where(gm == gate, jnp.where(m < blk, 1.0, 0.0), 0.0))
            rank = rank + beats * jnp.where(m < i, 1.0, 0.0)
        sel_ref[g] = jnp.where(blk < i, jnp.where(rank < MOBA_TOPK, 1.0, 0.0), 0.0)

    def scores(g, j, tile):
        koff = pl.multiple_of(j * MOBA_BLOCK, MOBA_BLOCK)
        kj = k_ref[pl.ds(koff, MOBA_BLOCK), hsl[g]]
        s = lax.dot_general(kj, qs[g], _NT_DIMS, preferred_element_type=F32)
        return s * scale + bias_ref[g, tile]

    s0 = [scores(g, i, 0) for g in heads]
    m0 = [jnp.max(s0[g], axis=0, keepdims=True) for g in heads]
    p0 = [jnp.exp(s0[g] - m0[g]) for g in heads]
    l0 = [jnp.sum(p0[g], axis=0, keepdims=True) for g in heads]
    for g in heads:
        acc_ref[g] = jnp.dot(vt_ref[i, hsl[g], :], p0[g].astype(BF16), preferred_element_type=F32)

    def body(j, carry):
        ms, ls = carry
        tile = jnp.minimum(i - j, 2)
        ss = [scores(g, j, tile) for g in heads]
        ss = [jnp.where(sel_ref[g, pl.ds(j, 1), :] > 0.5, ss[g], NEG) for g in heads]
        new_m = [jnp.maximum(ms[g], jnp.max(ss[g], axis=0, keepdims=True)) for g in heads]
        alpha = [jnp.exp(ms[g] - new_m[g]) for g in heads]
        ps = [jnp.exp(ss[g] - new_m[g]) for g in heads]
        new_l = [alpha[g] * ls[g] + jnp.sum(ps[g], axis=0, keepdims=True) for g in heads]
        pv = [jnp.dot(vt_ref[j, hsl[g], :], ps[g].astype(BF16), preferred_element_type=F32) for g in heads]
        for g in heads:
            acc_ref[g] = alpha[g] * acc_ref[g] + pv[g]
        return tuple(new_m), tuple(new_l)

    _, l_f = lax.fori_loop(0, i, body, (tuple(m0), tuple(l0)))
    for g in heads:
        o_ref[:, hsl[g]] = (acc_ref[g] / l_f[g]).T.astype(o_ref.dtype)


def _attention(qk, vt3, bias, batch, seq):
    nb = seq // MOBA_BLOCK
    t = batch * seq
    gh = ATTN_HEADS_PER_STEP
    width = gh * HEAD_DIM
    groups = ATTN_HEADS // gh
    return pl.pallas_call(
        functools.partial(_attn_kernel, nb=nb),
        grid=(batch, groups, nb),
        in_specs=[
            pl.BlockSpec((MOBA_BLOCK, width), lambda b, h, i: (b * nb + i, h)),
            pl.BlockSpec((seq, width), lambda b, h, i: (b, groups + h)),
            pl.BlockSpec((nb, width, MOBA_BLOCK), lambda b, h, i: (b, h, 0)),
            pl.BlockSpec((gh, 3, MOBA_BLOCK, MOBA_BLOCK), lambda b, h, i: (h, 0, 0, 0)),
        ],
        out_specs=pl.BlockSpec((MOBA_BLOCK, width), lambda b, h, i: (b * nb + i, h)),
        out_shape=jax.ShapeDtypeStruct((t, ATTN_WIDTH), BF16),
        scratch_shapes=[pltpu.VMEM((gh, nb, HEAD_DIM), F32),
                        pltpu.VMEM((gh, nb, MOBA_BLOCK), F32),
                        pltpu.VMEM((gh, HEAD_DIM, MOBA_BLOCK), F32)],
        compiler_params=_params(("parallel", "parallel", "arbitrary")),
        name="moba_attention",
    )(qk, qk, vt3, bias)


def _rglru_kernel(xr_ref, yr_ref, cw_ref, cb_ref, wa_ref, wx_ref, ba_ref, bx_ref, lam_ref,
                  o_ref, tail_ref, h_ref):
    c = pl.program_id(1)
    tc = xr_ref.shape[0]

    @pl.when(c == 0)
    def _():
        tail_ref[...] = jnp.zeros(tail_ref.shape, F32)
        h_ref[...] = jnp.zeros(h_ref.shape, F32)

    x = xr_ref[...]
    xfull = jnp.concatenate([tail_ref[...], x], axis=0)
    conv = cb_ref[...]
    for k in range(CONV_WIDTH):
        off = 8 - (CONV_WIDTH - 1) + k
        conv = conv + cw_ref[k:k + 1, :] * xfull[off:off + tc, :]
    tail_ref[...] = x[tc - 8:, :]

    xb16 = conv.astype(BF16)
    pre_a, pre_x = [], []
    for g in range(LRU_BLOCKS):
        xs = xb16[:, g * LRU_BLOCK_DIM:(g + 1) * LRU_BLOCK_DIM]
        pre_a.append(jnp.dot(xs, wa_ref[g], preferred_element_type=F32))
        pre_x.append(jnp.dot(xs, wx_ref[g], preferred_element_type=F32))
    r_gate = jax.nn.sigmoid(jnp.concatenate(pre_a, axis=1) + ba_ref[...])
    i_gate = jax.nn.sigmoid(jnp.concatenate(pre_x, axis=1) + bx_ref[...])
    z = -lam_ref[...]
    softplus = jnp.maximum(z, 0.0) + jnp.log1p(jnp.exp(-jnp.abs(z)))
    log_a = (-LRU_C) * r_gate * softplus
    a = jnp.exp(log_a)
    th = jnp.tanh(log_a)
    y2 = -2.0 * th / (1.0 - th)
    u = jnp.where(y2 > 0.0, y2 * lax.rsqrt(y2), 0.0) * (i_gate * conv)

    grp_shape = (tc // 8, 8, a.shape[1])
    a = a.reshape(grp_shape)
    u = u.reshape(grp_shape)
    sub = lax.broadcasted_iota(jnp.int32, grp_shape, 1)
    s = 1
    while s < 8:
        a_sh = pltpu.roll(a, s, axis=1)
        u_sh = pltpu.roll(u, s, axis=1)
        valid = sub >= s
        u = jnp.where(valid, a * u_sh + u, u)
        a = jnp.where(valid, a * a_sh, a)
        s *= 2
    h = h_ref[...]
    gy = _gelu_tanh(yr_ref[...])
    for grp in range(tc // 8):
        rows = slice(grp * 8, (grp + 1) * 8)
        hh = a[grp] * h + u[grp]
        h = hh[7:8, :]
        o_ref[rows, :] = (hh * gy[rows, :]).astype(o_ref.dtype)
    h_ref[...] = h


def _rglru(r, conv_w, conv_b, wa, wx, ba, bx, lam, batch, seq, tc=256):
    t = batch * seq
    nc = seq // tc
    w = LRU_WIDTH
    row = lambda v: v.reshape(1, w).astype(F32)
    full2 = lambda shape: pl.BlockSpec(shape, lambda b, c: (0,) * len(shape))
    return pl.pallas_call(
        _rglru_kernel,
        grid=(batch, nc),
        in_specs=[pl.BlockSpec((tc, w), lambda b, c: (b * nc + c, 0)),
                  pl.BlockSpec((tc, w), lambda b, c: (b * nc + c, 1)),
                  full2((CONV_WIDTH, w)), full2((1, w)),
                  full2((LRU_BLOCKS, LRU_BLOCK_DIM, LRU_BLOCK_DIM)),
                  full2((LRU_BLOCKS, LRU_BLOCK_DIM, LRU_BLOCK_DIM)),
                  full2((1, w)), full2((1, w)), full2((1, w))],
        out_specs=pl.BlockSpec((tc, w), lambda b, c: (b * nc + c, 0)),
        out_shape=jax.ShapeDtypeStruct((t, w), BF16),
        scratch_shapes=[pltpu.VMEM((8, w), F32), pltpu.VMEM((1, w), F32)],
        compiler_params=_params(("parallel", "arbitrary")),
        name="rglru",
    )(r, r, conv_w.astype(F32), row(conv_b), wa.astype(BF16), wx.astype(BF16),
      row(ba), row(bx), row(lam))


def _merge_kernel(oa_ref, or_ref, g0_ref, g1_ref, wb0_ref, wb1_ref, wo_ref, x_ref, gn_ref,
                  x1_ref, xn_ref):
    pb0 = jnp.dot(oa_ref[...], wb0_ref[...], preferred_element_type=F32)
    pb1 = jnp.dot(or_ref[...], wb1_ref[...], preferred_element_type=F32)
    merged = jax.nn.sigmoid(g0_ref[...]) * pb0 + jax.nn.sigmoid(g1_ref[...]) * pb1
    x1 = x_ref[...] + jnp.dot(merged.astype(BF16), wo_ref[...], preferred_element_type=F32)
    x1_ref[...] = x1
    xn_ref[...] = _rms(x1, gn_ref[...]).astype(xn_ref.dtype)


def _merge_out(o_att, o_rec, gl, wb0, wb1, w_out, x, g_ffn, tm=256):
    t, d = x.shape
    cw = o_att.shape[1]
    resident = lambda shape: pl.BlockSpec(shape, lambda i: (0, 0), pipeline_mode=pl.Buffered(1))
    return pl.pallas_call(
        _merge_kernel,
        grid=(t // tm,),
        in_specs=[pl.BlockSpec((tm, cw), lambda i: (i, 0)),
                  pl.BlockSpec((tm, cw), lambda i: (i, 0)),
                  pl.BlockSpec((tm, d), lambda i: (i, 0)),
                  pl.BlockSpec((tm, d), lambda i: (i, 1)),
                  resident((cw, d)), resident((cw, d)), resident((d, d)),
                  pl.BlockSpec((tm, d), lambda i: (i, 0)),
                  pl.BlockSpec((1, d), lambda i: (0, 0))],
        out_specs=[pl.BlockSpec((tm, d), lambda i: (i, 0)),
                   pl.BlockSpec((tm, d), lambda i: (i, 0))],
        out_shape=[jax.ShapeDtypeStruct((t, d), F32), jax.ShapeDtypeStruct((t, d), BF16)],
        compiler_params=_params(("parallel",)),
        name="merge_out",
    )(o_att, o_rec, gl, gl, wb0, wb1, w_out, x, g_ffn.reshape(1, d).astype(F32))


def _oddeven_merge(lo, hi, r):
    step = r * 2
    if step < hi - lo:
        yield from _oddeven_merge(lo, hi, step)
        yield from _oddeven_merge(lo + r, hi, step)
        yield from [(i, i + r) for i in range(lo + r, hi - r, step)]
    else:
        yield (lo, lo + r)


def _oddeven_merge_sort(lo, hi):
    if hi - lo >= 1:
        mid = lo + (hi - lo) // 2
        yield from _oddeven_merge_sort(lo, mid)
        yield from _oddeven_merge_sort(mid + 1, hi)
        yield from _oddeven_merge(lo, hi, 1)


def _peer_scores_kernel(q_ref, keys_ref, thr_ref, e1_ref, s2_ref, e2_ref, s_ref, vals_ref, cand_ref):
    nhc = keys_ref.shape[0]
    k = PEER_TOPK
    for hc in range(nhc):
        qs = q_ref[:, hc * PEER_NKEYS:(hc + 1) * PEER_NKEYS]
        s_ref[hc] = lax.dot_general(keys_ref[hc], qs, _NT_DIMS, preferred_element_type=F32)

    n_lvl = PEER_NKEYS // 8
    network = list(_oddeven_merge_sort(0, n_lvl - 1))

    def top_values(hc, carry):
        x = s_ref[hc]
        lvl = [x[8 * v:8 * v + 8, :] for v in range(n_lvl)]
        for a, b in network:
            lvl[a], lvl[b] = jnp.maximum(lvl[a], lvl[b]), jnp.minimum(lvl[a], lvl[b])
        for r in range(k + 1):
            m = jnp.max(lvl[0], axis=0, keepdims=True)
            vals_ref[hc, r:r + 1, :] = m
            hit = lvl[0] == m
            for v in range(min(k - r, n_lvl)):
                below = lvl[v + 1] if v + 1 < n_lvl else LOWEST
                lvl[v] = jnp.where(hit, below, lvl[v])
        return carry

    lax.fori_loop(0, nhc, top_values, 0)

    def head_stats(h, carry):
        v1 = vals_ref[2 * h, 0:k + 1, :]
        v2 = vals_ref[2 * h + 1, 0:k + 1, :]
        cand_ref[...] = jnp.full(cand_ref.shape, LOWEST, F32)
        off = 0
        for a in range(k + 1):
            nb = (k + 1) // (a + 1)
            cand_ref[off:off + nb, :] = v1[a:a + 1, :] + v2[0:nb, :]
            off += nb
        cand = cand_ref[...]
        x = cand
        kth = None
        for r in range(k):
            kth = jnp.max(x, axis=0, keepdims=True)
            x = jnp.where(x == kth, LOWEST, x)
        nxt = jnp.max(x, axis=0, keepdims=True)
        m1 = v1[0:1, :]
        m2 = v2[0:1, :]
        tau = 0.5 * (kth + nxt)
        z = jnp.sum(jnp.where(cand >= tau, jnp.exp(cand - (m1 + m2)), 0.0), axis=0, keepdims=True)
        s1 = s_ref[2 * h]
        s2 = s_ref[2 * h + 1]
        thr_ref[h] = tau - s1
        e1_ref[h] = jnp.exp(s1 - m1)
        e2 = jnp.exp(s2 - m2) * (1.0 / z)
        for c in range(s2_ref.shape[1]):
            s2_ref[h, c] = s2[:, c * 128:(c + 1) * 128]
            e2_ref[h, c] = e2[:, c * 128:(c + 1) * 128]
        return carry

    lax.fori_loop(0, nhc // 2, head_stats, 0)


def _peer_scores(q, keys, tm=256):
    t, _ = q.shape
    nhc = keys.shape[0]
    out_spec = pl.BlockSpec((PEER_HEADS, PEER_NKEYS, tm), lambda i: (0, 0, i))
    out_shape = jax.ShapeDtypeStruct((PEER_HEADS, PEER_NKEYS, t), F32)
    chunk_spec = pl.BlockSpec((PEER_HEADS, tm // 128, PEER_NKEYS, 128), lambda i: (0, i, 0, 0))
    chunk_shape = jax.ShapeDtypeStruct((PEER_HEADS, t // 128, PEER_NKEYS, 128), F32)
    n_cand = sum((PEER_TOPK + 1) // (a + 1) for a in range(PEER_TOPK + 1))
    n_cand = -(-n_cand // 8) * 8
    return pl.pallas_call(
        _peer_scores_kernel,
        grid=(t // tm,),
        in_specs=[pl.BlockSpec((tm, nhc * PEER_NKEYS), lambda i: (i, 0)),
                  pl.BlockSpec(keys.shape, lambda i: (0, 0, 0))],
        out_specs=[out_spec, out_spec, chunk_spec, chunk_spec],
        out_shape=[out_shape, out_shape, chunk_shape, chunk_shape],
        scratch_shapes=[pltpu.VMEM((nhc, PEER_NKEYS, tm), F32),
                        pltpu.VMEM((nhc, PEER_TOPK + 8, tm), F32),
                        pltpu.VMEM((n_cand, tm), F32)],
        compiler_params=_params(("parallel",)),
        name="peer_scores",
    )(q, keys)


def _transpose_kernel(x_ref, o_ref):
    o_ref[...] = x_ref[...].T.astype(o_ref.dtype)


def _transposed_tiles(table, tn):
    e, d = table.shape
    return pl.pallas_call(
        _transpose_kernel,
        grid=(e // tn,),
        in_specs=[pl.BlockSpec((tn, d), lambda i: (i, 0))],
        out_specs=pl.BlockSpec((None, d, tn), lambda i: (i, 0, 0)),
        out_shape=jax.ShapeDtypeStruct((e // tn, d, tn), BF16),
        compiler_params=_params(("parallel",)),
        name="expert_value_tiles",
    )(table)


def _peer_dense_kernel(xn_ref, thr_ref, e1_ref, s2_ref, e2_ref, u_ref, vt_ref, acc_ref,
                       xnt_ref, act_ref, p_ref):
    j = pl.program_id(1)
    lane_chunks, tn, _ = act_ref.shape
    rows = tn // PEER_NKEYS

    @pl.when(j == 0)
    def _():
        acc_ref[...] = jnp.zeros(acc_ref.shape, F32)
        xnt_ref[...] = xn_ref[...].astype(F32).T.astype(xnt_ref.dtype)

    act = jnp.dot(u_ref[...], xnt_ref[...], preferred_element_type=F32)
    for lc in range(lane_chunks):
        act_ref[lc] = act[:, lc * 128:(lc + 1) * 128]

    def weigh(r, carry):
        i1 = j * rows + r
        roff = pl.multiple_of(r * PEER_NKEYS, PEER_NKEYS)
        thr_rows = [thr_ref[h, pl.ds(i1, 1), :] for h in range(PEER_HEADS)]
        e1_rows = [e1_ref[h, pl.ds(i1, 1), :] for h in range(PEER_HEADS)]
        for lc in range(lane_chunks):
            lanes = slice(lc * 128, (lc + 1) * 128)
            w = None
            for h in range(PEER_HEADS):
                w_h = jnp.where(s2_ref[h, lc] >= thr_rows[h][:, lanes],
                                e2_ref[h, lc] * e1_rows[h][:, lanes], 0.0)
                w = w_h if w is None else w + w_h
            a = act_ref[lc, pl.ds(roff, PEER_NKEYS), :]
            p_ref[pl.ds(roff, PEER_NKEYS), lanes] = (_gelu_tanh(a) * w).astype(p_ref.dtype)
        return carry

    lax.fori_loop(0, rows, weigh, 0)
    res = jnp.dot(vt_ref[...], p_ref[...], preferred_element_type=F32)
    for lc in range(lane_chunks):
        acc_ref[lc] += res[:, lc * 128:(lc + 1) * 128]


def _peer_dense(xn, thr, e1, s2, e2, u_bf, vt_tiles, tm=512):
    t, d = xn.shape
    n_tiles, _, tn = vt_tiles.shape
    route_spec = pl.BlockSpec((PEER_HEADS, PEER_NKEYS, tm), lambda i, j: (0, 0, i))
    chunk_spec = pl.BlockSpec((PEER_HEADS, tm // 128, PEER_NKEYS, 128), lambda i, j: (0, i, 0, 0))
    return pl.pallas_call(
        _peer_dense_kernel,
        grid=(t // tm, n_tiles),
        in_specs=[pl.BlockSpec((tm, d), lambda i, j: (i, 0)),
                  route_spec, route_spec, chunk_spec, chunk_spec,
                  pl.BlockSpec((tn, d), lambda i, j: (j, 0)),
                  pl.BlockSpec((None, d, tn), lambda i, j: (j, 0, 0))],
        out_specs=pl.BlockSpec((tm // 128, d, 128), lambda i, j: (i, 0, 0)),
        out_shape=jax.ShapeDtypeStruct((t // 128, d, 128), F32),
        scratch_shapes=[pltpu.VMEM((d, tm), BF16), pltpu.VMEM((tm // 128, tn, 128), F32),
                        pltpu.VMEM((tn, tm), BF16)],
        compiler_params=_params(("parallel", "arbitrary")),
        name="peer_dense",
    )(xn, thr, e1, s2, e2, u_bf, vt_tiles)


def _final_kernel(x1_ref, pt_ref, g_ref, y_ref):
    for c in range(pt_ref.shape[0]):
        rows = slice(c * 128, (c + 1) * 128)
        y_ref[rows, :] = _rms(x1_ref[rows, :] + pt_ref[c].T, g_ref[...])


def _residual_norm(x1, peer_chunks, g, tm=256):
    t, d = x1.shape
    return pl.pallas_call(
        _final_kernel,
        grid=(t // tm,),
        in_specs=[pl.BlockSpec((tm, d), lambda i: (i, 0)),
                  pl.BlockSpec((tm // 128, d, 128), lambda i: (i, 0, 0)),
                  pl.BlockSpec((1, d), lambda i: (0, 0))],
        out_specs=pl.BlockSpec((tm, d), lambda i: (i, 0)),
        out_shape=jax.ShapeDtypeStruct((t, d), F32),
        compiler_params=_params(("parallel",)),
        name="residual_norm",
    )(x1, peer_chunks, g.reshape(1, d).astype(F32))


def kernel(x, norm_mix_g, w_in, conv_w, conv_b, lru_wa, lru_ba, lru_wx, lru_bx, lru_lambda,
           w_branch, w_out, rel_bias, norm_ffn_g, peer_wq, peer_keys, peer_u, peer_v, norm_final_g):
    batch, seq, d = x.shape
    t = batch * seq
    assert w_in.shape[0] == 1, "single-layer trunk: the final rmsnorm is fused into the PEER epilogue"
    xt = x.reshape(t, d)
    bias = _bias_tiles(rel_bias)
    a_w = ATTN_WIDTH
    w_l = w_in[0]
    h = _rmsnorm(xt, norm_mix_g[0], BF16)
    qk = _matmul(h, w_l[:, :2 * a_w].astype(BF16), BF16, "proj_qk")
    vt3 = _v_transposed(w_l[:, 2 * a_w:3 * a_w].T.astype(BF16), h)
    r = _matmul(h, w_l[:, 3 * a_w:3 * a_w + 2 * LRU_WIDTH].astype(BF16), F32, "proj_rec")
    gl = _matmul(h, w_l[:, 3 * a_w + 2 * LRU_WIDTH:].astype(BF16), F32, "proj_gate")
    o_att = _attention(qk, vt3, bias, batch, seq)
    o_rec = _rglru(r, conv_w[0], conv_b[0], lru_wa[0], lru_wx[0], lru_ba[0], lru_bx[0],
                   lru_lambda[0], batch, seq)
    x1, xn = _merge_out(o_att, o_rec, gl, w_branch[0, 0].astype(BF16),
                        w_branch[0, 1].astype(BF16), w_out[0].astype(BF16), xt, norm_ffn_g[0])
    q = _matmul(xn, peer_wq[0].astype(BF16), BF16, "peer_query")
    keys = peer_keys[0].reshape(PEER_HEADS * 2, PEER_NKEYS, PEER_DKEY // 2).astype(BF16)
    thr, e1, s2, e2 = _peer_scores(q, keys)
    vt_tiles = _transposed_tiles(peer_v[0], PEER_EXPERT_TILE)
    peer_out = _peer_dense(xn, thr, e1, s2, e2, peer_u[0].astype(BF16), vt_tiles)
    y = _residual_norm(x1, peer_out, norm_final_g)
    return y.reshape(batch, seq, d)
```

```python
import functools
import math

import jax
import jax.numpy as jnp
from jax import lax
from jax.experimental import pallas as pl
from jax.experimental.pallas import tpu as pltpu

D_MODEL = 2048
ATTN_HEADS = 8
HEAD_DIM = 128
ATTN_WIDTH = ATTN_HEADS * HEAD_DIM
MOBA_BLOCK = 256
MOBA_TOPK = 3
LRU_WIDTH = 1024
LRU_BLOCKS = 8
LRU_BLOCK_DIM = LRU_WIDTH // LRU_BLOCKS
CONV_WIDTH = 4
LRU_C = 8.0
REL_BUCKETS = 32
REL_MAX_DIST = 128
PEER_HEADS = 8
PEER_NKEYS = 128
PEER_EXPERTS = PEER_NKEYS * PEER_NKEYS
PEER_DKEY = 256
PEER_TOPK = 16
EPS = 1e-6
NEG = -1e30
LOWEST = -3.0e38
PEER_EXPERT_TILE = 1024

V7X_VMEM_LIMIT_BYTES = 56 * 1024 * 1024

F32 = jnp.float32
BF16 = jnp.bfloat16

_NT_DIMS = (((1,), (1,)), ((), ()))


def _params(semantics, flags=None):
    return pltpu.CompilerParams(dimension_semantics=semantics,
                                vmem_limit_bytes=V7X_VMEM_LIMIT_BYTES, flags=flags)


def _gelu_tanh_x2(x):
    c = math.sqrt(2.0 / math.pi)
    return x * (1.0 + jnp.tanh(x * (c + (0.044715 * c) * (x * x))))


def _gelu_tanh(x):
    return 0.5 * _gelu_tanh_x2(x)


def _rms(x, g):
    ms = jnp.mean(x * x, axis=-1, keepdims=True)
    return x * lax.rsqrt(ms + EPS) * g


def _rmsnorm_kernel(x_ref, g_ref, o_ref):
    o_ref[...] = _rms(x_ref[...], g_ref[...]).astype(o_ref.dtype)


def _rmsnorm(x, g, out_dtype, tm=512):
    t, d = x.shape
    return pl.pallas_call(
        _rmsnorm_kernel,
        grid=(t // tm,),
        in_specs=[pl.BlockSpec((tm, d), lambda i: (i, 0)),
                  pl.BlockSpec((1, d), lambda i: (0, 0))],
        out_specs=pl.BlockSpec((tm, d), lambda i: (i, 0)),
        out_shape=jax.ShapeDtypeStruct((t, d), out_dtype),
        compiler_params=_params(("parallel",)),
        name="rmsnorm",
    )(x, g.reshape(1, d))


def _mm_kernel(a_ref, b_ref, o_ref):
    o_ref[...] = jnp.dot(a_ref[...], b_ref[...],
                         preferred_element_type=F32).astype(o_ref.dtype)


def _matmul(a, b, out_dtype, name, tm=1024, tn=1024):
    m, k = a.shape
    _, n = b.shape
    return pl.pallas_call(
        _mm_kernel,
        grid=(m // tm, n // tn),
        in_specs=[pl.BlockSpec((tm, k), lambda i, j: (i, 0)),
                  pl.BlockSpec((k, tn), lambda i, j: (0, j))],
        out_specs=pl.BlockSpec((tm, tn), lambda i, j: (i, j)),
        out_shape=jax.ShapeDtypeStruct((m, n), out_dtype),
        compiler_params=_params(("parallel", "arbitrary")),
        name=name,
    )(a, b)


def _vt_kernel(w_ref, h_ref, o_ref):
    res = lax.dot_general(w_ref[...], h_ref[...], _NT_DIMS, preferred_element_type=F32)
    for t in range(o_ref.shape[0]):
        o_ref[t] = res[:, t * MOBA_BLOCK:(t + 1) * MOBA_BLOCK].astype(o_ref.dtype)


def _v_transposed(w_t, h, tm=1024):
    c, k = w_t.shape
    t, _ = h.shape
    nb = tm // MOBA_BLOCK
    return pl.pallas_call(
        _vt_kernel,
        grid=(t // tm,),
        in_specs=[pl.BlockSpec((c, k), lambda i: (0, 0)),
                  pl.BlockSpec((tm, k), lambda i: (i, 0))],
        out_specs=pl.BlockSpec((nb, c, MOBA_BLOCK), lambda i: (i, 0, 0)),
        out_shape=jax.ShapeDtypeStruct((t // MOBA_BLOCK, c, MOBA_BLOCK), BF16),
        compiler_params=_params(("parallel",)),
        name="v_transposed",
    )(w_t, h)


def _rel_bucket(dist):
    n = jnp.maximum(dist, 0)
    max_exact = REL_BUCKETS // 2
    nf = jnp.maximum(n, 1).astype(F32)
    large = max_exact + (jnp.log(nf / max_exact) / math.log(REL_MAX_DIST / max_exact)
                         * (REL_BUCKETS - max_exact)).astype(jnp.int32)
    large = jnp.minimum(large, REL_BUCKETS - 1)
    return jnp.where(n < max_exact, n, large)


def _bias_kernel(relb_ref, bucket_ref, o_ref):
    h = pl.program_id(0)
    for t in range(3):
        bk = bucket_ref[t]
        acc = jnp.zeros(bk.shape, F32)
        for b in range(REL_BUCKETS):
            acc = jnp.where(bk == b, relb_ref[b, h], acc)
        if t == 0:
            key = lax.broadcasted_iota(jnp.int32, bk.shape, 0)
            qry = lax.broadcasted_iota(jnp.int32, bk.shape, 1)
            acc = jnp.where(key <= qry, acc, NEG)
        o_ref[0, t] = acc


def _bias_tiles(rel_bias):
    assert REL_MAX_DIST <= MOBA_BLOCK
    key = jnp.arange(MOBA_BLOCK, dtype=jnp.int32)[:, None]
    qry = jnp.arange(MOBA_BLOCK, dtype=jnp.int32)[None, :]
    buckets = jnp.stack([_rel_bucket(t * MOBA_BLOCK + qry - key) for t in range(3)])
    return pl.pallas_call(
        _bias_kernel,
        grid=(ATTN_HEADS,),
        in_specs=[pl.BlockSpec(memory_space=pltpu.SMEM),
                  pl.BlockSpec((3, MOBA_BLOCK, MOBA_BLOCK), lambda h: (0, 0, 0))],
        out_specs=pl.BlockSpec((1, 3, MOBA_BLOCK, MOBA_BLOCK), lambda h: (h, 0, 0, 0)),
        out_shape=jax.ShapeDtypeStruct((ATTN_HEADS, 3, MOBA_BLOCK, MOBA_BLOCK), F32),
        compiler_params=_params(("parallel",)),
        name="attn_bias_tiles",
    )(rel_bias.astype(F32), buckets)


ATTN_HEADS_PER_STEP = 8


def _attn_kernel(q_ref, k_ref, vt_ref, bias_ref, o_ref, kmean_ref, sel_ref, acc_ref, *, nb):
    i = pl.program_id(2)
    scale = HEAD_DIM ** -0.5
    heads = range(ATTN_HEADS_PER_STEP)
    hsl = [slice(g * HEAD_DIM, (g + 1) * HEAD_DIM) for g in heads]

    @pl.when(i == 0)
    def _():
        for g in heads:
            for n in range(nb):
                kb = k_ref[n * MOBA_BLOCK:(n + 1) * MOBA_BLOCK, hsl[g]].astype(F32)
                kmean_ref[g, n:n + 1, :] = jnp.mean(kb, axis=0, keepdims=True)

    qs = [q_ref[:, hsl[g]] for g in heads]
    for g in heads:
        gate = lax.dot_general(kmean_ref[g].astype(BF16), qs[g], _NT_DIMS,
                               preferred_element_type=F32)
        blk = lax.broadcasted_iota(jnp.int32, gate.shape, 0)
        rank = jnp.zeros(gate.shape, F32)
        for m in range(nb):
            gm = gate[m:m + 1, :]
            beats = jnp.where(gm > gate, 1.0, jnp.where(gm == gate, jnp.where(m < blk, 1.0, 0.0), 0.0))
            rank = rank + beats * jnp.where(m < i, 1.0, 0.0)
        sel_ref[g] = jnp.where(blk < i, jnp.where(rank < MOBA_TOPK, 1.0, 0.0), 0.0)

    def scores(g, j, tile):
        koff = pl.multiple_of(j * MOBA_BLOCK, MOBA_BLOCK)
        kj = k_ref[pl.ds(koff, MOBA_BLOCK), hsl[g]]
        s = lax.dot_general(kj, qs[g], _NT_DIMS, preferred_element_type=F32)
        return s * scale + bias_ref[g, tile]

    s0 = [scores(g, i, 0) for g in heads]
    m0 = [jnp.max(s0[g], axis=0, keepdims=True) for g in heads]
    p0 = [jnp.exp(s0[g] - m0[g]) for g in heads]
    l0 = [jnp.sum(p0[g], axis=0, keepdims=True) for g in heads]
    for g in heads:
        acc_ref[g] = jnp.dot(vt_ref[i, hsl[g], :], p0[g].astype(BF16), preferred_element_type=F32)

    def body(j, carry):
        ms, ls = carry
        tile = jnp.minimum(i - j, 2)
        ss = [scores(g, j, tile) for g in heads]
        ss = [jnp.where(sel_ref[g, pl.ds(j, 1), :] > 0.5, ss[g], NEG) for g in heads]
        new_m = [jnp.maximum(ms[g], jnp.max(ss[g], axis=0, keepdims=True)) for g in heads]
        alpha = [jnp.exp(ms[g] - new_m[g]) for g in heads]
        ps = [jnp.exp(ss[g] - new_m[g]) for g in heads]
        new_l = [alpha[g] * ls[g] + jnp.sum(ps[g], axis=0, keepdims=True) for g in heads]
        pv = [jnp.dot(vt_ref[j, hsl[g], :], ps[g].astype(BF16), preferred_element_type=F32) for g in heads]
        for g in heads:
            acc_ref[g] = alpha[g] * acc_ref[g] + pv[g]
        return tuple(new_m), tuple(new_l)

    _, l_f = lax.fori_loop(0, i, body, (tuple(m0), tuple(l0)))
    for g in heads:
        o_ref[:, hsl[g]] = (acc_ref[g] / l_f[g]).T.astype(o_ref.dtype)


def _attention(qk, vt3, bias, batch, seq):
    nb = seq // MOBA_BLOCK
    t = batch * seq
    gh = ATTN_HEADS_PER_STEP
    width = gh * HEAD_DIM
    groups = ATTN_HEADS // gh
    return pl.pallas_call(
        functools.partial(_attn_kernel, nb=nb),
        grid=(batch, groups, nb),
        in_specs=[
            pl.BlockSpec((MOBA_BLOCK, width), lambda b, h, i: (b * nb + i, h)),
            pl.BlockSpec((seq, width), lambda b, h, i: (b, groups + h)),
            pl.BlockSpec((nb, width, MOBA_BLOCK), lambda b, h, i: (b, h, 0)),
            pl.BlockSpec((gh, 3, MOBA_BLOCK, MOBA_BLOCK), lambda b, h, i: (h, 0, 0, 0)),
        ],
        out_specs=pl.BlockSpec((MOBA_BLOCK, width), lambda b, h, i: (b * nb + i, h)),
        out_shape=jax.ShapeDtypeStruct((t, ATTN_WIDTH), BF16),
        scratch_shapes=[pltpu.VMEM((gh, nb, HEAD_DIM), F32),
                        pltpu.VMEM((gh, nb, MOBA_BLOCK), F32),
                        pltpu.VMEM((gh, HEAD_DIM, MOBA_BLOCK), F32)],
        compiler_params=_params(("parallel", "parallel", "arbitrary")),
        name="moba_attention",
    )(qk, qk, vt3, bias)


def _rglru_kernel(xr_ref, yr_ref, cw_ref, cb_ref, wa_ref, wx_ref, ba_ref, bx_ref, lam_ref,
                  o_ref, tail_ref, h_ref):
    c = pl.program_id(1)
    tc = xr_ref.shape[0]

    @pl.when(c == 0)
    def _():
        tail_ref[...] = jnp.zeros(tail_ref.shape, F32)
        h_ref[...] = jnp.zeros(h_ref.shape, F32)

    x = xr_ref[...]
    grp_shape = (tc // 8, 8, x.shape[1])
    x3 = x.reshape(grp_shape)
    tail3 = tail_ref[...].reshape(1, 8, x.shape[1])
    sub = lax.broadcasted_iota(jnp.int32, grp_shape, 1)
    conv = cb_ref[...] + cw_ref[CONV_WIDTH - 1:CONV_WIDTH, :] * x3
    for s in range(1, CONV_WIDTH):
        k = CONV_WIDTH - 1 - s
        rolled = pltpu.roll(x3, s, axis=1)
        rolled_prev = jnp.concatenate([pltpu.roll(tail3, s, axis=1), rolled[:-1]], axis=0)
        conv = conv + cw_ref[k:k + 1, :] * jnp.where(sub >= s, rolled, rolled_prev)
    conv = conv.reshape(x.shape)
    tail_ref[...] = x[tc - 8:, :]

    xb16 = conv.astype(BF16)
    pre_a, pre_x = [], []
    for g in range(LRU_BLOCKS):
        xs = xb16[:, g * LRU_BLOCK_DIM:(g + 1) * LRU_BLOCK_DIM]
        pre_a.append(jnp.dot(xs, wa_ref[g], preferred_element_type=F32))
        pre_x.append(jnp.dot(xs, wx_ref[g], preferred_element_type=F32))
    r_gate = jax.nn.sigmoid(jnp.concatenate(pre_a, axis=1) + ba_ref[...])
    i_gate = jax.nn.sigmoid(jnp.concatenate(pre_x, axis=1) + bx_ref[...])
    z = -lam_ref[...]
    softplus = jnp.maximum(z, 0.0) + jnp.log1p(jnp.exp(-jnp.abs(z)))
    log_a = (-LRU_C) * r_gate * softplus
    a = jnp.exp(log_a)
    th = jnp.tanh(log_a)
    y2 = -2.0 * th / (1.0 - th)
    u = jnp.where(y2 > 0.0, y2 * lax.rsqrt(y2), 0.0) * (i_gate * conv)

    grp_shape = (tc // 8, 8, a.shape[1])
    a = a.reshape(grp_shape)
    u = u.reshape(grp_shape)
    sub = lax.broadcasted_iota(jnp.int32, grp_shape, 1)
    s = 1
    while s < 8:
        a_sh = pltpu.roll(a, s, axis=1)
        u_sh = pltpu.roll(u, s, axis=1)
        valid = sub >= s
        u = jnp.where(valid, a * u_sh + u, u)
        a = jnp.where(valid, a * a_sh, a)
        s *= 2
    h = h_ref[...]
    gy = _gelu_tanh(yr_ref[...])
    for grp in range(tc // 8):
        rows = slice(grp * 8, (grp + 1) * 8)
        hh = a[grp] * h + u[grp]
        h = hh[7:8, :]
        o_ref[rows, :] = (hh * gy[rows, :]).astype(o_ref.dtype)
    h_ref[...] = h


def _rglru(r, conv_w, conv_b, wa, wx, ba, bx, lam, batch, seq, tc=256):
    t = batch * seq
    nc = seq // tc
    w = LRU_WIDTH
    row = lambda v: v.reshape(1, w).astype(F32)
    full2 = lambda shape: pl.BlockSpec(shape, lambda b, c: (0,) * len(shape))
    return pl.pallas_call(
        _rglru_kernel,
        grid=(batch, nc),
        in_specs=[pl.BlockSpec((tc, w), lambda b, c: (b * nc + c, 0)),
                  pl.BlockSpec((tc, w), lambda b, c: (b * nc + c, 1)),
                  full2((CONV_WIDTH, w)), full2((1, w)),
                  full2((LRU_BLOCKS, LRU_BLOCK_DIM, LRU_BLOCK_DIM)),
                  full2((LRU_BLOCKS, LRU_BLOCK_DIM, LRU_BLOCK_DIM)),
                  full2((1, w)), full2((1, w)), full2((1, w))],
        out_specs=pl.BlockSpec((tc, w), lambda b, c: (b * nc + c, 0)),
        out_shape=jax.ShapeDtypeStruct((t, w), BF16),
        scratch_shapes=[pltpu.VMEM((8, w), F32), pltpu.VMEM((1, w), F32)],
        compiler_params=_params(("parallel", "arbitrary")),
        name="rglru",
    )(r, r, conv_w.astype(F32), row(conv_b), wa.astype(BF16), wx.astype(BF16),
      row(ba), row(bx), row(lam))


def _merge_kernel(oa_ref, or_ref, g0_ref, g1_ref, wb0_ref, wb1_ref, wo_ref, x_ref, gn_ref,
                  x1_ref, xn_ref):
    pb0 = jnp.dot(oa_ref[...], wb0_ref[...], preferred_element_type=F32)
    pb1 = jnp.dot(or_ref[...], wb1_ref[...], preferred_element_type=F32)
    merged = jax.nn.sigmoid(g0_ref[...]) * pb0 + jax.nn.sigmoid(g1_ref[...]) * pb1
    x1 = x_ref[...] + jnp.dot(merged.astype(BF16), wo_ref[...], preferred_element_type=F32)
    x1_ref[...] = x1
    xn_ref[...] = _rms(x1, gn_ref[...]).astype(xn_ref.dtype)


def _merge_out(o_att, o_rec, gl, wb0, wb1, w_out, x, g_ffn, tm=256):
    t, d = x.shape
    cw = o_att.shape[1]
    resident = lambda shape: pl.BlockSpec(shape, lambda i: (0, 0), pipeline_mode=pl.Buffered(1))
    return pl.pallas_call(
        _merge_kernel,
        grid=(t // tm,),
        in_specs=[pl.BlockSpec((tm, cw), lambda i: (i, 0)),
                  pl.BlockSpec((tm, cw), lambda i: (i, 0)),
                  pl.BlockSpec((tm, d), lambda i: (i, 0)),
                  pl.BlockSpec((tm, d), lambda i: (i, 1)),
                  resident((cw, d)), resident((cw, d)), resident((d, d)),
                  pl.BlockSpec((tm, d), lambda i: (i, 0)),
                  pl.BlockSpec((1, d), lambda i: (0, 0))],
        out_specs=[pl.BlockSpec((tm, d), lambda i: (i, 0)),
                   pl.BlockSpec((tm, d), lambda i: (i, 0))],
        out_shape=[jax.ShapeDtypeStruct((t, d), F32), jax.ShapeDtypeStruct((t, d), BF16)],
        compiler_params=_params(("parallel",)),
        name="merge_out",
    )(o_att, o_rec, gl, gl, wb0, wb1, w_out, x, g_ffn.reshape(1, d).astype(F32))


def _oddeven_merge(lo, hi, r):
    step = r * 2
    if step < hi - lo:
        yield from _oddeven_merge(lo, hi, step)
        yield from _oddeven_merge(lo + r, hi, step)
        yield from [(i, i + r) for i in range(lo + r, hi - r, step)]
    else:
        yield (lo, lo + r)


def _oddeven_merge_sort(lo, hi):
    if hi - lo >= 1:
        mid = lo + (hi - lo) // 2
        yield from _oddeven_merge_sort(lo, mid)
        yield from _oddeven_merge_sort(mid + 1, hi)
        yield from _oddeven_merge(lo, hi, 1)


def _peer_scores_kernel(q_ref, keys_ref, thr_ref, e1_ref, s2_ref, e2_ref, s_ref, vals_ref, cand_ref):
    nhc = keys_ref.shape[0]
    k = PEER_TOPK
    for hc in range(nhc):
        qs = q_ref[:, hc * PEER_NKEYS:(hc + 1) * PEER_NKEYS]
        s_ref[hc] = lax.dot_general(keys_ref[hc], qs, _NT_DIMS, preferred_element_type=F32)

    n_lvl = PEER_NKEYS // 8
    network = list(_oddeven_merge_sort(0, n_lvl - 1))

    def top_values(hc, carry):
        x = s_ref[hc]
        lvl = [x[8 * v:8 * v + 8, :] for v in range(n_lvl)]
        for a, b in network:
            lvl[a], lvl[b] = jnp.maximum(lvl[a], lvl[b]), jnp.minimum(lvl[a], lvl[b])
        for r in range(k + 1):
            m = jnp.max(lvl[0], axis=0, keepdims=True)
            vals_ref[hc, r:r + 1, :] = m
            hit = lvl[0] == m
            for v in range(min(k - r, n_lvl)):
                below = lvl[v + 1] if v + 1 < n_lvl else LOWEST
                lvl[v] = jnp.where(hit, below, lvl[v])
        return carry

    lax.fori_loop(0, nhc, top_values, 0)

    def head_stats(h, carry):
        v1 = vals_ref[2 * h, 0:k + 1, :]
        v2 = vals_ref[2 * h + 1, 0:k + 1, :]
        cand_ref[...] = jnp.full(cand_ref.shape, LOWEST, F32)
        off = 0
        for a in range(k + 1):
            nb = (k + 1) // (a + 1)
            cand_ref[off:off + nb, :] = v1[a:a + 1, :] + v2[0:nb, :]
            off += nb
        cand = cand_ref[...]
        x = cand
        kth = None
        for r in range(k):
            kth = jnp.max(x, axis=0, keepdims=True)
            x = jnp.where(x == kth, LOWEST, x)
        nxt = jnp.max(x, axis=0, keepdims=True)
        m1 = v1[0:1, :]
        m2 = v2[0:1, :]
        tau = 0.5 * (kth + nxt)
        z = jnp.sum(jnp.where(cand >= tau, jnp.exp(cand - (m1 + m2)), 0.0), axis=0, keepdims=True)
        s1 = s_ref[2 * h]
        s2 = s_ref[2 * h + 1]
        thr_ref[h] = tau - s1
        e1_ref[h] = jnp.exp(s1 - m1)
        e2 = jnp.exp(s2 - m2) * (0.5 / z)
        for c in range(s2_ref.shape[1]):
            s2_ref[h, c] = s2[:, c * 128:(c + 1) * 128]
            e2_ref[h, c] = e2[:, c * 128:(c + 1) * 128]
        return carry

    lax.fori_loop(0, nhc // 2, head_stats, 0)


def _peer_scores(q, keys, tm=256):
    t, _ = q.shape
    nhc = keys.shape[0]
    out_spec = pl.BlockSpec((PEER_HEADS, PEER_NKEYS, tm), lambda i: (0, 0, i))
    out_shape = jax.ShapeDtypeStruct((PEER_HEADS, PEER_NKEYS, t), F32)
    chunk_spec = pl.BlockSpec((PEER_HEADS, tm // 128, PEER_NKEYS, 128), lambda i: (0, i, 0, 0))
    chunk_shape = jax.ShapeDtypeStruct((PEER_HEADS, t // 128, PEER_NKEYS, 128), F32)
    n_cand = sum((PEER_TOPK + 1) // (a + 1) for a in range(PEER_TOPK + 1))
    n_cand = -(-n_cand // 8) * 8
    return pl.pallas_call(
        _peer_scores_kernel,
        grid=(t // tm,),
        in_specs=[pl.BlockSpec((tm, nhc * PEER_NKEYS), lambda i: (i, 0)),
                  pl.BlockSpec(keys.shape, lambda i: (0, 0, 0))],
        out_specs=[out_spec, out_spec, chunk_spec, chunk_spec],
        out_shape=[out_shape, out_shape, chunk_shape, chunk_shape],
        scratch_shapes=[pltpu.VMEM((nhc, PEER_NKEYS, tm), F32),
                        pltpu.VMEM((nhc, PEER_TOPK + 8, tm), F32),
                        pltpu.VMEM((n_cand, tm), F32)],
        compiler_params=_params(("parallel",)),
        name="peer_scores",
    )(q, keys)


def _transpose_kernel(x_ref, o_ref):
    o_ref[...] = x_ref[...].T.astype(o_ref.dtype)


def _transposed_tiles(table, tn):
    e, d = table.shape
    return pl.pallas_call(
        _transpose_kernel,
        grid=(e // tn,),
        in_specs=[pl.BlockSpec((tn, d), lambda i: (i, 0))],
        out_specs=pl.BlockSpec((None, d, tn), lambda i: (i, 0, 0)),
        out_shape=jax.ShapeDtypeStruct((e // tn, d, tn), BF16),
        compiler_params=_params(("parallel",)),
        name="expert_value_tiles",
    )(table)


def _peer_dense_kernel(xn_ref, thr_ref, e1_ref, s2_ref, e2_ref, u_ref, vt_ref, acc_ref,
                       xnt_ref, act_ref, p_ref):
    j = pl.program_id(1)
    lane_chunks, tn, _ = act_ref.shape
    rows = tn // PEER_NKEYS

    @pl.when(j == 0)
    def _():
        acc_ref[...] = jnp.zeros(acc_ref.shape, F32)
        xnt_ref[...] = xn_ref[...].astype(F32).T.astype(xnt_ref.dtype)

    act = jnp.dot(u_ref[...], xnt_ref[...], preferred_element_type=F32)
    for lc in range(lane_chunks):
        act_ref[lc] = act[:, lc * 128:(lc + 1) * 128]

    def weigh(r, carry):
        i1 = j * rows + r
        roff = pl.multiple_of(r * PEER_NKEYS, PEER_NKEYS)
        thr_rows = [thr_ref[h, pl.ds(i1, 1), :] for h in range(PEER_HEADS)]
        e1_rows = [e1_ref[h, pl.ds(i1, 1), :] for h in range(PEER_HEADS)]
        for lc in range(lane_chunks):
            lanes = slice(lc * 128, (lc + 1) * 128)
            w = None
            for h in range(PEER_HEADS):
                w_h = jnp.where(s2_ref[h, lc] >= thr_rows[h][:, lanes],
                                e2_ref[h, lc] * e1_rows[h][:, lanes], 0.0)
                w = w_h if w is None else w + w_h
            a = act_ref[lc, pl.ds(roff, PEER_NKEYS), :]
            p_ref[pl.ds(roff, PEER_NKEYS), lanes] = (_gelu_tanh_x2(a) * w).astype(p_ref.dtype)
        return carry

    lax.fori_loop(0, rows, weigh, 0)
    res = jnp.dot(vt_ref[...], p_ref[...], preferred_element_type=F32)
    for lc in range(lane_chunks):
        acc_ref[lc] += res[:, lc * 128:(lc + 1) * 128]


def _peer_dense(xn, thr, e1, s2, e2, u_bf, vt_tiles, tm=512):
    t, d = xn.shape
    n_tiles, _, tn = vt_tiles.shape
    route_spec = pl.BlockSpec((PEER_HEADS, PEER_NKEYS, tm), lambda i, j: (0, 0, i))
    chunk_spec = pl.BlockSpec((PEER_HEADS, tm // 128, PEER_NKEYS, 128), lambda i, j: (0, i, 0, 0))
    return pl.pallas_call(
        _peer_dense_kernel,
        grid=(t // tm, n_tiles),
        in_specs=[pl.BlockSpec((tm, d), lambda i, j: (i, 0)),
                  route_spec, route_spec, chunk_spec, chunk_spec,
                  pl.BlockSpec((tn, d), lambda i, j: (j, 0)),
                  pl.BlockSpec((None, d, tn), lambda i, j: (j, 0, 0))],
        out_specs=pl.BlockSpec((tm // 128, d, 128), lambda i, j: (i, 0, 0)),
        out_shape=jax.ShapeDtypeStruct((t // 128, d, 128), F32),
        scratch_shapes=[pltpu.VMEM((d, tm), BF16), pltpu.VMEM((tm // 128, tn, 128), F32),
                        pltpu.VMEM((tn, tm), BF16)],
        compiler_params=_params(("parallel", "arbitrary")),
        name="peer_dense",
    )(xn, thr, e1, s2, e2, u_bf, vt_tiles)


def _final_kernel(x1_ref, pt_ref, g_ref, y_ref):
    for c in range(pt_ref.shape[0]):
        rows = slice(c * 128, (c + 1) * 128)
        y_ref[rows, :] = _rms(x1_ref[rows, :] + pt_ref[c].T, g_ref[...])


def _residual_norm(x1, peer_chunks, g, tm=256):
    t, d = x1.shape
    return pl.pallas_call(
        _final_kernel,
        grid=(t // tm,),
        in_specs=[pl.BlockSpec((tm, d), lambda i: (i, 0)),
                  pl.BlockSpec((tm // 128, d, 128), lambda i: (i, 0, 0)),
                  pl.BlockSpec((1, d), lambda i: (0, 0))],
        out_specs=pl.BlockSpec((tm, d), lambda i: (i, 0)),
        out_shape=jax.ShapeDtypeStruct((t, d), F32),
        compiler_params=_params(("parallel",)),
        name="residual_norm",
    )(x1, peer_chunks, g.reshape(1, d).astype(F32))


def kernel(x, norm_mix_g, w_in, conv_w, conv_b, lru_wa, lru_ba, lru_wx, lru_bx, lru_lambda,
           w_branch, w_out, rel_bias, norm_ffn_g, peer_wq, peer_keys, peer_u, peer_v, norm_final_g):
    batch, seq, d = x.shape
    t = batch * seq
    assert w_in.shape[0] == 1, "single-layer trunk: the final rmsnorm is fused into the PEER epilogue"
    xt = x.reshape(t, d)
    bias = _bias_tiles(rel_bias)
    a_w = ATTN_WIDTH
    w_l = w_in[0]
    h = _rmsnorm(xt, norm_mix_g[0], BF16)
    qk = _matmul(h, w_l[:, :2 * a_w].astype(BF16), BF16, "proj_qk")
    vt3 = _v_transposed(w_l[:, 2 * a_w:3 * a_w].T.astype(BF16), h)
    r = _matmul(h, w_l[:, 3 * a_w:3 * a_w + 2 * LRU_WIDTH].astype(BF16), F32, "proj_rec")
    gl = _matmul(h, w_l[:, 3 * a_w + 2 * LRU_WIDTH:].astype(BF16), F32, "proj_gate")
    o_att = _attention(qk, vt3, bias, batch, seq)
    o_rec = _rglru(r, conv_w[0], conv_b[0], lru_wa[0], lru_wx[0], lru_ba[0], lru_bx[0],
                   lru_lambda[0], batch, seq)
    x1, xn = _merge_out(o_att, o_rec, gl, w_branch[0, 0].astype(BF16),
                        w_branch[0, 1].astype(BF16), w_out[0].astype(BF16), xt, norm_ffn_g[0])
    q = _matmul(xn, peer_wq[0].astype(BF16), BF16, "peer_query")
    keys = peer_keys[0].reshape(PEER_HEADS * 2, PEER_NKEYS, PEER_DKEY // 2).astype(BF16)
    thr, e1, s2, e2 = _peer_scores(q, keys)
    vt_tiles = _transposed_tiles(peer_v[0], PEER_EXPERT_TILE)
    peer_out = _peer_dense(xn, thr, e1, s2, e2, peer_u[0].astype(BF16), vt_tiles)
    y = _residual_norm(x1, peer_out, norm_final_g)
    return y.reshape(batch, seq, d)
```

```python
import functools
import math

import jax
import jax.numpy as jnp
from jax import lax
from jax.experimental import pallas as pl
from jax.experimental.pallas import tpu as pltpu

D_MODEL = 2048
ATTN_HEADS = 8
HEAD_DIM = 128
ATTN_WIDTH = ATTN_HEADS * HEAD_DIM
MOBA_BLOCK = 256
MOBA_TOPK = 3
LRU_WIDTH = 1024
LRU_BLOCKS = 8
LRU_BLOCK_DIM = LRU_WIDTH // LRU_BLOCKS
CONV_WIDTH = 4
LRU_C = 8.0
REL_BUCKETS = 32
REL_MAX_DIST = 128
PEER_HEADS = 8
PEER_NKEYS = 128
PEER_EXPERTS = PEER_NKEYS * PEER_NKEYS
PEER_DKEY = 256
PEER_TOPK = 16
EPS = 1e-6
NEG = -1e30
LOWEST = -3.0e38
PEER_EXPERT_TILE = 1024

V7X_VMEM_LIMIT_BYTES = 56 * 1024 * 1024

F32 = jnp.float32
BF16 = jnp.bfloat16

_NT_DIMS = (((1,), (1,)), ((), ()))


def _params(semantics, flags=None):
    return pltpu.CompilerParams(dimension_semantics=semantics,
                                vmem_limit_bytes=V7X_VMEM_LIMIT_BYTES, flags=flags)


def _gelu_tanh_x2(x):
    c = math.sqrt(2.0 / math.pi)
    return x * (1.0 + jnp.tanh(x * (c + (0.044715 * c) * (x * x))))


def _gelu_tanh(x):
    return 0.5 * _gelu_tanh_x2(x)


def _rms(x, g):
    ms = jnp.mean(x * x, axis=-1, keepdims=True)
    return x * lax.rsqrt(ms + EPS) * g


def _rmsnorm_kernel(x_ref, g_ref, o_ref):
    o_ref[...] = _rms(x_ref[...], g_ref[...]).astype(o_ref.dtype)


def _rmsnorm(x, g, out_dtype, tm=512):
    t, d = x.shape
    return pl.pallas_call(
        _rmsnorm_kernel,
        grid=(t // tm,),
        in_specs=[pl.BlockSpec((tm, d), lambda i: (i, 0)),
                  pl.BlockSpec((1, d), lambda i: (0, 0))],
        out_specs=pl.BlockSpec((tm, d), lambda i: (i, 0)),
        out_shape=jax.ShapeDtypeStruct((t, d), out_dtype),
        compiler_params=_params(("parallel",)),
        name="rmsnorm",
    )(x, g.reshape(1, d))


def _mm_kernel(a_ref, b_ref, o_ref):
    o_ref[...] = jnp.dot(a_ref[...], b_ref[...].astype(BF16),
                         preferred_element_type=F32).astype(o_ref.dtype)


def _matmul(a, b, col_start, n, out_dtype, name, tm=1024, tn=1024):
    m, k = a.shape
    assert col_start % tn == 0 and n % tn == 0
    col_blk = col_start // tn
    return pl.pallas_call(
        _mm_kernel,
        grid=(m // tm, n // tn),
        in_specs=[pl.BlockSpec((tm, k), lambda i, j: (i, 0)),
                  pl.BlockSpec((k, tn), lambda i, j: (0, col_blk + j))],
        out_specs=pl.BlockSpec((tm, tn), lambda i, j: (i, j)),
        out_shape=jax.ShapeDtypeStruct((m, n), out_dtype),
        compiler_params=_params(("parallel", "arbitrary")),
        name=name,
    )(a, b)


def _vt_kernel(w_ref, h_ref, o_ref):
    res = lax.dot_general(w_ref[...], h_ref[...], _NT_DIMS, preferred_element_type=F32)
    for t in range(o_ref.shape[0]):
        o_ref[t] = res[:, t * MOBA_BLOCK:(t + 1) * MOBA_BLOCK].astype(o_ref.dtype)


def _v_transposed(w_t, h, tm=1024):
    c, k = w_t.shape
    t, _ = h.shape
    nb = tm // MOBA_BLOCK
    return pl.pallas_call(
        _vt_kernel,
        grid=(t // tm,),
        in_specs=[pl.BlockSpec((c, k), lambda i: (0, 0)),
                  pl.BlockSpec((tm, k), lambda i: (i, 0))],
        out_specs=pl.BlockSpec((nb, c, MOBA_BLOCK), lambda i: (i, 0, 0)),
        out_shape=jax.ShapeDtypeStruct((t // MOBA_BLOCK, c, MOBA_BLOCK), BF16),
        compiler_params=_params(("parallel",)),
        name="v_transposed",
    )(w_t, h)


def _rel_bucket(dist):
    n = jnp.maximum(dist, 0)
    max_exact = REL_BUCKETS // 2
    nf = jnp.maximum(n, 1).astype(F32)
    large = max_exact + (jnp.log(nf / max_exact) / math.log(REL_MAX_DIST / max_exact)
                         * (REL_BUCKETS - max_exact)).astype(jnp.int32)
    large = jnp.minimum(large, REL_BUCKETS - 1)
    return jnp.where(n < max_exact, n, large)


def _bias_kernel(relb_ref, bucket_ref, o_ref):
    h = pl.program_id(0)
    for t in range(3):
        bk = bucket_ref[t]
        acc = jnp.zeros(bk.shape, F32)
        for b in range(REL_BUCKETS):
            acc = jnp.where(bk == b, relb_ref[b, h], acc)
        if t == 0:
            key = lax.broadcasted_iota(jnp.int32, bk.shape, 0)
            qry = lax.broadcasted_iota(jnp.int32, bk.shape, 1)
            acc = jnp.where(key <= qry, acc, NEG)
        o_ref[0, t] = acc


def _bias_tiles(rel_bias):
    assert REL_MAX_DIST <= MOBA_BLOCK
    key = jnp.arange(MOBA_BLOCK, dtype=jnp.int32)[:, None]
    qry = jnp.arange(MOBA_BLOCK, dtype=jnp.int32)[None, :]
    buckets = jnp.stack([_rel_bucket(t * MOBA_BLOCK + qry - key) for t in range(3)])
    return pl.pallas_call(
        _bias_kernel,
        grid=(ATTN_HEADS,),
        in_specs=[pl.BlockSpec(memory_space=pltpu.SMEM),
                  pl.BlockSpec((3, MOBA_BLOCK, MOBA_BLOCK), lambda h: (0, 0, 0))],
        out_specs=pl.BlockSpec((1, 3, MOBA_BLOCK, MOBA_BLOCK), lambda h: (h, 0, 0, 0)),
        out_shape=jax.ShapeDtypeStruct((ATTN_HEADS, 3, MOBA_BLOCK, MOBA_BLOCK), F32),
        compiler_params=_params(("parallel",)),
        name="attn_bias_tiles",
    )(rel_bias.astype(F32), buckets)


ATTN_HEADS_PER_STEP = 8


def _attn_kernel(q_ref, k_ref, vt_ref, bias_ref, o_ref, kmean_ref, sel_ref, acc_ref, *, nb):
    i = pl.program_id(2)
    scale = HEAD_DIM ** -0.5
    heads = range(ATTN_HEADS_PER_STEP)
    hsl = [slice(g * HEAD_DIM, (g + 1) * HEAD_DIM) for g in heads]

    @pl.when(i == 0)
    def _():
        for g in heads:
            for n in range(nb):
                kb = k_ref[n * MOBA_BLOCK:(n + 1) * MOBA_BLOCK, hsl[g]].astype(F32)
                kmean_ref[g, n:n + 1, :] = jnp.mean(kb, axis=0, keepdims=True)

    qs = [q_ref[:, hsl[g]] for g in heads]
    for g in heads:
        gate = lax.dot_general(kmean_ref[g].astype(BF16), qs[g], _NT_DIMS,
                               preferred_element_type=F32)
        blk = lax.broadcasted_iota(jnp.int32, gate.shape, 0)
        rank = jnp.zeros(gate.shape, F32)
        for m in range(nb):
            gm = gate[m:m + 1, :]
            beats = jnp.where(gm > gate, 1.0, jnp.where(gm == gate, jnp.where(m < blk, 1.0, 0.0), 0.0))
            rank = rank + beats * jnp.where(m < i, 1.0, 0.0)
        sel_ref[g] = jnp.where(blk < i, jnp.where(rank < MOBA_TOPK, 1.0, 0.0), 0.0)

    def scores(g, j, tile):
        koff = pl.multiple_of(j * MOBA_BLOCK, MOBA_BLOCK)
        kj = k_ref[pl.ds(koff, MOBA_BLOCK), hsl[g]]
        s = lax.dot_general(kj, qs[g], _NT_DIMS, preferred_element_type=F32)
        return s * scale + bias_ref[g, tile]

    s0 = [scores(g, i, 0) for g in heads]
    m0 = [jnp.max(s0[g], axis=0, keepdims=True) for g in heads]
    p0 = [jnp.exp(s0[g] - m0[g]) for g in heads]
    l0 = [jnp.sum(p0[g], axis=0, keepdims=True) for g in heads]
    for g in heads:
        acc_ref[g] = jnp.dot(vt_ref[i, hsl[g], :], p0[g].astype(BF16), preferred_element_type=F32)

    def body(j, carry):
        ms, ls = carry
        tile = jnp.minimum(i - j, 2)
        ss = [scores(g, j, tile) for g in heads]
        ss = [jnp.where(sel_ref[g, pl.ds(j, 1), :] > 0.5, ss[g], NEG) for g in heads]
        new_m = [jnp.maximum(ms[g], jnp.max(ss[g], axis=0, keepdims=True)) for g in heads]
        alpha = [jnp.exp(ms[g] - new_m[g]) for g in heads]
        ps = [jnp.exp(ss[g] - new_m[g]) for g in heads]
        new_l = [alpha[g] * ls[g] + jnp.sum(ps[g], axis=0, keepdims=True) for g in heads]
        pv = [jnp.dot(vt_ref[j, hsl[g], :], ps[g].astype(BF16), preferred_element_type=F32) for g in heads]
        for g in heads:
            acc_ref[g] = alpha[g] * acc_ref[g] + pv[g]
        return tuple(new_m), tuple(new_l)

    _, l_f = lax.fori_loop(0, i, body, (tuple(m0), tuple(l0)))
    for g in heads:
        o_ref[:, hsl[g]] = (acc_ref[g] / l_f[g]).T.astype(o_ref.dtype)


def _attention(qk, vt3, bias, batch, seq):
    nb = seq // MOBA_BLOCK
    t = batch * seq
    gh = ATTN_HEADS_PER_STEP
    width = gh * HEAD_DIM
    groups = ATTN_HEADS // gh
    return pl.pallas_call(
        functools.partial(_attn_kernel, nb=nb),
        grid=(batch, groups, nb),
        in_specs=[
            pl.BlockSpec((MOBA_BLOCK, width), lambda b, h, i: (b * nb + i, h)),
            pl.BlockSpec((seq, width), lambda b, h, i: (b, groups + h)),
            pl.BlockSpec((nb, width, MOBA_BLOCK), lambda b, h, i: (b, h, 0)),
            pl.BlockSpec((gh, 3, MOBA_BLOCK, MOBA_BLOCK), lambda b, h, i: (h, 0, 0, 0)),
        ],
        out_specs=pl.BlockSpec((MOBA_BLOCK, width), lambda b, h, i: (b * nb + i, h)),
        out_shape=jax.ShapeDtypeStruct((t, ATTN_WIDTH), BF16),
        scratch_shapes=[pltpu.VMEM((gh, nb, HEAD_DIM), F32),
                        pltpu.VMEM((gh, nb, MOBA_BLOCK), F32),
                        pltpu.VMEM((gh, HEAD_DIM, MOBA_BLOCK), F32)],
        compiler_params=_params(("parallel", "parallel", "arbitrary")),
        name="moba_attention",
    )(qk, qk, vt3, bias)


def _rglru_kernel(xr_ref, yr_ref, cw_ref, cb_ref, wa_ref, wx_ref, ba_ref, bx_ref, lam_ref,
                  o_ref, tail_ref, h_ref):
    c = pl.program_id(1)
    tc = xr_ref.shape[0]

    @pl.when(c == 0)
    def _():
        tail_ref[...] = jnp.zeros(tail_ref.shape, F32)
        h_ref[...] = jnp.zeros(h_ref.shape, F32)

    x = xr_ref[...]
    grp_shape = (tc // 8, 8, x.shape[1])
    x3 = x.reshape(grp_shape)
    tail3 = tail_ref[...].reshape(1, 8, x.shape[1])
    sub = lax.broadcasted_iota(jnp.int32, grp_shape, 1)
    conv = cb_ref[...] + cw_ref[CONV_WIDTH - 1:CONV_WIDTH, :] * x3
    for s in range(1, CONV_WIDTH):
        k = CONV_WIDTH - 1 - s
        rolled = pltpu.roll(x3, s, axis=1)
        rolled_prev = jnp.concatenate([pltpu.roll(tail3, s, axis=1), rolled[:-1]], axis=0)
        conv = conv + cw_ref[k:k + 1, :] * jnp.where(sub >= s, rolled, rolled_prev)
    conv = conv.reshape(x.shape)
    tail_ref[...] = x[tc - 8:, :]

    xb16 = conv.astype(BF16)
    pre_a, pre_x = [], []
    for g in range(LRU_BLOCKS):
        xs = xb16[:, g * LRU_BLOCK_DIM:(g + 1) * LRU_BLOCK_DIM]
        pre_a.append(jnp.dot(xs, wa_ref[g], preferred_element_type=F32))
        pre_x.append(jnp.dot(xs, wx_ref[g], preferred_element_type=F32))
    r_gate = jax.nn.sigmoid(jnp.concatenate(pre_a, axis=1) + ba_ref[...])
    i_gate = jax.nn.sigmoid(jnp.concatenate(pre_x, axis=1) + bx_ref[...])
    z = -lam_ref[...]
    softplus = jnp.maximum(z, 0.0) + jnp.log1p(jnp.exp(-jnp.abs(z)))
    log_a = (-LRU_C) * r_gate * softplus
    a = jnp.exp(log_a)
    th = jnp.tanh(log_a)
    y2 = -2.0 * th / (1.0 - th)
    u = jnp.where(y2 > 0.0, y2 * lax.rsqrt(y2), 0.0) * (i_gate * conv)

    grp_shape = (tc // 8, 8, a.shape[1])
    a = a.reshape(grp_shape)
    u = u.reshape(grp_shape)
    sub = lax.broadcasted_iota(jnp.int32, grp_shape, 1)
    s = 1
    while s < 8:
        a_sh = pltpu.roll(a, s, axis=1)
        u_sh = pltpu.roll(u, s, axis=1)
        valid = sub >= s
        u = jnp.where(valid, a * u_sh + u, u)
        a = jnp.where(valid, a * a_sh, a)
        s *= 2
    h = h_ref[...]
    gy = _gelu_tanh(yr_ref[...])
    for grp in range(tc // 8):
        rows = slice(grp * 8, (grp + 1) * 8)
        hh = a[grp] * h + u[grp]
        h = hh[7:8, :]
        o_ref[rows, :] = (hh * gy[rows, :]).astype(o_ref.dtype)
    h_ref[...] = h


def _rglru(r, conv_w, conv_b, wa, wx, ba, bx, lam, batch, seq, tc=256):
    t = batch * seq
    nc = seq // tc
    w = LRU_WIDTH
    row = lambda v: v.reshape(1, w).astype(F32)
    full2 = lambda shape: pl.BlockSpec(shape, lambda b, c: (0,) * len(shape))
    return pl.pallas_call(
        _rglru_kernel,
        grid=(batch, nc),
        in_specs=[pl.BlockSpec((tc, w), lambda b, c: (b * nc + c, 0)),
                  pl.BlockSpec((tc, w), lambda b, c: (b * nc + c, 1)),
                  full2((CONV_WIDTH, w)), full2((1, w)),
                  full2((LRU_BLOCKS, LRU_BLOCK_DIM, LRU_BLOCK_DIM)),
                  full2((LRU_BLOCKS, LRU_BLOCK_DIM, LRU_BLOCK_DIM)),
                  full2((1, w)), full2((1, w)), full2((1, w))],
        out_specs=pl.BlockSpec((tc, w), lambda b, c: (b * nc + c, 0)),
        out_shape=jax.ShapeDtypeStruct((t, w), BF16),
        scratch_shapes=[pltpu.VMEM((8, w), F32), pltpu.VMEM((1, w), F32)],
        compiler_params=_params(("parallel", "arbitrary")),
        name="rglru",
    )(r, r, conv_w.astype(F32), row(conv_b), wa.astype(BF16), wx.astype(BF16),
      row(ba), row(bx), row(lam))


def _merge_kernel(oa_ref, or_ref, g0_ref, g1_ref, wb0_ref, wb1_ref, wo_ref, x_ref, gn_ref,
                  x1_ref, xn_ref):
    pb0 = jnp.dot(oa_ref[...], wb0_ref[...], preferred_element_type=F32)
    pb1 = jnp.dot(or_ref[...], wb1_ref[...], preferred_element_type=F32)
    merged = jax.nn.sigmoid(g0_ref[...]) * pb0 + jax.nn.sigmoid(g1_ref[...]) * pb1
    x1 = x_ref[...] + jnp.dot(merged.astype(BF16), wo_ref[...], preferred_element_type=F32)
    x1_ref[...] = x1
    xn_ref[...] = _rms(x1, gn_ref[...]).astype(xn_ref.dtype)


def _merge_out(o_att, o_rec, gl, wb0, wb1, w_out, x, g_ffn, tm=256):
    t, d = x.shape
    cw = o_att.shape[1]
    resident = lambda shape: pl.BlockSpec(shape, lambda i: (0, 0), pipeline_mode=pl.Buffered(1))
    return pl.pallas_call(
        _merge_kernel,
        grid=(t // tm,),
        in_specs=[pl.BlockSpec((tm, cw), lambda i: (i, 0)),
                  pl.BlockSpec((tm, cw), lambda i: (i, 0)),
                  pl.BlockSpec((tm, d), lambda i: (i, 0)),
                  pl.BlockSpec((tm, d), lambda i: (i, 1)),
                  resident((cw, d)), resident((cw, d)), resident((d, d)),
                  pl.BlockSpec((tm, d), lambda i: (i, 0)),
                  pl.BlockSpec((1, d), lambda i: (0, 0))],
        out_specs=[pl.BlockSpec((tm, d), lambda i: (i, 0)),
                   pl.BlockSpec((tm, d), lambda i: (i, 0))],
        out_shape=[jax.ShapeDtypeStruct((t, d), F32), jax.ShapeDtypeStruct((t, d), BF16)],
        compiler_params=_params(("parallel",)),
        name="merge_out",
    )(o_att, o_rec, gl, gl, wb0, wb1, w_out, x, g_ffn.reshape(1, d).astype(F32))


def _oddeven_merge(lo, hi, r):
    step = r * 2
    if step < hi - lo:
        yield from _oddeven_merge(lo, hi, step)
        yield from _oddeven_merge(lo + r, hi, step)
        yield from [(i, i + r) for i in range(lo + r, hi - r, step)]
    else:
        yield (lo, lo + r)


def _oddeven_merge_sort(lo, hi):
    if hi - lo >= 1:
        mid = lo + (hi - lo) // 2
        yield from _oddeven_merge_sort(lo, mid)
        yield from _oddeven_merge_sort(mid + 1, hi)
        yield from _oddeven_merge(lo, hi, 1)


def _peer_scores_kernel(q_ref, keys_ref, thr_ref, e1_ref, s2_ref, e2_ref, s_ref, vals_ref, cand_ref):
    nhc = keys_ref.shape[0]
    k = PEER_TOPK
    for hc in range(nhc):
        qs = q_ref[:, hc * PEER_NKEYS:(hc + 1) * PEER_NKEYS]
        s_ref[hc] = lax.dot_general(keys_ref[hc], qs, _NT_DIMS, preferred_element_type=F32)

    n_lvl = PEER_NKEYS // 8
    network = list(_oddeven_merge_sort(0, n_lvl - 1))

    def top_values(hc, carry):
        x = s_ref[hc]
        lvl = [x[8 * v:8 * v + 8, :] for v in range(n_lvl)]
        for a, b in network:
            lvl[a], lvl[b] = jnp.maximum(lvl[a], lvl[b]), jnp.minimum(lvl[a], lvl[b])
        for r in range(k + 1):
            m = jnp.max(lvl[0], axis=0, keepdims=True)
            vals_ref[hc, r:r + 1, :] = m
            hit = lvl[0] == m
            for v in range(min(k - r, n_lvl)):
                below = lvl[v + 1] if v + 1 < n_lvl else LOWEST
                lvl[v] = jnp.where(hit, below, lvl[v])
        return carry

    lax.fori_loop(0, nhc, top_values, 0)

    def head_stats(h, carry):
        v1 = vals_ref[2 * h, 0:k + 1, :]
        v2 = vals_ref[2 * h + 1, 0:k + 1, :]
        cand_ref[...] = jnp.full(cand_ref.shape, LOWEST, F32)
        off = 0
        for a in range(k + 1):
            nb = (k + 1) // (a + 1)
            cand_ref[off:off + nb, :] = v1[a:a + 1, :] + v2[0:nb, :]
            off += nb
        cand = cand_ref[...]
        x = cand
        kth = None
        for r in range(k):
            kth = jnp.max(x, axis=0, keepdims=True)
            x = jnp.where(x == kth, LOWEST, x)
        nxt = jnp.max(x, axis=0, keepdims=True)
        m1 = v1[0:1, :]
        m2 = v2[0:1, :]
        tau = 0.5 * (kth + nxt)
        z = jnp.sum(jnp.where(cand >= tau, jnp.exp(cand - (m1 + m2)), 0.0), axis=0, keepdims=True)
        s1 = s_ref[2 * h]
        s2 = s_ref[2 * h + 1]
        thr_ref[h] = tau - s1
        e1_ref[h] = jnp.exp(s1 - m1)
        e2 = jnp.exp(s2 - m2) * (0.5 / z)
        for c in range(s2_ref.shape[1]):
            s2_ref[h, c] = s2[:, c * 128:(c + 1) * 128]
            e2_ref[h, c] = e2[:, c * 128:(c + 1) * 128]
        return carry

    lax.fori_loop(0, nhc // 2, head_stats, 0)


def _peer_scores(q, keys, tm=256):
    t, _ = q.shape
    nhc = keys.shape[0]
    out_spec = pl.BlockSpec((PEER_HEADS, PEER_NKEYS, tm), lambda i: (0, 0, i))
    out_shape = jax.ShapeDtypeStruct((PEER_HEADS, PEER_NKEYS, t), F32)
    chunk_spec = pl.BlockSpec((PEER_HEADS, tm // 128, PEER_NKEYS, 128), lambda i: (0, i, 0, 0))
    chunk_shape = jax.ShapeDtypeStruct((PEER_HEADS, t // 128, PEER_NKEYS, 128), F32)
    n_cand = sum((PEER_TOPK + 1) // (a + 1) for a in range(PEER_TOPK + 1))
    n_cand = -(-n_cand // 8) * 8
    return pl.pallas_call(
        _peer_scores_kernel,
        grid=(t // tm,),
        in_specs=[pl.BlockSpec((tm, nhc * PEER_NKEYS), lambda i: (i, 0)),
                  pl.BlockSpec(keys.shape, lambda i: (0, 0, 0))],
        out_specs=[out_spec, out_spec, chunk_spec, chunk_spec],
        out_shape=[out_shape, out_shape, chunk_shape, chunk_shape],
        scratch_shapes=[pltpu.VMEM((nhc, PEER_NKEYS, tm), F32),
                        pltpu.VMEM((nhc, PEER_TOPK + 8, tm), F32),
                        pltpu.VMEM((n_cand, tm), F32)],
        compiler_params=_params(("parallel",)),
        name="peer_scores",
    )(q, keys)


def _transpose_kernel(x_ref, o_ref):
    o_ref[...] = x_ref[...].T.astype(o_ref.dtype)


def _transposed_tiles(table, tn):
    e, d = table.shape
    return pl.pallas_call(
        _transpose_kernel,
        grid=(e // tn,),
        in_specs=[pl.BlockSpec((tn, d), lambda i: (i, 0))],
        out_specs=pl.BlockSpec((None, d, tn), lambda i: (i, 0, 0)),
        out_shape=jax.ShapeDtypeStruct((e // tn, d, tn), BF16),
        compiler_params=_params(("parallel",)),
        name="expert_value_tiles",
    )(table)


def _peer_dense_kernel(xn_ref, thr_ref, e1_ref, s2_ref, e2_ref, u_ref, vt_ref, acc_ref,
                       xnt_ref, act_ref, p_ref):
    j = pl.program_id(1)
    lane_chunks, tn, _ = act_ref.shape
    rows = tn // PEER_NKEYS

    @pl.when(j == 0)
    def _():
        acc_ref[...] = jnp.zeros(acc_ref.shape, F32)
        xnt_ref[...] = xn_ref[...].astype(F32).T.astype(xnt_ref.dtype)

    act = jnp.dot(u_ref[...], xnt_ref[...], preferred_element_type=F32)
    for lc in range(lane_chunks):
        act_ref[lc] = act[:, lc * 128:(lc + 1) * 128]

    def weigh(r, carry):
        i1 = j * rows + r
        roff = pl.multiple_of(r * PEER_NKEYS, PEER_NKEYS)
        thr_rows = [thr_ref[h, pl.ds(i1, 1), :] for h in range(PEER_HEADS)]
        e1_rows = [e1_ref[h, pl.ds(i1, 1), :] for h in range(PEER_HEADS)]
        for lc in range(lane_chunks):
            lanes = slice(lc * 128, (lc + 1) * 128)
            w = None
            for h in range(PEER_HEADS):
                w_h = jnp.where(s2_ref[h, lc] >= thr_rows[h][:, lanes],
                                e2_ref[h, lc] * e1_rows[h][:, lanes], 0.0)
                w = w_h if w is None else w + w_h
            a = act_ref[lc, pl.ds(roff, PEER_NKEYS), :]
            p_ref[pl.ds(roff, PEER_NKEYS), lanes] = (_gelu_tanh_x2(a) * w).astype(p_ref.dtype)
        return carry

    lax.fori_loop(0, rows, weigh, 0)
    res = jnp.dot(vt_ref[...], p_ref[...], preferred_element_type=F32)
    for lc in range(lane_chunks):
        acc_ref[lc] += res[:, lc * 128:(lc + 1) * 128]


def _peer_dense(xn, thr, e1, s2, e2, u_bf, vt_tiles, tm=512):
    t, d = xn.shape
    n_tiles, _, tn = vt_tiles.shape
    route_spec = pl.BlockSpec((PEER_HEADS, PEER_NKEYS, tm), lambda i, j: (0, 0, i))
    chunk_spec = pl.BlockSpec((PEER_HEADS, tm // 128, PEER_NKEYS, 128), lambda i, j: (0, i, 0, 0))
    return pl.pallas_call(
        _peer_dense_kernel,
        grid=(t // tm, n_tiles),
        in_specs=[pl.BlockSpec((tm, d), lambda i, j: (i, 0)),
                  route_spec, route_spec, chunk_spec, chunk_spec,
                  pl.BlockSpec((tn, d), lambda i, j: (j, 0)),
                  pl.BlockSpec((None, d, tn), lambda i, j: (j, 0, 0))],
        out_specs=pl.BlockSpec((tm // 128, d, 128), lambda i, j: (i, 0, 0)),
        out_shape=jax.ShapeDtypeStruct((t // 128, d, 128), F32),
        scratch_shapes=[pltpu.VMEM((d, tm), BF16), pltpu.VMEM((tm // 128, tn, 128), F32),
                        pltpu.VMEM((tn, tm), BF16)],
        compiler_params=_params(("parallel", "arbitrary")),
        name="peer_dense",
    )(xn, thr, e1, s2, e2, u_bf, vt_tiles)


def _final_kernel(x1_ref, pt_ref, g_ref, y_ref):
    for c in range(pt_ref.shape[0]):
        rows = slice(c * 128, (c + 1) * 128)
        y_ref[rows, :] = _rms(x1_ref[rows, :] + pt_ref[c].T, g_ref[...])


def _residual_norm(x1, peer_chunks, g, tm=256):
    t, d = x1.shape
    return pl.pallas_call(
        _final_kernel,
        grid=(t // tm,),
        in_specs=[pl.BlockSpec((tm, d), lambda i: (i, 0)),
                  pl.BlockSpec((tm // 128, d, 128), lambda i: (i, 0, 0)),
                  pl.BlockSpec((1, d), lambda i: (0, 0))],
        out_specs=pl.BlockSpec((tm, d), lambda i: (i, 0)),
        out_shape=jax.ShapeDtypeStruct((t, d), F32),
        compiler_params=_params(("parallel",)),
        name="residual_norm",
    )(x1, peer_chunks, g.reshape(1, d).astype(F32))


def kernel(x, norm_mix_g, w_in, conv_w, conv_b, lru_wa, lru_ba, lru_wx, lru_bx, lru_lambda,
           w_branch, w_out, rel_bias, norm_ffn_g, peer_wq, peer_keys, peer_u, peer_v, norm_final_g):
    batch, seq, d = x.shape
    t = batch * seq
    assert w_in.shape[0] == 1, "single-layer trunk: the final rmsnorm is fused into the PEER epilogue"
    xt = x.reshape(t, d)
    bias = _bias_tiles(rel_bias)
    a_w = ATTN_WIDTH
    w_l = w_in[0]
    h = _rmsnorm(xt, norm_mix_g[0], BF16)
    qk = _matmul(h, w_l, 0, 2 * a_w, BF16, "proj_qk")
    vt3 = _v_transposed(w_l[:, 2 * a_w:3 * a_w].T.astype(BF16), h)
    r = _matmul(h, w_l, 3 * a_w, 2 * LRU_WIDTH, F32, "proj_rec")
    gl = _matmul(h, w_l, 3 * a_w + 2 * LRU_WIDTH, 2 * d, F32, "proj_gate")
    o_att = _attention(qk, vt3, bias, batch, seq)
    o_rec = _rglru(r, conv_w[0], conv_b[0], lru_wa[0], lru_wx[0], lru_ba[0], lru_bx[0],
                   lru_lambda[0], batch, seq)
    x1, xn = _merge_out(o_att, o_rec, gl, w_branch[0, 0].astype(BF16),
                        w_branch[0, 1].astype(BF16), w_out[0].astype(BF16), xt, norm_ffn_g[0])
    q = _matmul(xn, peer_wq[0], 0, peer_wq.shape[2], BF16, "peer_query")
    keys = peer_keys[0].reshape(PEER_HEADS * 2, PEER_NKEYS, PEER_DKEY // 2).astype(BF16)
    thr, e1, s2, e2 = _peer_scores(q, keys)
    vt_tiles = _transposed_tiles(peer_v[0], PEER_EXPERT_TILE)
    peer_out = _peer_dense(xn, thr, e1, s2, e2, peer_u[0].astype(BF16), vt_tiles)
    y = _residual_norm(x1, peer_out, norm_final_g)
    return y.reshape(batch, seq, d)
```

```python
import functools
import math

import jax
import jax.numpy as jnp
from jax import lax
from jax.experimental import pallas as pl
from jax.experimental.pallas import tpu as pltpu

D_MODEL = 2048
ATTN_HEADS = 8
HEAD_DIM = 128
ATTN_WIDTH = ATTN_HEADS * HEAD_DIM
MOBA_BLOCK = 256
MOBA_TOPK = 3
LRU_WIDTH = 1024
LRU_BLOCKS = 8
LRU_BLOCK_DIM = LRU_WIDTH // LRU_BLOCKS
CONV_WIDTH = 4
LRU_C = 8.0
REL_BUCKETS = 32
REL_MAX_DIST = 128
PEER_HEADS = 8
PEER_NKEYS = 128
PEER_EXPERTS = PEER_NKEYS * PEER_NKEYS
PEER_DKEY = 256
PEER_TOPK = 16
EPS = 1e-6
NEG = -1e30
LOWEST = -3.0e38
PEER_EXPERT_TILE = 1024

V7X_VMEM_LIMIT_BYTES = 56 * 1024 * 1024

F32 = jnp.float32
BF16 = jnp.bfloat16

_NT_DIMS = (((1,), (1,)), ((), ()))


def _params(semantics, flags=None):
    return pltpu.CompilerParams(dimension_semantics=semantics,
                                vmem_limit_bytes=V7X_VMEM_LIMIT_BYTES, flags=flags)


def _gelu_tanh_x2(x):
    c = math.sqrt(2.0 / math.pi)
    return x * (1.0 + jnp.tanh(x * (c + (0.044715 * c) * (x * x))))


def _gelu_tanh(x):
    return 0.5 * _gelu_tanh_x2(x)


def _rms(x, g):
    ms = jnp.mean(x * x, axis=-1, keepdims=True)
    return x * lax.rsqrt(ms + EPS) * g


def _rmsnorm_kernel(x_ref, g_ref, o_ref):
    o_ref[...] = _rms(x_ref[...], g_ref[...]).astype(o_ref.dtype)


def _rmsnorm(x, g, out_dtype, tm=512):
    t, d = x.shape
    return pl.pallas_call(
        _rmsnorm_kernel,
        grid=(t // tm,),
        in_specs=[pl.BlockSpec((tm, d), lambda i: (i, 0)),
                  pl.BlockSpec((1, d), lambda i: (0, 0))],
        out_specs=pl.BlockSpec((tm, d), lambda i: (i, 0)),
        out_shape=jax.ShapeDtypeStruct((t, d), out_dtype),
        compiler_params=_params(("parallel",)),
        name="rmsnorm",
    )(x, g.reshape(1, d))


def _mm_kernel(a_ref, b_ref, o_ref, w_ref):
    @pl.when(pl.program_id(1) == 0)
    def _():
        w_ref[...] = b_ref[...].astype(w_ref.dtype)

    o_ref[...] = jnp.dot(a_ref[...], w_ref[...], preferred_element_type=F32).astype(o_ref.dtype)


def _matmul(a, b, col_start, n, out_dtype, name, tm=1024, tn=1024):
    m, k = a.shape
    assert col_start % tn == 0 and n % tn == 0
    col_blk = col_start // tn
    return pl.pallas_call(
        _mm_kernel,
        grid=(n // tn, m // tm),
        in_specs=[pl.BlockSpec((tm, k), lambda j, i: (i, 0)),
                  pl.BlockSpec((k, tn), lambda j, i: (0, col_blk + j))],
        out_specs=pl.BlockSpec((tm, tn), lambda j, i: (i, j)),
        out_shape=jax.ShapeDtypeStruct((m, n), out_dtype),
        scratch_shapes=[pltpu.VMEM((k, tn), BF16)],
        compiler_params=_params(("parallel", "arbitrary")),
        name=name,
    )(a, b)


def _vt_kernel(w_ref, h_ref, o_ref):
    res = lax.dot_general(w_ref[...], h_ref[...], _NT_DIMS, preferred_element_type=F32)
    for t in range(o_ref.shape[0]):
        o_ref[t] = res[:, t * MOBA_BLOCK:(t + 1) * MOBA_BLOCK].astype(o_ref.dtype)


def _v_transposed(w_t, h, tm=1024):
    c, k = w_t.shape
    t, _ = h.shape
    nb = tm // MOBA_BLOCK
    return pl.pallas_call(
        _vt_kernel,
        grid=(t // tm,),
        in_specs=[pl.BlockSpec((c, k), lambda i: (0, 0)),
                  pl.BlockSpec((tm, k), lambda i: (i, 0))],
        out_specs=pl.BlockSpec((nb, c, MOBA_BLOCK), lambda i: (i, 0, 0)),
        out_shape=jax.ShapeDtypeStruct((t // MOBA_BLOCK, c, MOBA_BLOCK), BF16),
        compiler_params=_params(("parallel",)),
        name="v_transposed",
    )(w_t, h)


def _rel_bucket(dist):
    n = jnp.maximum(dist, 0)
    max_exact = REL_BUCKETS // 2
    nf = jnp.maximum(n, 1).astype(F32)
    large = max_exact + (jnp.log(nf / max_exact) / math.log(REL_MAX_DIST / max_exact)
                         * (REL_BUCKETS - max_exact)).astype(jnp.int32)
    large = jnp.minimum(large, REL_BUCKETS - 1)
    return jnp.where(n < max_exact, n, large)


def _bias_kernel(relb_ref, bucket_ref, o_ref):
    h = pl.program_id(0)
    for t in range(3):
        bk = bucket_ref[t]
        acc = jnp.zeros(bk.shape, F32)
        for b in range(REL_BUCKETS):
            acc = jnp.where(bk == b, relb_ref[b, h], acc)
        if t == 0:
            key = lax.broadcasted_iota(jnp.int32, bk.shape, 0)
            qry = lax.broadcasted_iota(jnp.int32, bk.shape, 1)
            acc = jnp.where(key <= qry, acc, NEG)
        o_ref[0, t] = acc


def _bias_tiles(rel_bias):
    assert REL_MAX_DIST <= MOBA_BLOCK
    key = jnp.arange(MOBA_BLOCK, dtype=jnp.int32)[:, None]
    qry = jnp.arange(MOBA_BLOCK, dtype=jnp.int32)[None, :]
    buckets = jnp.stack([_rel_bucket(t * MOBA_BLOCK + qry - key) for t in range(3)])
    return pl.pallas_call(
        _bias_kernel,
        grid=(ATTN_HEADS,),
        in_specs=[pl.BlockSpec(memory_space=pltpu.SMEM),
                  pl.BlockSpec((3, MOBA_BLOCK, MOBA_BLOCK), lambda h: (0, 0, 0))],
        out_specs=pl.BlockSpec((1, 3, MOBA_BLOCK, MOBA_BLOCK), lambda h: (h, 0, 0, 0)),
        out_shape=jax.ShapeDtypeStruct((ATTN_HEADS, 3, MOBA_BLOCK, MOBA_BLOCK), F32),
        compiler_params=_params(("parallel",)),
        name="attn_bias_tiles",
    )(rel_bias.astype(F32), buckets)


ATTN_HEADS_PER_STEP = 8


def _attn_kernel(q_ref, k_ref, vt_ref, bias_ref, o_ref, kmean_ref, sel_ref, acc_ref, *, nb):
    i = pl.program_id(2)
    scale = HEAD_DIM ** -0.5
    heads = range(ATTN_HEADS_PER_STEP)
    hsl = [slice(g * HEAD_DIM, (g + 1) * HEAD_DIM) for g in heads]

    @pl.when(i == 0)
    def _():
        for g in heads:
            for n in range(nb):
                kb = k_ref[n * MOBA_BLOCK:(n + 1) * MOBA_BLOCK, hsl[g]].astype(F32)
                kmean_ref[g, n:n + 1, :] = jnp.mean(kb, axis=0, keepdims=True)

    qs = [q_ref[:, hsl[g]] for g in heads]
    for g in heads:
        gate = lax.dot_general(kmean_ref[g].astype(BF16), qs[g], _NT_DIMS,
                               preferred_element_type=F32)
        blk = lax.broadcasted_iota(jnp.int32, gate.shape, 0)
        rank = jnp.zeros(gate.shape, F32)
        for m in range(nb):
            gm = gate[m:m + 1, :]
            beats = jnp.where(gm > gate, 1.0, jnp.where(gm == gate, jnp.where(m < blk, 1.0, 0.0), 0.0))
            rank = rank + beats * jnp.where(m < i, 1.0, 0.0)
        sel_ref[g] = jnp.where(blk < i, jnp.where(rank < MOBA_TOPK, 1.0, 0.0), 0.0)

    def scores(g, j, tile):
        koff = pl.multiple_of(j * MOBA_BLOCK, MOBA_BLOCK)
        kj = k_ref[pl.ds(koff, MOBA_BLOCK), hsl[g]]
        s = lax.dot_general(kj, qs[g], _NT_DIMS, preferred_element_type=F32)
        return s * scale + bias_ref[g, tile]

    s0 = [scores(g, i, 0) for g in heads]
    m0 = [jnp.max(s0[g], axis=0, keepdims=True) for g in heads]
    p0 = [jnp.exp(s0[g] - m0[g]) for g in heads]
    l0 = [jnp.sum(p0[g], axis=0, keepdims=True) for g in heads]
    for g in heads:
        acc_ref[g] = jnp.dot(vt_ref[i, hsl[g], :], p0[g].astype(BF16), preferred_element_type=F32)

    def body(j, carry):
        ms, ls = carry
        tile = jnp.minimum(i - j, 2)
        ss = [scores(g, j, tile) for g in heads]
        ss = [jnp.where(sel_ref[g, pl.ds(j, 1), :] > 0.5, ss[g], NEG) for g in heads]
        new_m = [jnp.maximum(ms[g], jnp.max(ss[g], axis=0, keepdims=True)) for g in heads]
        alpha = [jnp.exp(ms[g] - new_m[g]) for g in heads]
        ps = [jnp.exp(ss[g] - new_m[g]) for g in heads]
        new_l = [alpha[g] * ls[g] + jnp.sum(ps[g], axis=0, keepdims=True) for g in heads]
        pv = [jnp.dot(vt_ref[j, hsl[g], :], ps[g].astype(BF16), preferred_element_type=F32) for g in heads]
        for g in heads:
            acc_ref[g] = alpha[g] * acc_ref[g] + pv[g]
        return tuple(new_m), tuple(new_l)

    _, l_f = lax.fori_loop(0, i, body, (tuple(m0), tuple(l0)))
    for g in heads:
        o_ref[:, hsl[g]] = (acc_ref[g] / l_f[g]).T.astype(o_ref.dtype)


def _attention(qk, vt3, bias, batch, seq):
    nb = seq // MOBA_BLOCK
    t = batch * seq
    gh = ATTN_HEADS_PER_STEP
    width = gh * HEAD_DIM
    groups = ATTN_HEADS // gh
    return pl.pallas_call(
        functools.partial(_attn_kernel, nb=nb),
        grid=(batch, groups, nb),
        in_specs=[
            pl.BlockSpec((MOBA_BLOCK, width), lambda b, h, i: (b * nb + i, h)),
            pl.BlockSpec((seq, width), lambda b, h, i: (b, groups + h)),
            pl.BlockSpec((nb, width, MOBA_BLOCK), lambda b, h, i: (b, h, 0)),
            pl.BlockSpec((gh, 3, MOBA_BLOCK, MOBA_BLOCK), lambda b, h, i: (h, 0, 0, 0)),
        ],
        out_specs=pl.BlockSpec((MOBA_BLOCK, width), lambda b, h, i: (b * nb + i, h)),
        out_shape=jax.ShapeDtypeStruct((t, ATTN_WIDTH), BF16),
        scratch_shapes=[pltpu.VMEM((gh, nb, HEAD_DIM), F32),
                        pltpu.VMEM((gh, nb, MOBA_BLOCK), F32),
                        pltpu.VMEM((gh, HEAD_DIM, MOBA_BLOCK), F32)],
        compiler_params=_params(("parallel", "parallel", "arbitrary")),
        name="moba_attention",
    )(qk, qk, vt3, bias)


def _rglru_kernel(xr_ref, yr_ref, cw_ref, cb_ref, wa_ref, wx_ref, ba_ref, bx_ref, lam_ref,
                  o_ref, tail_ref, h_ref):
    c = pl.program_id(1)
    tc = xr_ref.shape[0]

    @pl.when(c == 0)
    def _():
        tail_ref[...] = jnp.zeros(tail_ref.shape, F32)
        h_ref[...] = jnp.zeros(h_ref.shape, F32)

    x = xr_ref[...]
    grp_shape = (tc // 8, 8, x.shape[1])
    x3 = x.reshape(grp_shape)
    tail3 = tail_ref[...].reshape(1, 8, x.shape[1])
    sub = lax.broadcasted_iota(jnp.int32, grp_shape, 1)
    conv = cb_ref[...] + cw_ref[CONV_WIDTH - 1:CONV_WIDTH, :] * x3
    for s in range(1, CONV_WIDTH):
        k = CONV_WIDTH - 1 - s
        rolled = pltpu.roll(x3, s, axis=1)
        rolled_prev = jnp.concatenate([pltpu.roll(tail3, s, axis=1), rolled[:-1]], axis=0)
        conv = conv + cw_ref[k:k + 1, :] * jnp.where(sub >= s, rolled, rolled_prev)
    conv = conv.reshape(x.shape)
    tail_ref[...] = x[tc - 8:, :]

    xb16 = conv.astype(BF16)
    pre_a, pre_x = [], []
    for g in range(LRU_BLOCKS):
        xs = xb16[:, g * LRU_BLOCK_DIM:(g + 1) * LRU_BLOCK_DIM]
        pre_a.append(jnp.dot(xs, wa_ref[g], preferred_element_type=F32))
        pre_x.append(jnp.dot(xs, wx_ref[g], preferred_element_type=F32))
    r_gate = jax.nn.sigmoid(jnp.concatenate(pre_a, axis=1) + ba_ref[...])
    i_gate = jax.nn.sigmoid(jnp.concatenate(pre_x, axis=1) + bx_ref[...])
    z = -lam_ref[...]
    softplus = jnp.maximum(z, 0.0) + jnp.log1p(jnp.exp(-jnp.abs(z)))
    log_a = (-LRU_C) * r_gate * softplus
    a = jnp.exp(log_a)
    th = jnp.tanh(log_a)
    y2 = -2.0 * th / (1.0 - th)
    u = jnp.where(y2 > 0.0, y2 * lax.rsqrt(y2), 0.0) * (i_gate * conv)

    grp_shape = (tc // 8, 8, a.shape[1])
    a = a.reshape(grp_shape)
    u = u.reshape(grp_shape)
    sub = lax.broadcasted_iota(jnp.int32, grp_shape, 1)
    s = 1
    while s < 8:
        a_sh = pltpu.roll(a, s, axis=1)
        u_sh = pltpu.roll(u, s, axis=1)
        valid = sub >= s
        u = jnp.where(valid, a * u_sh + u, u)
        a = jnp.where(valid, a * a_sh, a)
        s *= 2
    h = h_ref[...]
    gy = _gelu_tanh(yr_ref[...])
    for grp in range(tc // 8):
        rows = slice(grp * 8, (grp + 1) * 8)
        hh = a[grp] * h + u[grp]
        h = hh[7:8, :]
        o_ref[rows, :] = (hh * gy[rows, :]).astype(o_ref.dtype)
    h_ref[...] = h


def _rglru(r, conv_w, conv_b, wa, wx, ba, bx, lam, batch, seq, tc=256):
    t = batch * seq
    nc = seq // tc
    w = LRU_WIDTH
    row = lambda v: v.reshape(1, w).astype(F32)
    full2 = lambda shape: pl.BlockSpec(shape, lambda b, c: (0,) * len(shape))
    return pl.pallas_call(
        _rglru_kernel,
        grid=(batch, nc),
        in_specs=[pl.BlockSpec((tc, w), lambda b, c: (b * nc + c, 0)),
                  pl.BlockSpec((tc, w), lambda b, c: (b * nc + c, 1)),
                  full2((CONV_WIDTH, w)), full2((1, w)),
                  full2((LRU_BLOCKS, LRU_BLOCK_DIM, LRU_BLOCK_DIM)),
                  full2((LRU_BLOCKS, LRU_BLOCK_DIM, LRU_BLOCK_DIM)),
                  full2((1, w)), full2((1, w)), full2((1, w))],
        out_specs=pl.BlockSpec((tc, w), lambda b, c: (b * nc + c, 0)),
        out_shape=jax.ShapeDtypeStruct((t, w), BF16),
        scratch_shapes=[pltpu.VMEM((8, w), F32), pltpu.VMEM((1, w), F32)],
        compiler_params=_params(("parallel", "arbitrary")),
        name="rglru",
    )(r, r, conv_w.astype(F32), row(conv_b), wa.astype(BF16), wx.astype(BF16),
      row(ba), row(bx), row(lam))


def _merge_kernel(oa_ref, or_ref, g0_ref, g1_ref, wb0_ref, wb1_ref, wo_ref, x_ref, gn_ref,
                  x1_ref, xn_ref):
    pb0 = jnp.dot(oa_ref[...], wb0_ref[...], preferred_element_type=F32)
    pb1 = jnp.dot(or_ref[...], wb1_ref[...], preferred_element_type=F32)
    merged = jax.nn.sigmoid(g0_ref[...]) * pb0 + jax.nn.sigmoid(g1_ref[...]) * pb1
    x1 = x_ref[...] + jnp.dot(merged.astype(BF16), wo_ref[...], preferred_element_type=F32)
    x1_ref[...] = x1
    xn_ref[...] = _rms(x1, gn_ref[...]).astype(xn_ref.dtype)


def _merge_out(o_att, o_rec, gl, wb0, wb1, w_out, x, g_ffn, tm=256):
    t, d = x.shape
    cw = o_att.shape[1]
    resident = lambda shape: pl.BlockSpec(shape, lambda i: (0, 0), pipeline_mode=pl.Buffered(1))
    return pl.pallas_call(
        _merge_kernel,
        grid=(t // tm,),
        in_specs=[pl.BlockSpec((tm, cw), lambda i: (i, 0)),
                  pl.BlockSpec((tm, cw), lambda i: (i, 0)),
                  pl.BlockSpec((tm, d), lambda i: (i, 0)),
                  pl.BlockSpec((tm, d), lambda i: (i, 1)),
                  resident((cw, d)), resident((cw, d)), resident((d, d)),
                  pl.BlockSpec((tm, d), lambda i: (i, 0)),
                  pl.BlockSpec((1, d), lambda i: (0, 0))],
        out_specs=[pl.BlockSpec((tm, d), lambda i: (i, 0)),
                   pl.BlockSpec((tm, d), lambda i: (i, 0))],
        out_shape=[jax.ShapeDtypeStruct((t, d), F32), jax.ShapeDtypeStruct((t, d), BF16)],
        compiler_params=_params(("parallel",)),
        name="merge_out",
    )(o_att, o_rec, gl, gl, wb0, wb1, w_out, x, g_ffn.reshape(1, d).astype(F32))


def _oddeven_merge(lo, hi, r):
    step = r * 2
    if step < hi - lo:
        yield from _oddeven_merge(lo, hi, step)
        yield from _oddeven_merge(lo + r, hi, step)
        yield from [(i, i + r) for i in range(lo + r, hi - r, step)]
    else:
        yield (lo, lo + r)


def _oddeven_merge_sort(lo, hi):
    if hi - lo >= 1:
        mid = lo + (hi - lo) // 2
        yield from _oddeven_merge_sort(lo, mid)
        yield from _oddeven_merge_sort(mid + 1, hi)
        yield from _oddeven_merge(lo, hi, 1)


def _peer_scores_kernel(q_ref, keys_ref, thr_ref, e1_ref, s2_ref, e2_ref, s_ref, vals_ref, cand_ref):
    nhc = keys_ref.shape[0]
    k = PEER_TOPK
    for hc in range(nhc):
        qs = q_ref[:, hc * PEER_NKEYS:(hc + 1) * PEER_NKEYS]
        s_ref[hc] = lax.dot_general(keys_ref[hc], qs, _NT_DIMS, preferred_element_type=F32)

    n_lvl = PEER_NKEYS // 8
    network = list(_oddeven_merge_sort(0, n_lvl - 1))

    def top_values(hc, carry):
        x = s_ref[hc]
        lvl = [x[8 * v:8 * v + 8, :] for v in range(n_lvl)]
        for a, b in network:
            lvl[a], lvl[b] = jnp.maximum(lvl[a], lvl[b]), jnp.minimum(lvl[a], lvl[b])
        for r in range(k + 1):
            m = jnp.max(lvl[0], axis=0, keepdims=True)
            vals_ref[hc, r:r + 1, :] = m
            hit = lvl[0] == m
            for v in range(min(k - r, n_lvl)):
                below = lvl[v + 1] if v + 1 < n_lvl else LOWEST
                lvl[v] = jnp.where(hit, below, lvl[v])
        return carry

    lax.fori_loop(0, nhc, top_values, 0)

    def head_stats(h, carry):
        v1 = vals_ref[2 * h, 0:k + 1, :]
        v2 = vals_ref[2 * h + 1, 0:k + 1, :]
        cand_ref[...] = jnp.full(cand_ref.shape, LOWEST, F32)
        off = 0
        for a in range(k + 1):
            nb = (k + 1) // (a + 1)
            cand_ref[off:off + nb, :] = v1[a:a + 1, :] + v2[0:nb, :]
            off += nb
        cand = cand_ref[...]
        x = cand
        kth = None
        for r in range(k):
            kth = jnp.max(x, axis=0, keepdims=True)
            x = jnp.where(x == kth, LOWEST, x)
        nxt = jnp.max(x, axis=0, keepdims=True)
        m1 = v1[0:1, :]
        m2 = v2[0:1, :]
        tau = 0.5 * (kth + nxt)
        z = jnp.sum(jnp.where(cand >= tau, jnp.exp(cand - (m1 + m2)), 0.0), axis=0, keepdims=True)
        s1 = s_ref[2 * h]
        s2 = s_ref[2 * h + 1]
        thr_ref[h] = tau - s1
        e1_ref[h] = jnp.exp(s1 - m1)
        e2 = jnp.exp(s2 - m2) * (0.5 / z)
        for c in range(s2_ref.shape[1]):
            s2_ref[h, c] = s2[:, c * 128:(c + 1) * 128]
            e2_ref[h, c] = e2[:, c * 128:(c + 1) * 128]
        return carry

    lax.fori_loop(0, nhc // 2, head_stats, 0)


def _peer_scores(q, keys, tm=256):
    t, _ = q.shape
    nhc = keys.shape[0]
    out_spec = pl.BlockSpec((PEER_HEADS, PEER_NKEYS, tm), lambda i: (0, 0, i))
    out_shape = jax.ShapeDtypeStruct((PEER_HEADS, PEER_NKEYS, t), F32)
    chunk_spec = pl.BlockSpec((PEER_HEADS, tm // 128, PEER_NKEYS, 128), lambda i: (0, i, 0, 0))
    chunk_shape = jax.ShapeDtypeStruct((PEER_HEADS, t // 128, PEER_NKEYS, 128), F32)
    n_cand = sum((PEER_TOPK + 1) // (a + 1) for a in range(PEER_TOPK + 1))
    n_cand = -(-n_cand // 8) * 8
    return pl.pallas_call(
        _peer_scores_kernel,
        grid=(t // tm,),
        in_specs=[pl.BlockSpec((tm, nhc * PEER_NKEYS), lambda i: (i, 0)),
                  pl.BlockSpec(keys.shape, lambda i: (0, 0, 0))],
        out_specs=[out_spec, out_spec, chunk_spec, chunk_spec],
        out_shape=[out_shape, out_shape, chunk_shape, chunk_shape],
        scratch_shapes=[pltpu.VMEM((nhc, PEER_NKEYS, tm), F32),
                        pltpu.VMEM((nhc, PEER_TOPK + 8, tm), F32),
                        pltpu.VMEM((n_cand, tm), F32)],
        compiler_params=_params(("parallel",)),
        name="peer_scores",
    )(q, keys)


def _transpose_kernel(x_ref, o_ref):
    o_ref[...] = x_ref[...].T.astype(o_ref.dtype)


def _transposed_tiles(table, tn):
    e, d = table.shape
    return pl.pallas_call(
        _transpose_kernel,
        grid=(e // tn,),
        in_specs=[pl.BlockSpec((tn, d), lambda i: (i, 0))],
        out_specs=pl.BlockSpec((None, d, tn), lambda i: (i, 0, 0)),
        out_shape=jax.ShapeDtypeStruct((e // tn, d, tn), BF16),
        compiler_params=_params(("parallel",)),
        name="expert_value_tiles",
    )(table)


def _peer_dense_kernel(xn_ref, thr_ref, e1_ref, s2_ref, e2_ref, u_ref, vt_ref, acc_ref,
                       xnt_ref, act_ref, p_ref):
    j = pl.program_id(1)
    lane_chunks, tn, _ = act_ref.shape
    rows = tn // PEER_NKEYS

    @pl.when(j == 0)
    def _():
        acc_ref[...] = jnp.zeros(acc_ref.shape, F32)
        xnt_ref[...] = xn_ref[...].astype(F32).T.astype(xnt_ref.dtype)

    act = jnp.dot(u_ref[...], xnt_ref[...], preferred_element_type=F32)
    for lc in range(lane_chunks):
        act_ref[lc] = act[:, lc * 128:(lc + 1) * 128]

    def weigh(r, carry):
        i1 = j * rows + r
        roff = pl.multiple_of(r * PEER_NKEYS, PEER_NKEYS)
        thr_rows = [thr_ref[h, pl.ds(i1, 1), :] for h in range(PEER_HEADS)]
        e1_rows = [e1_ref[h, pl.ds(i1, 1), :] for h in range(PEER_HEADS)]
        for lc in range(lane_chunks):
            lanes = slice(lc * 128, (lc + 1) * 128)
            w = None
            for h in range(PEER_HEADS):
                w_h = jnp.where(s2_ref[h, lc] >= thr_rows[h][:, lanes],
                                e2_ref[h, lc] * e1_rows[h][:, lanes], 0.0)
                w = w_h if w is None else w + w_h
            a = act_ref[lc, pl.ds(roff, PEER_NKEYS), :]
            p_ref[pl.ds(roff, PEER_NKEYS), lanes] = (_gelu_tanh_x2(a) * w).astype(p_ref.dtype)
        return carry

    lax.fori_loop(0, rows, weigh, 0)
    res = jnp.dot(vt_ref[...], p_ref[...], preferred_element_type=F32)
    for lc in range(lane_chunks):
        acc_ref[lc] += res[:, lc * 128:(lc + 1) * 128]


def _peer_dense(xn, thr, e1, s2, e2, u_bf, vt_tiles, tm=512):
    t, d = xn.shape
    n_tiles, _, tn = vt_tiles.shape
    route_spec = pl.BlockSpec((PEER_HEADS, PEER_NKEYS, tm), lambda i, j: (0, 0, i))
    chunk_spec = pl.BlockSpec((PEER_HEADS, tm // 128, PEER_NKEYS, 128), lambda i, j: (0, i, 0, 0))
    return pl.pallas_call(
        _peer_dense_kernel,
        grid=(t // tm, n_tiles),
        in_specs=[pl.BlockSpec((tm, d), lambda i, j: (i, 0)),
                  route_spec, route_spec, chunk_spec, chunk_spec,
                  pl.BlockSpec((tn, d), lambda i, j: (j, 0)),
                  pl.BlockSpec((None, d, tn), lambda i, j: (j, 0, 0))],
        out_specs=pl.BlockSpec((tm // 128, d, 128), lambda i, j: (i, 0, 0)),
        out_shape=jax.ShapeDtypeStruct((t // 128, d, 128), F32),
        scratch_shapes=[pltpu.VMEM((d, tm), BF16), pltpu.VMEM((tm // 128, tn, 128), F32),
                        pltpu.VMEM((tn, tm), BF16)],
        compiler_params=_params(("parallel", "arbitrary")),
        name="peer_dense",
    )(xn, thr, e1, s2, e2, u_bf, vt_tiles)


def _final_kernel(x1_ref, pt_ref, g_ref, y_ref):
    for c in range(pt_ref.shape[0]):
        rows = slice(c * 128, (c + 1) * 128)
        y_ref[rows, :] = _rms(x1_ref[rows, :] + pt_ref[c].T, g_ref[...])


def _residual_norm(x1, peer_chunks, g, tm=256):
    t, d = x1.shape
    return pl.pallas_call(
        _final_kernel,
        grid=(t // tm,),
        in_specs=[pl.BlockSpec((tm, d), lambda i: (i, 0)),
                  pl.BlockSpec((tm // 128, d, 128), lambda i: (i, 0, 0)),
                  pl.BlockSpec((1, d), lambda i: (0, 0))],
        out_specs=pl.BlockSpec((tm, d), lambda i: (i, 0)),
        out_shape=jax.ShapeDtypeStruct((t, d), F32),
        compiler_params=_params(("parallel",)),
        name="residual_norm",
    )(x1, peer_chunks, g.reshape(1, d).astype(F32))


def kernel(x, norm_mix_g, w_in, conv_w, conv_b, lru_wa, lru_ba, lru_wx, lru_bx, lru_lambda,
           w_branch, w_out, rel_bias, norm_ffn_g, peer_wq, peer_keys, peer_u, peer_v, norm_final_g):
    batch, seq, d = x.shape
    t = batch * seq
    assert w_in.shape[0] == 1, "single-layer trunk: the final rmsnorm is fused into the PEER epilogue"
    xt = x.reshape(t, d)
    bias = _bias_tiles(rel_bias)
    a_w = ATTN_WIDTH
    w_l = w_in[0]
    h = _rmsnorm(xt, norm_mix_g[0], BF16)
    qk = _matmul(h, w_l, 0, 2 * a_w, BF16, "proj_qk")
    vt3 = _v_transposed(w_l[:, 2 * a_w:3 * a_w].T.astype(BF16), h)
    r = _matmul(h, w_l, 3 * a_w, 2 * LRU_WIDTH, F32, "proj_rec")
    gl = _matmul(h, w_l, 3 * a_w + 2 * LRU_WIDTH, 2 * d, F32, "proj_gate")
    o_att = _attention(qk, vt3, bias, batch, seq)
    o_rec = _rglru(r, conv_w[0], conv_b[0], lru_wa[0], lru_wx[0], lru_ba[0], lru_bx[0],
                   lru_lambda[0], batch, seq)
    x1, xn = _merge_out(o_att, o_rec, gl, w_branch[0, 0].astype(BF16),
                        w_branch[0, 1].astype(BF16), w_out[0].astype(BF16), xt, norm_ffn_g[0])
    q = _matmul(xn, peer_wq[0], 0, peer_wq.shape[2], BF16, "peer_query")
    keys = peer_keys[0].reshape(PEER_HEADS * 2, PEER_NKEYS, PEER_DKEY // 2).astype(BF16)
    thr, e1, s2, e2 = _peer_scores(q, keys)
    vt_tiles = _transposed_tiles(peer_v[0], PEER_EXPERT_TILE)
    peer_out = _peer_dense(xn, thr, e1, s2, e2, peer_u[0].astype(BF16), vt_tiles)
    y = _residual_norm(x1, peer_out, norm_final_g)
    return y.reshape(batch, seq, d)
```

```python
import functools
import math

import jax
import jax.numpy as jnp
from jax import lax
from jax.experimental import pallas as pl
from jax.experimental.pallas import tpu as pltpu

ATTN_HEADS = 8
HEAD_DIM = 128
ATTN_WIDTH = ATTN_HEADS * HEAD_DIM
MOBA_BLOCK = 256
MOBA_TOPK = 3
LRU_WIDTH = 1024
LRU_BLOCKS = 8
LRU_BLOCK_DIM = LRU_WIDTH // LRU_BLOCKS
CONV_WIDTH = 4
LRU_C = 8.0
REL_BUCKETS = 32
REL_MAX_DIST = 128
PEER_HEADS = 8
PEER_NKEYS = 128
PEER_DKEY = 256
PEER_TOPK = 16
EPS = 1e-6
NEG = -1e30
LOWEST = -3.0e38
PEER_EXPERT_TILE = 1024

V7X_VMEM_LIMIT_BYTES = 56 * 1024 * 1024

F32 = jnp.float32
BF16 = jnp.bfloat16

_NT_DIMS = (((1,), (1,)), ((), ()))


def _params(semantics):
    return pltpu.CompilerParams(dimension_semantics=semantics,
                                vmem_limit_bytes=V7X_VMEM_LIMIT_BYTES)


def _gelu_tanh_x2(x):
    c = math.sqrt(2.0 / math.pi)
    return x * (1.0 + jnp.tanh(x * (c + (0.044715 * c) * (x * x))))


def _gelu_tanh(x):
    return 0.5 * _gelu_tanh_x2(x)


def _rms(x, g):
    ms = jnp.mean(x * x, axis=-1, keepdims=True)
    return x * lax.rsqrt(ms + EPS) * g


def _rmsnorm_kernel(x_ref, g_ref, o_ref):
    o_ref[...] = _rms(x_ref[...], g_ref[...]).astype(o_ref.dtype)


def _rmsnorm(x, g, out_dtype, tm=512):
    t, d = x.shape
    return pl.pallas_call(
        _rmsnorm_kernel,
        grid=(t // tm,),
        in_specs=[pl.BlockSpec((tm, d), lambda i: (i, 0)),
                  pl.BlockSpec((1, d), lambda i: (0, 0))],
        out_specs=pl.BlockSpec((tm, d), lambda i: (i, 0)),
        out_shape=jax.ShapeDtypeStruct((t, d), out_dtype),
        compiler_params=_params(("parallel",)),
        name="rmsnorm",
    )(x, g.reshape(1, d))


def _mm_kernel(a_ref, b_ref, o_ref, w_ref):
    @pl.when(pl.program_id(1) == 0)
    def _():
        w_ref[...] = b_ref[...].astype(w_ref.dtype)

    o_ref[...] = jnp.dot(a_ref[...], w_ref[...], preferred_element_type=F32).astype(o_ref.dtype)


def _matmul(a, b, col_start, n, out_dtype, name, tm=1024, tn=1024):
    m, k = a.shape
    assert col_start % tn == 0 and n % tn == 0
    col_blk = col_start // tn
    return pl.pallas_call(
        _mm_kernel,
        grid=(n // tn, m // tm),
        in_specs=[pl.BlockSpec((tm, k), lambda j, i: (i, 0)),
                  pl.BlockSpec((k, tn), lambda j, i: (0, col_blk + j))],
        out_specs=pl.BlockSpec((tm, tn), lambda j, i: (i, j)),
        out_shape=jax.ShapeDtypeStruct((m, n), out_dtype),
        scratch_shapes=[pltpu.VMEM((k, tn), BF16)],
        compiler_params=_params(("parallel", "arbitrary")),
        name=name,
    )(a, b)


def _vt_kernel(w_ref, h_ref, o_ref):
    res = lax.dot_general(w_ref[...], h_ref[...], _NT_DIMS, preferred_element_type=F32)
    for t in range(o_ref.shape[0]):
        o_ref[t] = res[:, t * MOBA_BLOCK:(t + 1) * MOBA_BLOCK].astype(o_ref.dtype)


def _v_transposed(w_t, h, tm=1024):
    c, k = w_t.shape
    t, _ = h.shape
    nb = tm // MOBA_BLOCK
    return pl.pallas_call(
        _vt_kernel,
        grid=(t // tm,),
        in_specs=[pl.BlockSpec((c, k), lambda i: (0, 0)),
                  pl.BlockSpec((tm, k), lambda i: (i, 0))],
        out_specs=pl.BlockSpec((nb, c, MOBA_BLOCK), lambda i: (i, 0, 0)),
        out_shape=jax.ShapeDtypeStruct((t // MOBA_BLOCK, c, MOBA_BLOCK), BF16),
        compiler_params=_params(("parallel",)),
        name="v_transposed",
    )(w_t, h)


def _rel_bucket(dist):
    n = jnp.maximum(dist, 0)
    max_exact = REL_BUCKETS // 2
    nf = jnp.maximum(n, 1).astype(F32)
    large = max_exact + (jnp.log(nf / max_exact) / math.log(REL_MAX_DIST / max_exact)
                         * (REL_BUCKETS - max_exact)).astype(jnp.int32)
    large = jnp.minimum(large, REL_BUCKETS - 1)
    return jnp.where(n < max_exact, n, large)


def _bias_kernel(relb_ref, bucket_ref, o_ref):
    h = pl.program_id(0)
    for t in range(3):
        bk = bucket_ref[t]
        acc = jnp.zeros(bk.shape, F32)
        for b in range(REL_BUCKETS):
            acc = jnp.where(bk == b, relb_ref[b, h], acc)
        if t == 0:
            key = lax.broadcasted_iota(jnp.int32, bk.shape, 0)
            qry = lax.broadcasted_iota(jnp.int32, bk.shape, 1)
            acc = jnp.where(key <= qry, acc, NEG)
        o_ref[0, t] = acc


def _bias_tiles(rel_bias):
    assert REL_MAX_DIST <= MOBA_BLOCK
    key = jnp.arange(MOBA_BLOCK, dtype=jnp.int32)[:, None]
    qry = jnp.arange(MOBA_BLOCK, dtype=jnp.int32)[None, :]
    buckets = jnp.stack([_rel_bucket(t * MOBA_BLOCK + qry - key) for t in range(3)])
    return pl.pallas_call(
        _bias_kernel,
        grid=(ATTN_HEADS,),
        in_specs=[pl.BlockSpec(memory_space=pltpu.SMEM),
                  pl.BlockSpec((3, MOBA_BLOCK, MOBA_BLOCK), lambda h: (0, 0, 0))],
        out_specs=pl.BlockSpec((1, 3, MOBA_BLOCK, MOBA_BLOCK), lambda h: (h, 0, 0, 0)),
        out_shape=jax.ShapeDtypeStruct((ATTN_HEADS, 3, MOBA_BLOCK, MOBA_BLOCK), F32),
        compiler_params=_params(("parallel",)),
        name="attn_bias_tiles",
    )(rel_bias.astype(F32), buckets)


ATTN_HEADS_PER_STEP = 8


def _attn_kernel(q_ref, k_ref, vt_ref, bias_ref, o_ref, kmean_ref, sel_ref, acc_ref, *, nb):
    i = pl.program_id(2)
    scale = HEAD_DIM ** -0.5
    heads = range(ATTN_HEADS_PER_STEP)
    hsl = [slice(g * HEAD_DIM, (g + 1) * HEAD_DIM) for g in heads]

    @pl.when(i == 0)
    def _():
        for g in heads:
            for n in range(nb):
                kb = k_ref[n * MOBA_BLOCK:(n + 1) * MOBA_BLOCK, hsl[g]].astype(F32)
                kmean_ref[g, n:n + 1, :] = jnp.mean(kb, axis=0, keepdims=True)

    qs = [q_ref[:, hsl[g]] for g in heads]
    for g in heads:
        gate = lax.dot_general(kmean_ref[g].astype(BF16), qs[g], _NT_DIMS,
                               preferred_element_type=F32)
        blk = lax.broadcasted_iota(jnp.int32, gate.shape, 0)
        rank = jnp.zeros(gate.shape, F32)
        for m in range(nb):
            gm = gate[m:m + 1, :]
            beats = jnp.where(gm > gate, 1.0, jnp.where(gm == gate, jnp.where(m < blk, 1.0, 0.0), 0.0))
            rank = rank + beats * jnp.where(m < i, 1.0, 0.0)
        sel_ref[g] = jnp.where(blk < i, jnp.where(rank < MOBA_TOPK, 1.0, 0.0), 0.0)

    def scores(g, j, tile):
        koff = pl.multiple_of(j * MOBA_BLOCK, MOBA_BLOCK)
        kj = k_ref[pl.ds(koff, MOBA_BLOCK), hsl[g]]
        s = lax.dot_general(kj, qs[g], _NT_DIMS, preferred_element_type=F32)
        return s * scale + bias_ref[g, tile]

    s0 = [scores(g, i, 0) for g in heads]
    m0 = [jnp.max(s0[g], axis=0, keepdims=True) for g in heads]
    p0 = [jnp.exp(s0[g] - m0[g]) for g in heads]
    l0 = [jnp.sum(p0[g], axis=0, keepdims=True) for g in heads]
    for g in heads:
        acc_ref[g] = jnp.dot(vt_ref[i, hsl[g], :], p0[g].astype(BF16), preferred_element_type=F32)

    def body(j, carry):
        ms, ls = carry
        tile = jnp.minimum(i - j, 2)
        ss = [scores(g, j, tile) for g in heads]
        ss = [jnp.where(sel_ref[g, pl.ds(j, 1), :] > 0.5, ss[g], NEG) for g in heads]
        new_m = [jnp.maximum(ms[g], jnp.max(ss[g], axis=0, keepdims=True)) for g in heads]
        alpha = [jnp.exp(ms[g] - new_m[g]) for g in heads]
        ps = [jnp.exp(ss[g] - new_m[g]) for g in heads]
        new_l = [alpha[g] * ls[g] + jnp.sum(ps[g], axis=0, keepdims=True) for g in heads]
        pv = [jnp.dot(vt_ref[j, hsl[g], :], ps[g].astype(BF16), preferred_element_type=F32) for g in heads]
        for g in heads:
            acc_ref[g] = alpha[g] * acc_ref[g] + pv[g]
        return tuple(new_m), tuple(new_l)

    _, l_f = lax.fori_loop(0, i, body, (tuple(m0), tuple(l0)))
    for g in heads:
        o_ref[:, hsl[g]] = (acc_ref[g] / l_f[g]).T.astype(o_ref.dtype)


def _attention(qk, vt3, bias, batch, seq):
    nb = seq // MOBA_BLOCK
    t = batch * seq
    gh = ATTN_HEADS_PER_STEP
    width = gh * HEAD_DIM
    groups = ATTN_HEADS // gh
    return pl.pallas_call(
        functools.partial(_attn_kernel, nb=nb),
        grid=(batch, groups, nb),
        in_specs=[
            pl.BlockSpec((MOBA_BLOCK, width), lambda b, h, i: (b * nb + i, h)),
            pl.BlockSpec((seq, width), lambda b, h, i: (b, groups + h)),
            pl.BlockSpec((nb, width, MOBA_BLOCK), lambda b, h, i: (b, h, 0)),
            pl.BlockSpec((gh, 3, MOBA_BLOCK, MOBA_BLOCK), lambda b, h, i: (h, 0, 0, 0)),
        ],
        out_specs=pl.BlockSpec((MOBA_BLOCK, width), lambda b, h, i: (b * nb + i, h)),
        out_shape=jax.ShapeDtypeStruct((t, ATTN_WIDTH), BF16),
        scratch_shapes=[pltpu.VMEM((gh, nb, HEAD_DIM), F32),
                        pltpu.VMEM((gh, nb, MOBA_BLOCK), F32),
                        pltpu.VMEM((gh, HEAD_DIM, MOBA_BLOCK), F32)],
        compiler_params=_params(("parallel", "parallel", "arbitrary")),
        name="moba_attention",
    )(qk, qk, vt3, bias)


def _rglru_kernel(xr_ref, yr_ref, cw_ref, cb_ref, wa_ref, wx_ref, ba_ref, bx_ref, lam_ref,
                  o_ref, tail_ref, h_ref):
    c = pl.program_id(1)
    tc = xr_ref.shape[0]

    @pl.when(c == 0)
    def _():
        tail_ref[...] = jnp.zeros(tail_ref.shape, F32)
        h_ref[...] = jnp.zeros(h_ref.shape, F32)

    x = xr_ref[...]
    grp_shape = (tc // 8, 8, x.shape[1])
    x3 = x.reshape(grp_shape)
    tail3 = tail_ref[...].reshape(1, 8, x.shape[1])
    sub = lax.broadcasted_iota(jnp.int32, grp_shape, 1)
    conv = cb_ref[...] + cw_ref[CONV_WIDTH - 1:CONV_WIDTH, :] * x3
    for s in range(1, CONV_WIDTH):
        k = CONV_WIDTH - 1 - s
        rolled = pltpu.roll(x3, s, axis=1)
        rolled_prev = jnp.concatenate([pltpu.roll(tail3, s, axis=1), rolled[:-1]], axis=0)
        conv = conv + cw_ref[k:k + 1, :] * jnp.where(sub >= s, rolled, rolled_prev)
    conv = conv.reshape(x.shape)
    tail_ref[...] = x[tc - 8:, :]

    xb16 = conv.astype(BF16)
    pre_a, pre_x = [], []
    for g in range(LRU_BLOCKS):
        xs = xb16[:, g * LRU_BLOCK_DIM:(g + 1) * LRU_BLOCK_DIM]
        pre_a.append(jnp.dot(xs, wa_ref[g], preferred_element_type=F32))
        pre_x.append(jnp.dot(xs, wx_ref[g], preferred_element_type=F32))
    r_gate = jax.nn.sigmoid(jnp.concatenate(pre_a, axis=1) + ba_ref[...])
    i_gate = jax.nn.sigmoid(jnp.concatenate(pre_x, axis=1) + bx_ref[...])
    z = -lam_ref[...]
    softplus = jnp.maximum(z, 0.0) + jnp.log1p(jnp.exp(-jnp.abs(z)))
    log_a = (-LRU_C) * r_gate * softplus
    a = jnp.exp(log_a)
    th = jnp.tanh(log_a)
    y2 = -2.0 * th / (1.0 - th)
    u = jnp.where(y2 > 0.0, y2 * lax.rsqrt(y2), 0.0) * (i_gate * conv)

    grp_shape = (tc // 8, 8, a.shape[1])
    a = a.reshape(grp_shape)
    u = u.reshape(grp_shape)
    sub = lax.broadcasted_iota(jnp.int32, grp_shape, 1)
    s = 1
    while s < 8:
        a_sh = pltpu.roll(a, s, axis=1)
        u_sh = pltpu.roll(u, s, axis=1)
        valid = sub >= s
        u = jnp.where(valid, a * u_sh + u, u)
        a = jnp.where(valid, a * a_sh, a)
        s *= 2
    h = h_ref[...]
    gy = _gelu_tanh(yr_ref[...])
    for grp in range(tc // 8):
        rows = slice(grp * 8, (grp + 1) * 8)
        hh = a[grp] * h + u[grp]
        h = hh[7:8, :]
        o_ref[rows, :] = (hh * gy[rows, :]).astype(o_ref.dtype)
    h_ref[...] = h


def _rglru(r, conv_w, conv_b, wa, wx, ba, bx, lam, batch, seq, tc=256):
    t = batch * seq
    nc = seq // tc
    w = LRU_WIDTH
    row = lambda v: v.reshape(1, w).astype(F32)
    full2 = lambda shape: pl.BlockSpec(shape, lambda b, c: (0,) * len(shape))
    return pl.pallas_call(
        _rglru_kernel,
        grid=(batch, nc),
        in_specs=[pl.BlockSpec((tc, w), lambda b, c: (b * nc + c, 0)),
                  pl.BlockSpec((tc, w), lambda b, c: (b * nc + c, 1)),
                  full2((CONV_WIDTH, w)), full2((1, w)),
                  full2((LRU_BLOCKS, LRU_BLOCK_DIM, LRU_BLOCK_DIM)),
                  full2((LRU_BLOCKS, LRU_BLOCK_DIM, LRU_BLOCK_DIM)),
                  full2((1, w)), full2((1, w)), full2((1, w))],
        out_specs=pl.BlockSpec((tc, w), lambda b, c: (b * nc + c, 0)),
        out_shape=jax.ShapeDtypeStruct((t, w), BF16),
        scratch_shapes=[pltpu.VMEM((8, w), F32), pltpu.VMEM((1, w), F32)],
        compiler_params=_params(("parallel", "arbitrary")),
        name="rglru",
    )(r, r, conv_w.astype(F32), row(conv_b), wa.astype(BF16), wx.astype(BF16),
      row(ba), row(bx), row(lam))


def _merge_kernel(oa_ref, or_ref, g0_ref, g1_ref, wb0_ref, wb1_ref, wo_ref, x_ref, gn_ref,
                  x1_ref, xn_ref):
    pb0 = jnp.dot(oa_ref[...], wb0_ref[...], preferred_element_type=F32)
    pb1 = jnp.dot(or_ref[...], wb1_ref[...], preferred_element_type=F32)
    merged = jax.nn.sigmoid(g0_ref[...]) * pb0 + jax.nn.sigmoid(g1_ref[...]) * pb1
    x1 = x_ref[...] + jnp.dot(merged.astype(BF16), wo_ref[...], preferred_element_type=F32)
    x1_ref[...] = x1
    xn_ref[...] = _rms(x1, gn_ref[...]).astype(xn_ref.dtype)


def _merge_out(o_att, o_rec, gl, wb0, wb1, w_out, x, g_ffn, tm=256):
    t, d = x.shape
    cw = o_att.shape[1]
    resident = lambda shape: pl.BlockSpec(shape, lambda i: (0, 0), pipeline_mode=pl.Buffered(1))
    return pl.pallas_call(
        _merge_kernel,
        grid=(t // tm,),
        in_specs=[pl.BlockSpec((tm, cw), lambda i: (i, 0)),
                  pl.BlockSpec((tm, cw), lambda i: (i, 0)),
                  pl.BlockSpec((tm, d), lambda i: (i, 0)),
                  pl.BlockSpec((tm, d), lambda i: (i, 1)),
                  resident((cw, d)), resident((cw, d)), resident((d, d)),
                  pl.BlockSpec((tm, d), lambda i: (i, 0)),
                  pl.BlockSpec((1, d), lambda i: (0, 0))],
        out_specs=[pl.BlockSpec((tm, d), lambda i: (i, 0)),
                   pl.BlockSpec((tm, d), lambda i: (i, 0))],
        out_shape=[jax.ShapeDtypeStruct((t, d), F32), jax.ShapeDtypeStruct((t, d), BF16)],
        compiler_params=_params(("parallel",)),
        name="merge_out",
    )(o_att, o_rec, gl, gl, wb0, wb1, w_out, x, g_ffn.reshape(1, d).astype(F32))


def _oddeven_merge(lo, hi, r):
    step = r * 2
    if step < hi - lo:
        yield from _oddeven_merge(lo, hi, step)
        yield from _oddeven_merge(lo + r, hi, step)
        yield from [(i, i + r) for i in range(lo + r, hi - r, step)]
    else:
        yield (lo, lo + r)


def _oddeven_merge_sort(lo, hi):
    if hi - lo >= 1:
        mid = lo + (hi - lo) // 2
        yield from _oddeven_merge_sort(lo, mid)
        yield from _oddeven_merge_sort(mid + 1, hi)
        yield from _oddeven_merge(lo, hi, 1)


def _peer_scores_kernel(q_ref, keys_ref, thr_ref, e1_ref, s2_ref, e2_ref, s_ref, vals_ref, cand_ref):
    nhc = keys_ref.shape[0]
    k = PEER_TOPK
    for hc in range(nhc):
        qs = q_ref[:, hc * PEER_NKEYS:(hc + 1) * PEER_NKEYS]
        s_ref[hc] = lax.dot_general(keys_ref[hc], qs, _NT_DIMS, preferred_element_type=F32)

    n_lvl = PEER_NKEYS // 8
    network = list(_oddeven_merge_sort(0, n_lvl - 1))

    def top_values(hc, carry):
        x = s_ref[hc]
        lvl = [x[8 * v:8 * v + 8, :] for v in range(n_lvl)]
        for a, b in network:
            lvl[a], lvl[b] = jnp.maximum(lvl[a], lvl[b]), jnp.minimum(lvl[a], lvl[b])
        for r in range(k + 1):
            m = jnp.max(lvl[0], axis=0, keepdims=True)
            vals_ref[hc, r:r + 1, :] = m
            hit = lvl[0] == m
            for v in range(min(k - r, n_lvl)):
                below = lvl[v + 1] if v + 1 < n_lvl else LOWEST
                lvl[v] = jnp.where(hit, below, lvl[v])
        return carry

    lax.fori_loop(0, nhc, top_values, 0)

    def head_stats(h, carry):
        v1 = vals_ref[2 * h, 0:k + 1, :]
        v2 = vals_ref[2 * h + 1, 0:k + 1, :]
        cand_ref[...] = jnp.full(cand_ref.shape, LOWEST, F32)
        off = 0
        for a in range(k + 1):
            nb = (k + 1) // (a + 1)
            cand_ref[off:off + nb, :] = v1[a:a + 1, :] + v2[0:nb, :]
            off += nb
        cand = cand_ref[...]
        x = cand
        kth = None
        for r in range(k):
            kth = jnp.max(x, axis=0, keepdims=True)
            x = jnp.where(x == kth, LOWEST, x)
        nxt = jnp.max(x, axis=0, keepdims=True)
        m1 = v1[0:1, :]
        m2 = v2[0:1, :]
        tau = 0.5 * (kth + nxt)
        z = jnp.sum(jnp.where(cand >= tau, jnp.exp(cand - (m1 + m2)), 0.0), axis=0, keepdims=True)
        s1 = s_ref[2 * h]
        s2 = s_ref[2 * h + 1]
        thr_ref[h] = tau - s1
        e1_ref[h] = jnp.exp(s1 - m1)
        e2 = jnp.exp(s2 - m2) * (0.5 / z)
        for c in range(s2_ref.shape[1]):
            s2_ref[h, c] = s2[:, c * 128:(c + 1) * 128]
            e2_ref[h, c] = e2[:, c * 128:(c + 1) * 128]
        return carry

    lax.fori_loop(0, nhc // 2, head_stats, 0)


def _peer_scores(q, keys, tm=256):
    t, _ = q.shape
    nhc = keys.shape[0]
    out_spec = pl.BlockSpec((PEER_HEADS, PEER_NKEYS, tm), lambda i: (0, 0, i))
    out_shape = jax.ShapeDtypeStruct((PEER_HEADS, PEER_NKEYS, t), F32)
    chunk_spec = pl.BlockSpec((PEER_HEADS, tm // 128, PEER_NKEYS, 128), lambda i: (0, i, 0, 0))
    chunk_shape = jax.ShapeDtypeStruct((PEER_HEADS, t // 128, PEER_NKEYS, 128), F32)
    n_cand = sum((PEER_TOPK + 1) // (a + 1) for a in range(PEER_TOPK + 1))
    n_cand = -(-n_cand // 8) * 8
    return pl.pallas_call(
        _peer_scores_kernel,
        grid=(t // tm,),
        in_specs=[pl.BlockSpec((tm, nhc * PEER_NKEYS), lambda i: (i, 0)),
                  pl.BlockSpec(keys.shape, lambda i: (0, 0, 0))],
        out_specs=[out_spec, out_spec, chunk_spec, chunk_spec],
        out_shape=[out_shape, out_shape, chunk_shape, chunk_shape],
        scratch_shapes=[pltpu.VMEM((nhc, PEER_NKEYS, tm), F32),
                        pltpu.VMEM((nhc, PEER_TOPK + 8, tm), F32),
                        pltpu.VMEM((n_cand, tm), F32)],
        compiler_params=_params(("parallel",)),
        name="peer_scores",
    )(q, keys)


def _transpose_kernel(x_ref, o_ref):
    o_ref[...] = x_ref[...].T.astype(o_ref.dtype)


def _transposed_tiles(table, tn):
    e, d = table.shape
    return pl.pallas_call(
        _transpose_kernel,
        grid=(e // tn,),
        in_specs=[pl.BlockSpec((tn, d), lambda i: (i, 0))],
        out_specs=pl.BlockSpec((None, d, tn), lambda i: (i, 0, 0)),
        out_shape=jax.ShapeDtypeStruct((e // tn, d, tn), BF16),
        compiler_params=_params(("parallel",)),
        name="expert_value_tiles",
    )(table)


def _peer_dense_kernel(xn_ref, thr_ref, e1_ref, s2_ref, e2_ref, u_ref, vt_ref, acc_ref,
                       xnt_ref, act_ref, p_ref):
    j = pl.program_id(1)
    lane_chunks, tn, _ = act_ref.shape
    rows = tn // PEER_NKEYS

    @pl.when(j == 0)
    def _():
        acc_ref[...] = jnp.zeros(acc_ref.shape, F32)
        xnt_ref[...] = xn_ref[...].astype(F32).T.astype(xnt_ref.dtype)

    act = jnp.dot(u_ref[...].astype(BF16), xnt_ref[...], preferred_element_type=F32)
    for lc in range(lane_chunks):
        act_ref[lc] = act[:, lc * 128:(lc + 1) * 128]

    def weigh(r, carry):
        i1 = j * rows + r
        roff = pl.multiple_of(r * PEER_NKEYS, PEER_NKEYS)
        thr_rows = [thr_ref[h, pl.ds(i1, 1), :] for h in range(PEER_HEADS)]
        e1_rows = [e1_ref[h, pl.ds(i1, 1), :] for h in range(PEER_HEADS)]
        for lc in range(lane_chunks):
            lanes = slice(lc * 128, (lc + 1) * 128)
            w = None
            for h in range(PEER_HEADS):
                w_h = jnp.where(s2_ref[h, lc] >= thr_rows[h][:, lanes],
                                e2_ref[h, lc] * e1_rows[h][:, lanes], 0.0)
                w = w_h if w is None else w + w_h
            a = act_ref[lc, pl.ds(roff, PEER_NKEYS), :]
            p_ref[pl.ds(roff, PEER_NKEYS), lanes] = (_gelu_tanh_x2(a) * w).astype(p_ref.dtype)
        return carry

    lax.fori_loop(0, rows, weigh, 0)
    res = jnp.dot(vt_ref[...], p_ref[...], preferred_element_type=F32)
    for lc in range(lane_chunks):
        acc_ref[lc] += res[:, lc * 128:(lc + 1) * 128]


def _peer_dense(xn, thr, e1, s2, e2, u_tab, vt_tiles, tm=512):
    t, d = xn.shape
    n_tiles, _, tn = vt_tiles.shape
    once = pl.Buffered(1)
    route_spec = pl.BlockSpec((PEER_HEADS, PEER_NKEYS, tm), lambda i, j: (0, 0, i), pipeline_mode=once)
    chunk_spec = pl.BlockSpec((PEER_HEADS, tm // 128, PEER_NKEYS, 128), lambda i, j: (0, i, 0, 0))
    return pl.pallas_call(
        _peer_dense_kernel,
        grid=(t // tm, n_tiles),
        in_specs=[pl.BlockSpec((tm, d), lambda i, j: (i, 0), pipeline_mode=once),
                  route_spec, route_spec, chunk_spec, chunk_spec,
                  pl.BlockSpec((tn, d), lambda i, j: (j, 0)),
                  pl.BlockSpec((None, d, tn), lambda i, j: (j, 0, 0))],
        out_specs=pl.BlockSpec((tm // 128, d, 128), lambda i, j: (i, 0, 0)),
        out_shape=jax.ShapeDtypeStruct((t // 128, d, 128), F32),
        scratch_shapes=[pltpu.VMEM((d, tm), BF16), pltpu.VMEM((tm // 128, tn, 128), F32),
                        pltpu.VMEM((tn, tm), BF16)],
        compiler_params=_params(("parallel", "arbitrary")),
        name="peer_dense",
    )(xn, thr, e1, s2, e2, u_tab, vt_tiles)


def _final_kernel(x1_ref, pt_ref, g_ref, y_ref):
    for c in range(pt_ref.shape[0]):
        rows = slice(c * 128, (c + 1) * 128)
        y_ref[rows, :] = _rms(x1_ref[rows, :] + pt_ref[c].T, g_ref[...])


def _residual_norm(x1, peer_chunks, g, tm=256):
    t, d = x1.shape
    return pl.pallas_call(
        _final_kernel,
        grid=(t // tm,),
        in_specs=[pl.BlockSpec((tm, d), lambda i: (i, 0)),
                  pl.BlockSpec((tm // 128, d, 128), lambda i: (i, 0, 0)),
                  pl.BlockSpec((1, d), lambda i: (0, 0))],
        out_specs=pl.BlockSpec((tm, d), lambda i: (i, 0)),
        out_shape=jax.ShapeDtypeStruct((t, d), F32),
        compiler_params=_params(("parallel",)),
        name="residual_norm",
    )(x1, peer_chunks, g.reshape(1, d).astype(F32))


def kernel(x, norm_mix_g, w_in, conv_w, conv_b, lru_wa, lru_ba, lru_wx, lru_bx, lru_lambda,
           w_branch, w_out, rel_bias, norm_ffn_g, peer_wq, peer_keys, peer_u, peer_v, norm_final_g):
    batch, seq, d = x.shape
    t = batch * seq
    assert w_in.shape[0] == 1, "single-layer trunk"
    xt = x.reshape(t, d)
    bias = _bias_tiles(rel_bias)
    a_w = ATTN_WIDTH
    w_l = w_in[0]
    h = _rmsnorm(xt, norm_mix_g[0], BF16)
    qk = _matmul(h, w_l, 0, 2 * a_w, BF16, "proj_qk")
    vt3 = _v_transposed(w_l[:, 2 * a_w:3 * a_w].T.astype(BF16), h)
    r = _matmul(h, w_l, 3 * a_w, 2 * LRU_WIDTH, F32, "proj_rec")
    gl = _matmul(h, w_l, 3 * a_w + 2 * LRU_WIDTH, 2 * d, F32, "proj_gate")
    o_att = _attention(qk, vt3, bias, batch, seq)
    o_rec = _rglru(r, conv_w[0], conv_b[0], lru_wa[0], lru_wx[0], lru_ba[0], lru_bx[0],
                   lru_lambda[0], batch, seq)
    x1, xn = _merge_out(o_att, o_rec, gl, w_branch[0, 0].astype(BF16),
                        w_branch[0, 1].astype(BF16), w_out[0].astype(BF16), xt, norm_ffn_g[0])
    q = _matmul(xn, peer_wq[0], 0, peer_wq.shape[2], BF16, "peer_query")
    keys = peer_keys[0].reshape(PEER_HEADS * 2, PEER_NKEYS, PEER_DKEY // 2).astype(BF16)
    thr, e1, s2, e2 = _peer_scores(q, keys)
    vt_tiles = _transposed_tiles(peer_v[0], PEER_EXPERT_TILE)
    peer_out = _peer_dense(xn, thr, e1, s2, e2, peer_u[0], vt_tiles)
    y = _residual_norm(x1, peer_out, norm_final_g)
    return y.reshape(batch, seq, d)
```

```python
import functools
import math

import jax
import jax.numpy as jnp
from jax import lax
from jax.experimental import pallas as pl
from jax.experimental.pallas import tpu as pltpu

ATTN_HEADS = 8
HEAD_DIM = 128
ATTN_WIDTH = ATTN_HEADS * HEAD_DIM
MOBA_BLOCK = 256
MOBA_TOPK = 3
LRU_WIDTH = 1024
LRU_BLOCKS = 8
LRU_BLOCK_DIM = LRU_WIDTH // LRU_BLOCKS
CONV_WIDTH = 4
LRU_C = 8.0
REL_BUCKETS = 32
REL_MAX_DIST = 128
PEER_HEADS = 8
PEER_NKEYS = 128
PEER_DKEY = 256
PEER_TOPK = 16
EPS = 1e-6
NEG = -1e30
LOWEST = -3.0e38
PEER_EXPERT_TILE = 1024

V7X_VMEM_LIMIT_BYTES = 56 * 1024 * 1024

F32 = jnp.float32
BF16 = jnp.bfloat16

_NT_DIMS = (((1,), (1,)), ((), ()))


def _params(semantics):
    return pltpu.CompilerParams(dimension_semantics=semantics,
                                vmem_limit_bytes=V7X_VMEM_LIMIT_BYTES)


def _gelu_tanh_x2(x):
    c = math.sqrt(2.0 / math.pi)
    return x * (1.0 + jnp.tanh(x * (c + (0.044715 * c) * (x * x))))


def _gelu_tanh(x):
    return 0.5 * _gelu_tanh_x2(x)


def _rms(x, g):
    ms = jnp.mean(x * x, axis=-1, keepdims=True)
    return x * lax.rsqrt(ms + EPS) * g


def _rmsnorm_kernel(x_ref, g_ref, o_ref):
    o_ref[...] = _rms(x_ref[...], g_ref[...]).astype(o_ref.dtype)


def _rmsnorm(x, g, out_dtype, tm=512):
    t, d = x.shape
    return pl.pallas_call(
        _rmsnorm_kernel,
        grid=(t // tm,),
        in_specs=[pl.BlockSpec((tm, d), lambda i: (i, 0)),
                  pl.BlockSpec((1, d), lambda i: (0, 0))],
        out_specs=pl.BlockSpec((tm, d), lambda i: (i, 0)),
        out_shape=jax.ShapeDtypeStruct((t, d), out_dtype),
        compiler_params=_params(("parallel",)),
        name="rmsnorm",
    )(x, g.reshape(1, d))


def _mm_kernel(a_ref, b_ref, o_ref, w_ref):
    @pl.when(pl.program_id(1) == 0)
    def _():
        w_ref[...] = b_ref[...].astype(w_ref.dtype)

    o_ref[...] = jnp.dot(a_ref[...], w_ref[...], preferred_element_type=F32).astype(o_ref.dtype)


def _matmul(a, b, col_start, n, out_dtype, name, tm=1024, tn=1024):
    m, k = a.shape
    assert col_start % tn == 0 and n % tn == 0
    col_blk = col_start // tn
    return pl.pallas_call(
        _mm_kernel,
        grid=(n // tn, m // tm),
        in_specs=[pl.BlockSpec((tm, k), lambda j, i: (i, 0)),
                  pl.BlockSpec((k, tn), lambda j, i: (0, col_blk + j))],
        out_specs=pl.BlockSpec((tm, tn), lambda j, i: (i, j)),
        out_shape=jax.ShapeDtypeStruct((m, n), out_dtype),
        scratch_shapes=[pltpu.VMEM((k, tn), BF16)],
        compiler_params=_params(("parallel", "arbitrary")),
        name=name,
    )(a, b)


def _vt_kernel(w_ref, h_ref, o_ref):
    res = lax.dot_general(w_ref[...], h_ref[...], _NT_DIMS, preferred_element_type=F32)
    for t in range(o_ref.shape[0]):
        o_ref[t] = res[:, t * MOBA_BLOCK:(t + 1) * MOBA_BLOCK].astype(o_ref.dtype)


def _v_transposed(w_t, h, tm=1024):
    c, k = w_t.shape
    t, _ = h.shape
    nb = tm // MOBA_BLOCK
    return pl.pallas_call(
        _vt_kernel,
        grid=(t // tm,),
        in_specs=[pl.BlockSpec((c, k), lambda i: (0, 0)),
                  pl.BlockSpec((tm, k), lambda i: (i, 0))],
        out_specs=pl.BlockSpec((nb, c, MOBA_BLOCK), lambda i: (i, 0, 0)),
        out_shape=jax.ShapeDtypeStruct((t // MOBA_BLOCK, c, MOBA_BLOCK), BF16),
        compiler_params=_params(("parallel",)),
        name="v_transposed",
    )(w_t, h)


def _rel_bucket(dist):
    n = jnp.maximum(dist, 0)
    max_exact = REL_BUCKETS // 2
    nf = jnp.maximum(n, 1).astype(F32)
    large = max_exact + (jnp.log(nf / max_exact) / math.log(REL_MAX_DIST / max_exact)
                         * (REL_BUCKETS - max_exact)).astype(jnp.int32)
    large = jnp.minimum(large, REL_BUCKETS - 1)
    return jnp.where(n < max_exact, n, large)


def _bias_kernel(relb_ref, bucket_ref, o_ref):
    h = pl.program_id(0)
    for t in range(3):
        bk = bucket_ref[t]
        acc = jnp.zeros(bk.shape, F32)
        for b in range(REL_BUCKETS):
            acc = jnp.where(bk == b, relb_ref[b, h], acc)
        if t == 0:
            key = lax.broadcasted_iota(jnp.int32, bk.shape, 0)
            qry = lax.broadcasted_iota(jnp.int32, bk.shape, 1)
            acc = jnp.where(key <= qry, acc, NEG)
        o_ref[0, t] = acc


def _bias_tiles(rel_bias):
    assert REL_MAX_DIST <= MOBA_BLOCK
    key = jnp.arange(MOBA_BLOCK, dtype=jnp.int32)[:, None]
    qry = jnp.arange(MOBA_BLOCK, dtype=jnp.int32)[None, :]
    buckets = jnp.stack([_rel_bucket(t * MOBA_BLOCK + qry - key) for t in range(3)])
    return pl.pallas_call(
        _bias_kernel,
        grid=(ATTN_HEADS,),
        in_specs=[pl.BlockSpec(memory_space=pltpu.SMEM),
                  pl.BlockSpec((3, MOBA_BLOCK, MOBA_BLOCK), lambda h: (0, 0, 0))],
        out_specs=pl.BlockSpec((1, 3, MOBA_BLOCK, MOBA_BLOCK), lambda h: (h, 0, 0, 0)),
        out_shape=jax.ShapeDtypeStruct((ATTN_HEADS, 3, MOBA_BLOCK, MOBA_BLOCK), F32),
        compiler_params=_params(("parallel",)),
        name="attn_bias_tiles",
    )(rel_bias.astype(F32), buckets)


ATTN_HEADS_PER_STEP = 8


def _attn_kernel(q_ref, k_ref, vt_ref, bias_ref, o_ref, kmean_ref, sel_ref, acc_ref, *, nb):
    i = pl.program_id(2)
    scale = HEAD_DIM ** -0.5
    heads = range(ATTN_HEADS_PER_STEP)
    hsl = [slice(g * HEAD_DIM, (g + 1) * HEAD_DIM) for g in heads]

    @pl.when(i == 0)
    def _():
        for g in heads:
            for n in range(nb):
                kb = k_ref[n * MOBA_BLOCK:(n + 1) * MOBA_BLOCK, hsl[g]].astype(F32)
                kmean_ref[g, n:n + 1, :] = jnp.mean(kb, axis=0, keepdims=True)

    qs = [q_ref[:, hsl[g]] for g in heads]
    for g in heads:
        gate = lax.dot_general(kmean_ref[g].astype(BF16), qs[g], _NT_DIMS,
                               preferred_element_type=F32)
        blk = lax.broadcasted_iota(jnp.int32, gate.shape, 0)
        rank = jnp.zeros(gate.shape, F32)
        for m in range(nb):
            gm = gate[m:m + 1, :]
            beats = jnp.where(gm > gate, 1.0, jnp.where(gm == gate, jnp.where(m < blk, 1.0, 0.0), 0.0))
            rank = rank + beats * jnp.where(m < i, 1.0, 0.0)
        sel_ref[g] = jnp.where(blk < i, jnp.where(rank < MOBA_TOPK, 1.0, 0.0), 0.0)

    def scores(g, j, tile):
        koff = pl.multiple_of(j * MOBA_BLOCK, MOBA_BLOCK)
        kj = k_ref[pl.ds(koff, MOBA_BLOCK), hsl[g]]
        s = lax.dot_general(kj, qs[g], _NT_DIMS, preferred_element_type=F32)
        return s * scale + bias_ref[g, tile]

    s0 = [scores(g, i, 0) for g in heads]
    m0 = [jnp.max(s0[g], axis=0, keepdims=True) for g in heads]
    p0 = [jnp.exp(s0[g] - m0[g]) for g in heads]
    l0 = [jnp.sum(p0[g], axis=0, keepdims=True) for g in heads]
    for g in heads:
        acc_ref[g] = jnp.dot(vt_ref[i, hsl[g], :], p0[g].astype(BF16), preferred_element_type=F32)

    def body(j, carry):
        ms, ls = carry
        tile = jnp.minimum(i - j, 2)
        ss = [scores(g, j, tile) for g in heads]
        ss = [jnp.where(sel_ref[g, pl.ds(j, 1), :] > 0.5, ss[g], NEG) for g in heads]
        new_m = [jnp.maximum(ms[g], jnp.max(ss[g], axis=0, keepdims=True)) for g in heads]
        alpha = [jnp.exp(ms[g] - new_m[g]) for g in heads]
        ps = [jnp.exp(ss[g] - new_m[g]) for g in heads]
        new_l = [alpha[g] * ls[g] + jnp.sum(ps[g], axis=0, keepdims=True) for g in heads]
        pv = [jnp.dot(vt_ref[j, hsl[g], :], ps[g].astype(BF16), preferred_element_type=F32) for g in heads]
        for g in heads:
            acc_ref[g] = alpha[g] * acc_ref[g] + pv[g]
        return tuple(new_m), tuple(new_l)

    _, l_f = lax.fori_loop(0, i, body, (tuple(m0), tuple(l0)))
    for g in heads:
        o_ref[:, hsl[g]] = (acc_ref[g] / l_f[g]).T.astype(o_ref.dtype)


def _attention(qk, vt3, bias, batch, seq):
    nb = seq // MOBA_BLOCK
    t = batch * seq
    gh = ATTN_HEADS_PER_STEP
    width = gh * HEAD_DIM
    groups = ATTN_HEADS // gh
    return pl.pallas_call(
        functools.partial(_attn_kernel, nb=nb),
        grid=(batch, groups, nb),
        in_specs=[
            pl.BlockSpec((MOBA_BLOCK, width), lambda b, h, i: (b * nb + i, h)),
            pl.BlockSpec((seq, width), lambda b, h, i: (b, groups + h)),
            pl.BlockSpec((nb, width, MOBA_BLOCK), lambda b, h, i: (b, h, 0)),
            pl.BlockSpec((gh, 3, MOBA_BLOCK, MOBA_BLOCK), lambda b, h, i: (h, 0, 0, 0)),
        ],
        out_specs=pl.BlockSpec((MOBA_BLOCK, width), lambda b, h, i: (b * nb + i, h)),
        out_shape=jax.ShapeDtypeStruct((t, ATTN_WIDTH), BF16),
        scratch_shapes=[pltpu.VMEM((gh, nb, HEAD_DIM), F32),
                        pltpu.VMEM((gh, nb, MOBA_BLOCK), F32),
                        pltpu.VMEM((gh, HEAD_DIM, MOBA_BLOCK), F32)],
        compiler_params=_params(("parallel", "parallel", "arbitrary")),
        name="moba_attention",
    )(qk, qk, vt3, bias)


def _rglru_kernel(xr_ref, yr_ref, cw_ref, cb_ref, wa_ref, wx_ref, ba_ref, bx_ref, lam_ref,
                  o_ref, tail_ref, h_ref):
    c = pl.program_id(1)
    tc = xr_ref.shape[0]

    @pl.when(c == 0)
    def _():
        tail_ref[...] = jnp.zeros(tail_ref.shape, F32)
        h_ref[...] = jnp.zeros(h_ref.shape, F32)

    x = xr_ref[...]
    grp_shape = (tc // 8, 8, x.shape[1])
    x3 = x.reshape(grp_shape)
    tail3 = tail_ref[...].reshape(1, 8, x.shape[1])
    sub = lax.broadcasted_iota(jnp.int32, grp_shape, 1)
    conv = cb_ref[...] + cw_ref[CONV_WIDTH - 1:CONV_WIDTH, :] * x3
    for s in range(1, CONV_WIDTH):
        k = CONV_WIDTH - 1 - s
        rolled = pltpu.roll(x3, s, axis=1)
        rolled_prev = jnp.concatenate([pltpu.roll(tail3, s, axis=1), rolled[:-1]], axis=0)
        conv = conv + cw_ref[k:k + 1, :] * jnp.where(sub >= s, rolled, rolled_prev)
    conv = conv.reshape(x.shape)
    tail_ref[...] = x[tc - 8:, :]

    xb16 = conv.astype(BF16)
    pre_a, pre_x = [], []
    for g in range(LRU_BLOCKS):
        xs = xb16[:, g * LRU_BLOCK_DIM:(g + 1) * LRU_BLOCK_DIM]
        pre_a.append(jnp.dot(xs, wa_ref[g], preferred_element_type=F32))
        pre_x.append(jnp.dot(xs, wx_ref[g], preferred_element_type=F32))
    r_gate = jax.nn.sigmoid(jnp.concatenate(pre_a, axis=1) + ba_ref[...])
    i_gate = jax.nn.sigmoid(jnp.concatenate(pre_x, axis=1) + bx_ref[...])
    z = -lam_ref[...]
    softplus = jnp.maximum(z, 0.0) + jnp.log1p(jnp.exp(-jnp.abs(z)))
    log_a = (-LRU_C) * r_gate * softplus
    a = jnp.exp(log_a)
    th = jnp.tanh(log_a)
    y2 = -2.0 * th / (1.0 - th)
    u = jnp.where(y2 > 0.0, y2 * lax.rsqrt(y2), 0.0) * (i_gate * conv)

    grp_shape = (tc // 8, 8, a.shape[1])
    a = a.reshape(grp_shape)
    u = u.reshape(grp_shape)
    sub = lax.broadcasted_iota(jnp.int32, grp_shape, 1)
    s = 1
    while s < 8:
        a_sh = pltpu.roll(a, s, axis=1)
        u_sh = pltpu.roll(u, s, axis=1)
        valid = sub >= s
        u = jnp.where(valid, a * u_sh + u, u)
        a = jnp.where(valid, a * a_sh, a)
        s *= 2
    h = h_ref[...]
    gy = _gelu_tanh(yr_ref[...])
    for grp in range(tc // 8):
        rows = slice(grp * 8, (grp + 1) * 8)
        hh = a[grp] * h + u[grp]
        h = hh[7:8, :]
        o_ref[rows, :] = (hh * gy[rows, :]).astype(o_ref.dtype)
    h_ref[...] = h


def _rglru(r, conv_w, conv_b, wa, wx, ba, bx, lam, batch, seq, tc=256):
    t = batch * seq
    nc = seq // tc
    w = LRU_WIDTH
    row = lambda v: v.reshape(1, w).astype(F32)
    full2 = lambda shape: pl.BlockSpec(shape, lambda b, c: (0,) * len(shape))
    return pl.pallas_call(
        _rglru_kernel,
        grid=(batch, nc),
        in_specs=[pl.BlockSpec((tc, w), lambda b, c: (b * nc + c, 0)),
                  pl.BlockSpec((tc, w), lambda b, c: (b * nc + c, 1)),
                  full2((CONV_WIDTH, w)), full2((1, w)),
                  full2((LRU_BLOCKS, LRU_BLOCK_DIM, LRU_BLOCK_DIM)),
                  full2((LRU_BLOCKS, LRU_BLOCK_DIM, LRU_BLOCK_DIM)),
                  full2((1, w)), full2((1, w)), full2((1, w))],
        out_specs=pl.BlockSpec((tc, w), lambda b, c: (b * nc + c, 0)),
        out_shape=jax.ShapeDtypeStruct((t, w), BF16),
        scratch_shapes=[pltpu.VMEM((8, w), F32), pltpu.VMEM((1, w), F32)],
        compiler_params=_params(("parallel", "arbitrary")),
        name="rglru",
    )(r, r, conv_w.astype(F32), row(conv_b), wa.astype(BF16), wx.astype(BF16),
      row(ba), row(bx), row(lam))


def _merge_kernel(oa_ref, or_ref, g0_ref, g1_ref, wb0_ref, wb1_ref, wo_ref, x_ref, gn_ref,
                  x1_ref, xn_ref):
    pb0 = jnp.dot(oa_ref[...], wb0_ref[...], preferred_element_type=F32)
    pb1 = jnp.dot(or_ref[...], wb1_ref[...], preferred_element_type=F32)
    merged = jax.nn.sigmoid(g0_ref[...]) * pb0 + jax.nn.sigmoid(g1_ref[...]) * pb1
    x1 = x_ref[...] + jnp.dot(merged.astype(BF16), wo_ref[...], preferred_element_type=F32)
    x1_ref[...] = x1
    xn_ref[...] = _rms(x1, gn_ref[...]).astype(xn_ref.dtype)


def _merge_out(o_att, o_rec, gl, wb0, wb1, w_out, x, g_ffn, tm=256):
    t, d = x.shape
    cw = o_att.shape[1]
    resident = lambda shape: pl.BlockSpec(shape, lambda i: (0, 0), pipeline_mode=pl.Buffered(1))
    return pl.pallas_call(
        _merge_kernel,
        grid=(t // tm,),
        in_specs=[pl.BlockSpec((tm, cw), lambda i: (i, 0)),
                  pl.BlockSpec((tm, cw), lambda i: (i, 0)),
                  pl.BlockSpec((tm, d), lambda i: (i, 0)),
                  pl.BlockSpec((tm, d), lambda i: (i, 1)),
                  resident((cw, d)), resident((cw, d)), resident((d, d)),
                  pl.BlockSpec((tm, d), lambda i: (i, 0)),
                  pl.BlockSpec((1, d), lambda i: (0, 0))],
        out_specs=[pl.BlockSpec((tm, d), lambda i: (i, 0)),
                   pl.BlockSpec((tm, d), lambda i: (i, 0))],
        out_shape=[jax.ShapeDtypeStruct((t, d), F32), jax.ShapeDtypeStruct((t, d), BF16)],
        compiler_params=_params(("parallel",)),
        name="merge_out",
    )(o_att, o_rec, gl, gl, wb0, wb1, w_out, x, g_ffn.reshape(1, d).astype(F32))


def _oddeven_merge(lo, hi, r):
    step = r * 2
    if step < hi - lo:
        yield from _oddeven_merge(lo, hi, step)
        yield from _oddeven_merge(lo + r, hi, step)
        yield from [(i, i + r) for i in range(lo + r, hi - r, step)]
    else:
        yield (lo, lo + r)


def _oddeven_merge_sort(lo, hi):
    if hi - lo >= 1:
        mid = lo + (hi - lo) // 2
        yield from _oddeven_merge_sort(lo, mid)
        yield from _oddeven_merge_sort(mid + 1, hi)
        yield from _oddeven_merge(lo, hi, 1)


def _peer_scores_kernel(q_ref, keys_ref, thr_ref, e1_ref, s2_ref, e2_ref, s_ref, vals_ref, cand_ref):
    nhc = keys_ref.shape[0]
    k = PEER_TOPK
    for hc in range(nhc):
        qs = q_ref[:, hc * PEER_NKEYS:(hc + 1) * PEER_NKEYS]
        s_ref[hc] = lax.dot_general(keys_ref[hc], qs, _NT_DIMS, preferred_element_type=F32)

    n_lvl = PEER_NKEYS // 8
    network = list(_oddeven_merge_sort(0, n_lvl - 1))

    def top_values(hc, carry):
        x = s_ref[hc]
        lvl = [x[8 * v:8 * v + 8, :] for v in range(n_lvl)]
        for a, b in network:
            lvl[a], lvl[b] = jnp.maximum(lvl[a], lvl[b]), jnp.minimum(lvl[a], lvl[b])
        for r in range(k + 1):
            m = jnp.max(lvl[0], axis=0, keepdims=True)
            vals_ref[hc, r:r + 1, :] = m
            hit = lvl[0] == m
            for v in range(min(k - r, n_lvl)):
                below = lvl[v + 1] if v + 1 < n_lvl else LOWEST
                lvl[v] = jnp.where(hit, below, lvl[v])
        return carry

    lax.fori_loop(0, nhc, top_values, 0)

    def head_stats(h, carry):
        v1 = vals_ref[2 * h, 0:k + 1, :]
        v2 = vals_ref[2 * h + 1, 0:k + 1, :]
        cand_ref[...] = jnp.full(cand_ref.shape, LOWEST, F32)
        off = 0
        for a in range(k + 1):
            nb = (k + 1) // (a + 1)
            cand_ref[off:off + nb, :] = v1[a:a + 1, :] + v2[0:nb, :]
            off += nb
        cand = cand_ref[...]
        x = cand
        kth = None
        for r in range(k):
            kth = jnp.max(x, axis=0, keepdims=True)
            x = jnp.where(x == kth, LOWEST, x)
        nxt = jnp.max(x, axis=0, keepdims=True)
        m1 = v1[0:1, :]
        m2 = v2[0:1, :]
        tau = 0.5 * (kth + nxt)
        z = jnp.sum(jnp.where(cand >= tau, jnp.exp(cand - (m1 + m2)), 0.0), axis=0, keepdims=True)
        s1 = s_ref[2 * h]
        s2 = s_ref[2 * h + 1]
        thr_ref[h] = tau - s1
        e1_ref[h] = jnp.exp(s1 - m1)
        e2 = jnp.exp(s2 - m2) * (0.5 / z)
        for c in range(s2_ref.shape[1]):
            s2_ref[h, c] = s2[:, c * 128:(c + 1) * 128]
            e2_ref[h, c] = e2[:, c * 128:(c + 1) * 128]
        return carry

    lax.fori_loop(0, nhc // 2, head_stats, 0)


def _peer_scores(q, keys, tm=256):
    t, _ = q.shape
    nhc = keys.shape[0]
    out_spec = pl.BlockSpec((PEER_HEADS, PEER_NKEYS, tm), lambda i: (0, 0, i))
    out_shape = jax.ShapeDtypeStruct((PEER_HEADS, PEER_NKEYS, t), F32)
    chunk_spec = pl.BlockSpec((PEER_HEADS, tm // 128, PEER_NKEYS, 128), lambda i: (0, i, 0, 0))
    chunk_shape = jax.ShapeDtypeStruct((PEER_HEADS, t // 128, PEER_NKEYS, 128), F32)
    n_cand = sum((PEER_TOPK + 1) // (a + 1) for a in range(PEER_TOPK + 1))
    n_cand = -(-n_cand // 8) * 8
    return pl.pallas_call(
        _peer_scores_kernel,
        grid=(t // tm,),
        in_specs=[pl.BlockSpec((tm, nhc * PEER_NKEYS), lambda i: (i, 0)),
                  pl.BlockSpec(keys.shape, lambda i: (0, 0, 0))],
        out_specs=[out_spec, out_spec, chunk_spec, chunk_spec],
        out_shape=[out_shape, out_shape, chunk_shape, chunk_shape],
        scratch_shapes=[pltpu.VMEM((nhc, PEER_NKEYS, tm), F32),
                        pltpu.VMEM((nhc, PEER_TOPK + 8, tm), F32),
                        pltpu.VMEM((n_cand, tm), F32)],
        compiler_params=_params(("parallel",)),
        name="peer_scores",
    )(q, keys)


def _transpose_kernel(x_ref, o_ref):
    o_ref[...] = x_ref[...].T.astype(o_ref.dtype)


def _transposed_tiles(table, tn):
    e, d = table.shape
    return pl.pallas_call(
        _transpose_kernel,
        grid=(e // tn,),
        in_specs=[pl.BlockSpec((tn, d), lambda i: (i, 0))],
        out_specs=pl.BlockSpec((None, d, tn), lambda i: (i, 0, 0)),
        out_shape=jax.ShapeDtypeStruct((e // tn, d, tn), BF16),
        compiler_params=_params(("parallel",)),
        name="expert_value_tiles",
    )(table)


def _peer_dense_kernel(xn_ref, thr_ref, e1_ref, s2_ref, e2_ref, u_ref, vt_ref, acc_ref,
                       xnt_ref, act_ref, p_ref):
    j = pl.program_id(1)
    lane_chunks, tn, _ = act_ref.shape
    rows = tn // PEER_NKEYS

    @pl.when(j == 0)
    def _():
        acc_ref[...] = jnp.zeros(acc_ref.shape, F32)
        xnt_ref[...] = xn_ref[...].astype(F32).T.astype(xnt_ref.dtype)

    act = jnp.dot(u_ref[...], xnt_ref[...], preferred_element_type=F32)
    for lc in range(lane_chunks):
        act_ref[lc] = act[:, lc * 128:(lc + 1) * 128]

    def weigh(r, carry):
        i1 = j * rows + r
        roff = pl.multiple_of(r * PEER_NKEYS, PEER_NKEYS)
        thr_rows = [thr_ref[h, pl.ds(i1, 1), :] for h in range(PEER_HEADS)]
        e1_rows = [e1_ref[h, pl.ds(i1, 1), :] for h in range(PEER_HEADS)]
        for lc in range(lane_chunks):
            lanes = slice(lc * 128, (lc + 1) * 128)
            w = None
            for h in range(PEER_HEADS):
                w_h = jnp.where(s2_ref[h, lc] >= thr_rows[h][:, lanes],
                                e2_ref[h, lc] * e1_rows[h][:, lanes], 0.0)
                w = w_h if w is None else w + w_h
            a = act_ref[lc, pl.ds(roff, PEER_NKEYS), :]
            p_ref[pl.ds(roff, PEER_NKEYS), lanes] = (_gelu_tanh_x2(a) * w).astype(p_ref.dtype)
        return carry

    lax.fori_loop(0, rows, weigh, 0)
    res = jnp.dot(vt_ref[...], p_ref[...], preferred_element_type=F32)
    for lc in range(lane_chunks):
        acc_ref[lc] += res[:, lc * 128:(lc + 1) * 128]


def _peer_dense(xn, thr, e1, s2, e2, u_bf, vt_tiles, tm=512):
    t, d = xn.shape
    n_tiles, _, tn = vt_tiles.shape
    route_spec = pl.BlockSpec((PEER_HEADS, PEER_NKEYS, tm), lambda i, j: (0, 0, i))
    chunk_spec = pl.BlockSpec((PEER_HEADS, tm // 128, PEER_NKEYS, 128), lambda i, j: (0, i, 0, 0))
    return pl.pallas_call(
        _peer_dense_kernel,
        grid=(t // tm, n_tiles),
        in_specs=[pl.BlockSpec((tm, d), lambda i, j: (i, 0)),
                  route_spec, route_spec, chunk_spec, chunk_spec,
                  pl.BlockSpec((tn, d), lambda i, j: (j, 0)),
                  pl.BlockSpec((None, d, tn), lambda i, j: (j, 0, 0))],
        out_specs=pl.BlockSpec((tm // 128, d, 128), lambda i, j: (i, 0, 0)),
        out_shape=jax.ShapeDtypeStruct((t // 128, d, 128), F32),
        scratch_shapes=[pltpu.VMEM((d, tm), BF16), pltpu.VMEM((tm // 128, tn, 128), F32),
                        pltpu.VMEM((tn, tm), BF16)],
        compiler_params=_params(("parallel", "arbitrary")),
        name="peer_dense",
    )(xn, thr, e1, s2, e2, u_bf, vt_tiles)


def _final_kernel(x1_ref, pt_ref, g_ref, y_ref):
    for c in range(pt_ref.shape[0]):
        rows = slice(c * 128, (c + 1) * 128)
        y_ref[rows, :] = _rms(x1_ref[rows, :] + pt_ref[c].T, g_ref[...])


def _residual_norm(x1, peer_chunks, g, tm=256):
    t, d = x1.shape
    return pl.pallas_call(
        _final_kernel,
        grid=(t // tm,),
        in_specs=[pl.BlockSpec((tm, d), lambda i: (i, 0)),
                  pl.BlockSpec((tm // 128, d, 128), lambda i: (i, 0, 0)),
                  pl.BlockSpec((1, d), lambda i: (0, 0))],
        out_specs=pl.BlockSpec((tm, d), lambda i: (i, 0)),
        out_shape=jax.ShapeDtypeStruct((t, d), F32),
        compiler_params=_params(("parallel",)),
        name="residual_norm",
    )(x1, peer_chunks, g.reshape(1, d).astype(F32))


def kernel(x, norm_mix_g, w_in, conv_w, conv_b, lru_wa, lru_ba, lru_wx, lru_bx, lru_lambda,
           w_branch, w_out, rel_bias, norm_ffn_g, peer_wq, peer_keys, peer_u, peer_v, norm_final_g):
    batch, seq, d = x.shape
    t = batch * seq
    assert w_in.shape[0] == 1, "single-layer trunk"
    xt = x.reshape(t, d)
    bias = _bias_tiles(rel_bias)
    a_w = ATTN_WIDTH
    w_l = w_in[0]
    h = _rmsnorm(xt, norm_mix_g[0], BF16)
    qk = _matmul(h, w_l, 0, 2 * a_w, BF16, "proj_qk")
    vt3 = _v_transposed(w_l[:, 2 * a_w:3 * a_w].T.astype(BF16), h)
    r = _matmul(h, w_l, 3 * a_w, 2 * LRU_WIDTH, F32, "proj_rec")
    gl = _matmul(h, w_l, 3 * a_w + 2 * LRU_WIDTH, 2 * d, F32, "proj_gate")
    o_att = _attention(qk, vt3, bias, batch, seq)
    o_rec = _rglru(r, conv_w[0], conv_b[0], lru_wa[0], lru_wx[0], lru_ba[0], lru_bx[0],
                   lru_lambda[0], batch, seq)
    x1, xn = _merge_out(o_att, o_rec, gl, w_branch[0, 0].astype(BF16),
                        w_branch[0, 1].astype(BF16), w_out[0].astype(BF16), xt, norm_ffn_g[0])
    q = _matmul(xn, peer_wq[0], 0, peer_wq.shape[2], BF16, "peer_query")
    keys = peer_keys[0].reshape(PEER_HEADS * 2, PEER_NKEYS, PEER_DKEY // 2).astype(BF16)
    thr, e1, s2, e2 = _peer_scores(q, keys)
    vt_tiles = _transposed_tiles(peer_v[0], PEER_EXPERT_TILE)
    peer_out = _peer_dense(xn, thr, e1, s2, e2, peer_u[0].astype(BF16), vt_tiles)
    y = _residual_norm(x1, peer_out, norm_final_g)
    return y.reshape(batch, seq, d)
```

```python
import functools
import math

import jax
import jax.numpy as jnp
from jax import lax
from jax.experimental import pallas as pl
from jax.experimental.pallas import tpu as pltpu

ATTN_HEADS = 8
HEAD_DIM = 128
ATTN_WIDTH = ATTN_HEADS * HEAD_DIM
MOBA_BLOCK = 256
MOBA_TOPK = 3
LRU_WIDTH = 1024
LRU_BLOCKS = 8
LRU_BLOCK_DIM = LRU_WIDTH // LRU_BLOCKS
CONV_WIDTH = 4
LRU_C = 8.0
REL_BUCKETS = 32
REL_MAX_DIST = 128
PEER_HEADS = 8
PEER_NKEYS = 128
PEER_DKEY = 256
PEER_TOPK = 16
EPS = 1e-6
NEG = -1e30
LOWEST = -3.0e38
PEER_EXPERT_TILE = 1024

V7X_VMEM_LIMIT_BYTES = 56 * 1024 * 1024

F32 = jnp.float32
BF16 = jnp.bfloat16

_NT_DIMS = (((1,), (1,)), ((), ()))


def _params(semantics):
    return pltpu.CompilerParams(dimension_semantics=semantics,
                                vmem_limit_bytes=V7X_VMEM_LIMIT_BYTES)


def _gelu_tanh_x2(x):
    c = math.sqrt(2.0 / math.pi)
    return x * (1.0 + jnp.tanh(x * (c + (0.044715 * c) * (x * x))))


def _gelu_tanh(x):
    return 0.5 * _gelu_tanh_x2(x)


def _rms(x, g):
    ms = jnp.mean(x * x, axis=-1, keepdims=True)
    return x * lax.rsqrt(ms + EPS) * g


def _rmsnorm_kernel(x_ref, g_ref, o_ref):
    o_ref[...] = _rms(x_ref[...], g_ref[...]).astype(o_ref.dtype)


def _rmsnorm(x, g, out_dtype, tm=512):
    t, d = x.shape
    return pl.pallas_call(
        _rmsnorm_kernel,
        grid=(t // tm,),
        in_specs=[pl.BlockSpec((tm, d), lambda i: (i, 0)),
                  pl.BlockSpec((1, d), lambda i: (0, 0))],
        out_specs=pl.BlockSpec((tm, d), lambda i: (i, 0)),
        out_shape=jax.ShapeDtypeStruct((t, d), out_dtype),
        compiler_params=_params(("parallel",)),
        name="rmsnorm",
    )(x, g.reshape(1, d))


def _mm_kernel(a_ref, b_ref, o_ref, w_ref, *, sigmoid):
    @pl.when(pl.program_id(1) == 0)
    def _():
        w_ref[...] = b_ref[...].astype(w_ref.dtype)

    acc = jnp.dot(a_ref[...], w_ref[...], preferred_element_type=F32)
    if sigmoid:
        acc = jax.nn.sigmoid(acc)
    o_ref[...] = acc.astype(o_ref.dtype)


def _matmul(a, b, col_start, n, out_dtype, name, sigmoid=False, tm=1024, tn=1024):
    m, k = a.shape
    assert col_start % tn == 0 and n % tn == 0
    col_blk = col_start // tn
    return pl.pallas_call(
        functools.partial(_mm_kernel, sigmoid=sigmoid),
        grid=(n // tn, m // tm),
        in_specs=[pl.BlockSpec((tm, k), lambda j, i: (i, 0)),
                  pl.BlockSpec((k, tn), lambda j, i: (0, col_blk + j))],
        out_specs=pl.BlockSpec((tm, tn), lambda j, i: (i, j)),
        out_shape=jax.ShapeDtypeStruct((m, n), out_dtype),
        scratch_shapes=[pltpu.VMEM((k, tn), BF16)],
        compiler_params=_params(("parallel", "arbitrary")),
        name=name,
    )(a, b)


def _vt_kernel(w_ref, h_ref, o_ref):
    res = lax.dot_general(w_ref[...], h_ref[...], _NT_DIMS, preferred_element_type=F32)
    for t in range(o_ref.shape[0]):
        o_ref[t] = res[:, t * MOBA_BLOCK:(t + 1) * MOBA_BLOCK].astype(o_ref.dtype)


def _v_transposed(w_t, h, tm=1024):
    c, k = w_t.shape
    t, _ = h.shape
    nb = tm // MOBA_BLOCK
    return pl.pallas_call(
        _vt_kernel,
        grid=(t // tm,),
        in_specs=[pl.BlockSpec((c, k), lambda i: (0, 0)),
                  pl.BlockSpec((tm, k), lambda i: (i, 0))],
        out_specs=pl.BlockSpec((nb, c, MOBA_BLOCK), lambda i: (i, 0, 0)),
        out_shape=jax.ShapeDtypeStruct((t // MOBA_BLOCK, c, MOBA_BLOCK), BF16),
        compiler_params=_params(("parallel",)),
        name="v_transposed",
    )(w_t, h)


def _rel_bucket(dist):
    n = jnp.maximum(dist, 0)
    max_exact = REL_BUCKETS // 2
    nf = jnp.maximum(n, 1).astype(F32)
    large = max_exact + (jnp.log(nf / max_exact) / math.log(REL_MAX_DIST / max_exact)
                         * (REL_BUCKETS - max_exact)).astype(jnp.int32)
    large = jnp.minimum(large, REL_BUCKETS - 1)
    return jnp.where(n < max_exact, n, large)


def _bias_kernel(relb_ref, bucket_ref, o_ref):
    h = pl.program_id(0)
    for t in range(3):
        bk = bucket_ref[t]
        acc = jnp.zeros(bk.shape, F32)
        for b in range(REL_BUCKETS):
            acc = jnp.where(bk == b, relb_ref[b, h], acc)
        if t == 0:
            key = lax.broadcasted_iota(jnp.int32, bk.shape, 0)
            qry = lax.broadcasted_iota(jnp.int32, bk.shape, 1)
            acc = jnp.where(key <= qry, acc, NEG)
        o_ref[0, t] = acc


def _bias_tiles(rel_bias):
    assert REL_MAX_DIST <= MOBA_BLOCK
    key = jnp.arange(MOBA_BLOCK, dtype=jnp.int32)[:, None]
    qry = jnp.arange(MOBA_BLOCK, dtype=jnp.int32)[None, :]
    buckets = jnp.stack([_rel_bucket(t * MOBA_BLOCK + qry - key) for t in range(3)])
    return pl.pallas_call(
        _bias_kernel,
        grid=(ATTN_HEADS,),
        in_specs=[pl.BlockSpec(memory_space=pltpu.SMEM),
                  pl.BlockSpec((3, MOBA_BLOCK, MOBA_BLOCK), lambda h: (0, 0, 0))],
        out_specs=pl.BlockSpec((1, 3, MOBA_BLOCK, MOBA_BLOCK), lambda h: (h, 0, 0, 0)),
        out_shape=jax.ShapeDtypeStruct((ATTN_HEADS, 3, MOBA_BLOCK, MOBA_BLOCK), F32),
        compiler_params=_params(("parallel",)),
        name="attn_bias_tiles",
    )(rel_bias.astype(F32), buckets)


ATTN_HEADS_PER_STEP = 8


def _attn_kernel(q_ref, k_ref, vt_ref, bias_ref, o_ref, kmean_ref, sel_ref, acc_ref, *, nb):
    i = pl.program_id(2)
    scale = HEAD_DIM ** -0.5
    heads = range(ATTN_HEADS_PER_STEP)
    hsl = [slice(g * HEAD_DIM, (g + 1) * HEAD_DIM) for g in heads]

    @pl.when(i == 0)
    def _():
        for g in heads:
            for n in range(nb):
                kb = k_ref[n * MOBA_BLOCK:(n + 1) * MOBA_BLOCK, hsl[g]].astype(F32)
                kmean_ref[g, n:n + 1, :] = jnp.mean(kb, axis=0, keepdims=True)

    qs = [q_ref[:, hsl[g]] for g in heads]
    for g in heads:
        gate = lax.dot_general(kmean_ref[g].astype(BF16), qs[g], _NT_DIMS,
                               preferred_element_type=F32)
        blk = lax.broadcasted_iota(jnp.int32, gate.shape, 0)
        rank = jnp.zeros(gate.shape, F32)
        for m in range(nb):
            gm = gate[m:m + 1, :]
            beats = jnp.where(gm > gate, 1.0, jnp.where(gm == gate, jnp.where(m < blk, 1.0, 0.0), 0.0))
            rank = rank + beats * jnp.where(m < i, 1.0, 0.0)
        sel_ref[g] = jnp.where(blk < i, jnp.where(rank < MOBA_TOPK, 1.0, 0.0), 0.0)

    def scores(g, j, tile):
        koff = pl.multiple_of(j * MOBA_BLOCK, MOBA_BLOCK)
        kj = k_ref[pl.ds(koff, MOBA_BLOCK), hsl[g]]
        s = lax.dot_general(kj, qs[g], _NT_DIMS, preferred_element_type=F32)
        return s * scale + bias_ref[g, tile]

    s0 = [scores(g, i, 0) for g in heads]
    m0 = [jnp.max(s0[g], axis=0, keepdims=True) for g in heads]
    p0 = [jnp.exp(s0[g] - m0[g]) for g in heads]
    l0 = [jnp.sum(p0[g], axis=0, keepdims=True) for g in heads]
    for g in heads:
        acc_ref[g] = jnp.dot(vt_ref[i, hsl[g], :], p0[g].astype(BF16), preferred_element_type=F32)

    def body(j, carry):
        ms, ls = carry
        tile = jnp.minimum(i - j, 2)
        ss = [scores(g, j, tile) for g in heads]
        ss = [jnp.where(sel_ref[g, pl.ds(j, 1), :] > 0.5, ss[g], NEG) for g in heads]
        new_m = [jnp.maximum(ms[g], jnp.max(ss[g], axis=0, keepdims=True)) for g in heads]
        alpha = [jnp.exp(ms[g] - new_m[g]) for g in heads]
        ps = [jnp.exp(ss[g] - new_m[g]) for g in heads]
        new_l = [alpha[g] * ls[g] + jnp.sum(ps[g], axis=0, keepdims=True) for g in heads]
        pv = [jnp.dot(vt_ref[j, hsl[g], :], ps[g].astype(BF16), preferred_element_type=F32) for g in heads]
        for g in heads:
            acc_ref[g] = alpha[g] * acc_ref[g] + pv[g]
        return tuple(new_m), tuple(new_l)

    _, l_f = lax.fori_loop(0, i, body, (tuple(m0), tuple(l0)))
    for g in heads:
        o_ref[:, hsl[g]] = (acc_ref[g] / l_f[g]).T.astype(o_ref.dtype)


def _attention(qk, vt3, bias, batch, seq):
    nb = seq // MOBA_BLOCK
    t = batch * seq
    gh = ATTN_HEADS_PER_STEP
    width = gh * HEAD_DIM
    groups = ATTN_HEADS // gh
    return pl.pallas_call(
        functools.partial(_attn_kernel, nb=nb),
        grid=(batch, groups, nb),
        in_specs=[
            pl.BlockSpec((MOBA_BLOCK, width), lambda b, h, i: (b * nb + i, h)),
            pl.BlockSpec((seq, width), lambda b, h, i: (b, groups + h)),
            pl.BlockSpec((nb, width, MOBA_BLOCK), lambda b, h, i: (b, h, 0)),
            pl.BlockSpec((gh, 3, MOBA_BLOCK, MOBA_BLOCK), lambda b, h, i: (h, 0, 0, 0)),
        ],
        out_specs=pl.BlockSpec((MOBA_BLOCK, width), lambda b, h, i: (b * nb + i, h)),
        out_shape=jax.ShapeDtypeStruct((t, ATTN_WIDTH), BF16),
        scratch_shapes=[pltpu.VMEM((gh, nb, HEAD_DIM), F32),
                        pltpu.VMEM((gh, nb, MOBA_BLOCK), F32),
                        pltpu.VMEM((gh, HEAD_DIM, MOBA_BLOCK), F32)],
        compiler_params=_params(("parallel", "parallel", "arbitrary")),
        name="moba_attention",
    )(qk, qk, vt3, bias)


def _rglru_kernel(xr_ref, yr_ref, cw_ref, cb_ref, wa_ref, wx_ref, ba_ref, bx_ref, lam_ref,
                  o_ref, tail_ref, h_ref):
    c = pl.program_id(1)
    tc = xr_ref.shape[0]

    @pl.when(c == 0)
    def _():
        tail_ref[...] = jnp.zeros(tail_ref.shape, F32)
        h_ref[...] = jnp.zeros(h_ref.shape, F32)

    x = xr_ref[...]
    grp_shape = (tc // 8, 8, x.shape[1])
    x3 = x.reshape(grp_shape)
    tail3 = tail_ref[...].reshape(1, 8, x.shape[1])
    sub = lax.broadcasted_iota(jnp.int32, grp_shape, 1)
    conv = cb_ref[...] + cw_ref[CONV_WIDTH - 1:CONV_WIDTH, :] * x3
    for s in range(1, CONV_WIDTH):
        k = CONV_WIDTH - 1 - s
        rolled = pltpu.roll(x3, s, axis=1)
        rolled_prev = jnp.concatenate([pltpu.roll(tail3, s, axis=1), rolled[:-1]], axis=0)
        conv = conv + cw_ref[k:k + 1, :] * jnp.where(sub >= s, rolled, rolled_prev)
    conv = conv.reshape(x.shape)
    tail_ref[...] = x[tc - 8:, :]

    xb16 = conv.astype(BF16)
    pre_a, pre_x = [], []
    for g in range(LRU_BLOCKS):
        xs = xb16[:, g * LRU_BLOCK_DIM:(g + 1) * LRU_BLOCK_DIM]
        pre_a.append(jnp.dot(xs, wa_ref[g], preferred_element_type=F32))
        pre_x.append(jnp.dot(xs, wx_ref[g], preferred_element_type=F32))
    r_gate = jax.nn.sigmoid(jnp.concatenate(pre_a, axis=1) + ba_ref[...])
    i_gate = jax.nn.sigmoid(jnp.concatenate(pre_x, axis=1) + bx_ref[...])
    z = -lam_ref[...]
    softplus = jnp.maximum(z, 0.0) + jnp.log1p(jnp.exp(-jnp.abs(z)))
    log_a = (-LRU_C) * r_gate * softplus
    a = jnp.exp(log_a)
    th = jnp.tanh(log_a)
    y2 = -2.0 * th / (1.0 - th)
    u = jnp.where(y2 > 0.0, y2 * lax.rsqrt(y2), 0.0) * (i_gate * conv)

    grp_shape = (tc // 8, 8, a.shape[1])
    a = a.reshape(grp_shape)
    u = u.reshape(grp_shape)
    sub = lax.broadcasted_iota(jnp.int32, grp_shape, 1)
    s = 1
    while s < 8:
        a_sh = pltpu.roll(a, s, axis=1)
        u_sh = pltpu.roll(u, s, axis=1)
        valid = sub >= s
        u = jnp.where(valid, a * u_sh + u, u)
        a = jnp.where(valid, a * a_sh, a)
        s *= 2
    h = h_ref[...]
    gy = _gelu_tanh(yr_ref[...])
    for grp in range(tc // 8):
        rows = slice(grp * 8, (grp + 1) * 8)
        hh = a[grp] * h + u[grp]
        h = hh[7:8, :]
        o_ref[rows, :] = (hh * gy[rows, :]).astype(o_ref.dtype)
    h_ref[...] = h


def _rglru(r, conv_w, conv_b, wa, wx, ba, bx, lam, batch, seq, tc=256):
    t = batch * seq
    nc = seq // tc
    w = LRU_WIDTH
    row = lambda v: v.reshape(1, w).astype(F32)
    full2 = lambda shape: pl.BlockSpec(shape, lambda b, c: (0,) * len(shape))
    return pl.pallas_call(
        _rglru_kernel,
        grid=(batch, nc),
        in_specs=[pl.BlockSpec((tc, w), lambda b, c: (b * nc + c, 0)),
                  pl.BlockSpec((tc, w), lambda b, c: (b * nc + c, 1)),
                  full2((CONV_WIDTH, w)), full2((1, w)),
                  full2((LRU_BLOCKS, LRU_BLOCK_DIM, LRU_BLOCK_DIM)),
                  full2((LRU_BLOCKS, LRU_BLOCK_DIM, LRU_BLOCK_DIM)),
                  full2((1, w)), full2((1, w)), full2((1, w))],
        out_specs=pl.BlockSpec((tc, w), lambda b, c: (b * nc + c, 0)),
        out_shape=jax.ShapeDtypeStruct((t, w), BF16),
        scratch_shapes=[pltpu.VMEM((8, w), F32), pltpu.VMEM((1, w), F32)],
        compiler_params=_params(("parallel", "arbitrary")),
        name="rglru",
    )(r, r, conv_w.astype(F32), row(conv_b), wa.astype(BF16), wx.astype(BF16),
      row(ba), row(bx), row(lam))


def _merge_kernel(oa_ref, or_ref, g0_ref, g1_ref, wb0_ref, wb1_ref, wo_ref, x_ref, gn_ref,
                  x1_ref, xn_ref):
    pb0 = jnp.dot(oa_ref[...], wb0_ref[...], preferred_element_type=F32)
    pb1 = jnp.dot(or_ref[...], wb1_ref[...], preferred_element_type=F32)
    merged = g0_ref[...].astype(F32) * pb0 + g1_ref[...].astype(F32) * pb1
    x1 = x_ref[...] + jnp.dot(merged.astype(BF16), wo_ref[...], preferred_element_type=F32)
    x1_ref[...] = x1
    xn_ref[...] = _rms(x1, gn_ref[...]).astype(xn_ref.dtype)


def _merge_out(o_att, o_rec, gates, wb0, wb1, w_out, x, g_ffn, tm=512):
    t, d = x.shape
    cw = o_att.shape[1]
    resident = lambda shape: pl.BlockSpec(shape, lambda i: (0, 0), pipeline_mode=pl.Buffered(1))
    return pl.pallas_call(
        _merge_kernel,
        grid=(t // tm,),
        in_specs=[pl.BlockSpec((tm, cw), lambda i: (i, 0)),
                  pl.BlockSpec((tm, cw), lambda i: (i, 0)),
                  pl.BlockSpec((tm, d), lambda i: (i, 0)),
                  pl.BlockSpec((tm, d), lambda i: (i, 1)),
                  resident((cw, d)), resident((cw, d)), resident((d, d)),
                  pl.BlockSpec((tm, d), lambda i: (i, 0)),
                  pl.BlockSpec((1, d), lambda i: (0, 0))],
        out_specs=[pl.BlockSpec((tm, d), lambda i: (i, 0)),
                   pl.BlockSpec((tm, d), lambda i: (i, 0))],
        out_shape=[jax.ShapeDtypeStruct((t, d), F32), jax.ShapeDtypeStruct((t, d), BF16)],
        compiler_params=_params(("parallel",)),
        name="merge_out",
    )(o_att, o_rec, gates, gates, wb0, wb1, w_out, x, g_ffn.reshape(1, d).astype(F32))


def _oddeven_merge(lo, hi, r):
    step = r * 2
    if step < hi - lo:
        yield from _oddeven_merge(lo, hi, step)
        yield from _oddeven_merge(lo + r, hi, step)
        yield from [(i, i + r) for i in range(lo + r, hi - r, step)]
    else:
        yield (lo, lo + r)


def _oddeven_merge_sort(lo, hi):
    if hi - lo >= 1:
        mid = lo + (hi - lo) // 2
        yield from _oddeven_merge_sort(lo, mid)
        yield from _oddeven_merge_sort(mid + 1, hi)
        yield from _oddeven_merge(lo, hi, 1)


def _peer_scores_kernel(q_ref, keys_ref, thr_ref, e1_ref, s2_ref, e2_ref, s_ref, vals_ref, cand_ref):
    nhc = keys_ref.shape[0]
    k = PEER_TOPK
    for hc in range(nhc):
        qs = q_ref[:, hc * PEER_NKEYS:(hc + 1) * PEER_NKEYS]
        s_ref[hc] = lax.dot_general(keys_ref[hc], qs, _NT_DIMS, preferred_element_type=F32)

    n_lvl = PEER_NKEYS // 8
    network = list(_oddeven_merge_sort(0, n_lvl - 1))

    def top_values(hc, carry):
        x = s_ref[hc]
        lvl = [x[8 * v:8 * v + 8, :] for v in range(n_lvl)]
        for a, b in network:
            lvl[a], lvl[b] = jnp.maximum(lvl[a], lvl[b]), jnp.minimum(lvl[a], lvl[b])
        for r in range(k + 1):
            m = jnp.max(lvl[0], axis=0, keepdims=True)
            vals_ref[hc, r:r + 1, :] = m
            hit = lvl[0] == m
            for v in range(min(k - r, n_lvl)):
                below = lvl[v + 1] if v + 1 < n_lvl else LOWEST
                lvl[v] = jnp.where(hit, below, lvl[v])
        return carry

    lax.fori_loop(0, nhc, top_values, 0)

    def head_stats(h, carry):
        v1 = vals_ref[2 * h, 0:k + 1, :]
        v2 = vals_ref[2 * h + 1, 0:k + 1, :]
        cand_ref[...] = jnp.full(cand_ref.shape, LOWEST, F32)
        off = 0
        for a in range(k + 1):
            nb = (k + 1) // (a + 1)
            cand_ref[off:off + nb, :] = v1[a:a + 1, :] + v2[0:nb, :]
            off += nb
        cand = cand_ref[...]
        x = cand
        kth = None
        for r in range(k):
            kth = jnp.max(x, axis=0, keepdims=True)
            x = jnp.where(x == kth, LOWEST, x)
        nxt = jnp.max(x, axis=0, keepdims=True)
        m1 = v1[0:1, :]
        m2 = v2[0:1, :]
        tau = 0.5 * (kth + nxt)
        z = jnp.sum(jnp.where(cand >= tau, jnp.exp(cand - (m1 + m2)), 0.0), axis=0, keepdims=True)
        s1 = s_ref[2 * h]
        s2 = s_ref[2 * h + 1]
        thr_ref[h] = tau - s1
        e1_ref[h] = jnp.exp(s1 - m1)
        e2 = jnp.exp(s2 - m2) * (0.5 / z)
        for c in range(s2_ref.shape[1]):
            s2_ref[h, c] = s2[:, c * 128:(c + 1) * 128]
            e2_ref[h, c] = e2[:, c * 128:(c + 1) * 128]
        return carry

    lax.fori_loop(0, nhc // 2, head_stats, 0)


def _peer_scores(q, keys, tm=256):
    t, _ = q.shape
    nhc = keys.shape[0]
    out_spec = pl.BlockSpec((PEER_HEADS, PEER_NKEYS, tm), lambda i: (0, 0, i))
    out_shape = jax.ShapeDtypeStruct((PEER_HEADS, PEER_NKEYS, t), F32)
    chunk_spec = pl.BlockSpec((PEER_HEADS, tm // 128, PEER_NKEYS, 128), lambda i: (0, i, 0, 0))
    chunk_shape = jax.ShapeDtypeStruct((PEER_HEADS, t // 128, PEER_NKEYS, 128), F32)
    n_cand = sum((PEER_TOPK + 1) // (a + 1) for a in range(PEER_TOPK + 1))
    n_cand = -(-n_cand // 8) * 8
    return pl.pallas_call(
        _peer_scores_kernel,
        grid=(t // tm,),
        in_specs=[pl.BlockSpec((tm, nhc * PEER_NKEYS), lambda i: (i, 0)),
                  pl.BlockSpec(keys.shape, lambda i: (0, 0, 0))],
        out_specs=[out_spec, out_spec, chunk_spec, chunk_spec],
        out_shape=[out_shape, out_shape, chunk_shape, chunk_shape],
        scratch_shapes=[pltpu.VMEM((nhc, PEER_NKEYS, tm), F32),
                        pltpu.VMEM((nhc, PEER_TOPK + 8, tm), F32),
                        pltpu.VMEM((n_cand, tm), F32)],
        compiler_params=_params(("parallel",)),
        name="peer_scores",
    )(q, keys)


def _transpose_kernel(x_ref, o_ref):
    o_ref[...] = x_ref[...].T.astype(o_ref.dtype)


def _transposed_tiles(table, tn):
    e, d = table.shape
    return pl.pallas_call(
        _transpose_kernel,
        grid=(e // tn,),
        in_specs=[pl.BlockSpec((tn, d), lambda i: (i, 0))],
        out_specs=pl.BlockSpec((None, d, tn), lambda i: (i, 0, 0)),
        out_shape=jax.ShapeDtypeStruct((e // tn, d, tn), BF16),
        compiler_params=_params(("parallel",)),
        name="expert_value_tiles",
    )(table)


def _peer_dense_kernel(xn_ref, thr_ref, e1_ref, s2_ref, e2_ref, u_ref, vt_ref, acc_ref,
                       xnt_ref, act_ref, p_ref):
    j = pl.program_id(1)
    lane_chunks, tn, _ = act_ref.shape
    rows = tn // PEER_NKEYS

    @pl.when(j == 0)
    def _():
        acc_ref[...] = jnp.zeros(acc_ref.shape, F32)
        xnt_ref[...] = xn_ref[...].astype(F32).T.astype(xnt_ref.dtype)

    act = jnp.dot(u_ref[...], xnt_ref[...], preferred_element_type=F32)
    for lc in range(lane_chunks):
        act_ref[lc] = act[:, lc * 128:(lc + 1) * 128]

    def weigh(r, carry):
        i1 = j * rows + r
        roff = pl.multiple_of(r * PEER_NKEYS, PEER_NKEYS)
        thr_rows = [thr_ref[h, pl.ds(i1, 1), :] for h in range(PEER_HEADS)]
        e1_rows = [e1_ref[h, pl.ds(i1, 1), :] for h in range(PEER_HEADS)]
        for lc in range(lane_chunks):
            lanes = slice(lc * 128, (lc + 1) * 128)
            w = None
            for h in range(PEER_HEADS):
                w_h = jnp.where(s2_ref[h, lc] >= thr_rows[h][:, lanes],
                                e2_ref[h, lc] * e1_rows[h][:, lanes], 0.0)
                w = w_h if w is None else w + w_h
            a = act_ref[lc, pl.ds(roff, PEER_NKEYS), :]
            p_ref[pl.ds(roff, PEER_NKEYS), lanes] = (_gelu_tanh_x2(a) * w).astype(p_ref.dtype)
        return carry

    lax.fori_loop(0, rows, weigh, 0)
    res = jnp.dot(vt_ref[...], p_ref[...], preferred_element_type=F32)
    for lc in range(lane_chunks):
        acc_ref[lc] += res[:, lc * 128:(lc + 1) * 128]


def _peer_dense(xn, thr, e1, s2, e2, u_bf, vt_tiles, tm=512):
    t, d = xn.shape
    n_tiles, _, tn = vt_tiles.shape
    route_spec = pl.BlockSpec((PEER_HEADS, PEER_NKEYS, tm), lambda i, j: (0, 0, i))
    chunk_spec = pl.BlockSpec((PEER_HEADS, tm // 128, PEER_NKEYS, 128), lambda i, j: (0, i, 0, 0))
    return pl.pallas_call(
        _peer_dense_kernel,
        grid=(t // tm, n_tiles),
        in_specs=[pl.BlockSpec((tm, d), lambda i, j: (i, 0)),
                  route_spec, route_spec, chunk_spec, chunk_spec,
                  pl.BlockSpec((tn, d), lambda i, j: (j, 0)),
                  pl.BlockSpec((None, d, tn), lambda i, j: (j, 0, 0))],
        out_specs=pl.BlockSpec((tm // 128, d, 128), lambda i, j: (i, 0, 0)),
        out_shape=jax.ShapeDtypeStruct((t // 128, d, 128), F32),
        scratch_shapes=[pltpu.VMEM((d, tm), BF16), pltpu.VMEM((tm // 128, tn, 128), F32),
                        pltpu.VMEM((tn, tm), BF16)],
        compiler_params=_params(("parallel", "arbitrary")),
        name="peer_dense",
    )(xn, thr, e1, s2, e2, u_bf, vt_tiles)


def _final_kernel(x1_ref, pt_ref, g_ref, y_ref):
    for c in range(pt_ref.shape[0]):
        rows = slice(c * 128, (c + 1) * 128)
        y_ref[rows, :] = _rms(x1_ref[rows, :] + pt_ref[c].T, g_ref[...])


def _residual_norm(x1, peer_chunks, g, tm=256):
    t, d = x1.shape
    return pl.pallas_call(
        _final_kernel,
        grid=(t // tm,),
        in_specs=[pl.BlockSpec((tm, d), lambda i: (i, 0)),
                  pl.BlockSpec((tm // 128, d, 128), lambda i: (i, 0, 0)),
                  pl.BlockSpec((1, d), lambda i: (0, 0))],
        out_specs=pl.BlockSpec((tm, d), lambda i: (i, 0)),
        out_shape=jax.ShapeDtypeStruct((t, d), F32),
        compiler_params=_params(("parallel",)),
        name="residual_norm",
    )(x1, peer_chunks, g.reshape(1, d).astype(F32))


def kernel(x, norm_mix_g, w_in, conv_w, conv_b, lru_wa, lru_ba, lru_wx, lru_bx, lru_lambda,
           w_branch, w_out, rel_bias, norm_ffn_g, peer_wq, peer_keys, peer_u, peer_v, norm_final_g):
    batch, seq, d = x.shape
    t = batch * seq
    assert w_in.shape[0] == 1, "single-layer trunk"
    xt = x.reshape(t, d)
    bias = _bias_tiles(rel_bias)
    a_w = ATTN_WIDTH
    w_l = w_in[0]
    h = _rmsnorm(xt, norm_mix_g[0], BF16)
    qk = _matmul(h, w_l, 0, 2 * a_w, BF16, "proj_qk")
    vt3 = _v_transposed(w_l[:, 2 * a_w:3 * a_w].T.astype(BF16), h)
    r = _matmul(h, w_l, 3 * a_w, 2 * LRU_WIDTH, F32, "proj_rec")
    gates = _matmul(h, w_l, 3 * a_w + 2 * LRU_WIDTH, 2 * d, BF16, "proj_gate", sigmoid=True)
    o_att = _attention(qk, vt3, bias, batch, seq)
    o_rec = _rglru(r, conv_w[0], conv_b[0], lru_wa[0], lru_wx[0], lru_ba[0], lru_bx[0],
                   lru_lambda[0], batch, seq)
    x1, xn = _merge_out(o_att, o_rec, gates, w_branch[0, 0].astype(BF16),
                        w_branch[0, 1].astype(BF16), w_out[0].astype(BF16), xt, norm_ffn_g[0])
    q = _matmul(xn, peer_wq[0], 0, peer_wq.shape[2], BF16, "peer_query")
    keys = peer_keys[0].reshape(PEER_HEADS * 2, PEER_NKEYS, PEER_DKEY // 2).astype(BF16)
    thr, e1, s2, e2 = _peer_scores(q, keys)
    vt_tiles = _transposed_tiles(peer_v[0], PEER_EXPERT_TILE)
    peer_out = _peer_dense(xn, thr, e1, s2, e2, peer_u[0].astype(BF16), vt_tiles)
    y = _residual_norm(x1, peer_out, norm_final_g)
    return y.reshape(batch, seq, d)
```

```python
import functools
import math

import jax
import jax.numpy as jnp
from jax import lax
from jax.experimental import pallas as pl
from jax.experimental.pallas import tpu as pltpu

ATTN_HEADS = 8
HEAD_DIM = 128
ATTN_WIDTH = ATTN_HEADS * HEAD_DIM
MOBA_BLOCK = 256
MOBA_TOPK = 3
LRU_WIDTH = 1024
LRU_BLOCKS = 8
LRU_BLOCK_DIM = LRU_WIDTH // LRU_BLOCKS
CONV_WIDTH = 4
LRU_C = 8.0
REL_BUCKETS = 32
REL_MAX_DIST = 128
PEER_HEADS = 8
PEER_NKEYS = 128
PEER_DKEY = 256
PEER_TOPK = 16
EPS = 1e-6
NEG = -1e30
LOWEST = -3.0e38
PEER_EXPERT_TILE = 1024

V7X_VMEM_LIMIT_BYTES = 56 * 1024 * 1024

F32 = jnp.float32
BF16 = jnp.bfloat16

_NT_DIMS = (((1,), (1,)), ((), ()))


def _params(semantics):
    return pltpu.CompilerParams(dimension_semantics=semantics,
                                vmem_limit_bytes=V7X_VMEM_LIMIT_BYTES)


def _gelu_tanh_x2(x):
    c = math.sqrt(2.0 / math.pi)
    return x * (1.0 + jnp.tanh(x * (c + (0.044715 * c) * (x * x))))


def _gelu_tanh(x):
    return 0.5 * _gelu_tanh_x2(x)


def _rms(x, g):
    ms = jnp.mean(x * x, axis=-1, keepdims=True)
    return x * lax.rsqrt(ms + EPS) * g


def _rmsnorm_kernel(x_ref, g_ref, o_ref):
    o_ref[...] = _rms(x_ref[...], g_ref[...]).astype(o_ref.dtype)


def _rmsnorm(x, g, out_dtype, tm=512):
    t, d = x.shape
    return pl.pallas_call(
        _rmsnorm_kernel,
        grid=(t // tm,),
        in_specs=[pl.BlockSpec((tm, d), lambda i: (i, 0)),
                  pl.BlockSpec((1, d), lambda i: (0, 0))],
        out_specs=pl.BlockSpec((tm, d), lambda i: (i, 0)),
        out_shape=jax.ShapeDtypeStruct((t, d), out_dtype),
        compiler_params=_params(("parallel",)),
        name="rmsnorm",
    )(x, g.reshape(1, d))


def _mm_kernel(a_ref, b_ref, o_ref, w_ref):
    @pl.when(pl.program_id(1) == 0)
    def _():
        w_ref[...] = b_ref[...].astype(w_ref.dtype)

    o_ref[...] = jnp.dot(a_ref[...], w_ref[...], preferred_element_type=F32).astype(o_ref.dtype)


def _matmul(a, b, col_start, n, out_dtype, name, tm=1024, tn=1024):
    m, k = a.shape
    assert col_start % tn == 0 and n % tn == 0
    col_blk = col_start // tn
    return pl.pallas_call(
        _mm_kernel,
        grid=(n // tn, m // tm),
        in_specs=[pl.BlockSpec((tm, k), lambda j, i: (i, 0)),
                  pl.BlockSpec((k, tn), lambda j, i: (0, col_blk + j))],
        out_specs=pl.BlockSpec((tm, tn), lambda j, i: (i, j)),
        out_shape=jax.ShapeDtypeStruct((m, n), out_dtype),
        scratch_shapes=[pltpu.VMEM((k, tn), BF16)],
        compiler_params=_params(("parallel", "arbitrary")),
        name=name,
    )(a, b)


def _vt_kernel(w_ref, h_ref, o_ref):
    res = lax.dot_general(w_ref[...], h_ref[...], _NT_DIMS, preferred_element_type=F32)
    for t in range(o_ref.shape[0]):
        o_ref[t] = res[:, t * MOBA_BLOCK:(t + 1) * MOBA_BLOCK].astype(o_ref.dtype)


def _v_transposed(w_t, h, tm=1024):
    c, k = w_t.shape
    t, _ = h.shape
    nb = tm // MOBA_BLOCK
    return pl.pallas_call(
        _vt_kernel,
        grid=(t // tm,),
        in_specs=[pl.BlockSpec((c, k), lambda i: (0, 0)),
                  pl.BlockSpec((tm, k), lambda i: (i, 0))],
        out_specs=pl.BlockSpec((nb, c, MOBA_BLOCK), lambda i: (i, 0, 0)),
        out_shape=jax.ShapeDtypeStruct((t // MOBA_BLOCK, c, MOBA_BLOCK), BF16),
        compiler_params=_params(("parallel",)),
        name="v_transposed",
    )(w_t, h)


def _rel_bucket(dist):
    n = jnp.maximum(dist, 0)
    max_exact = REL_BUCKETS // 2
    nf = jnp.maximum(n, 1).astype(F32)
    large = max_exact + (jnp.log(nf / max_exact) / math.log(REL_MAX_DIST / max_exact)
                         * (REL_BUCKETS - max_exact)).astype(jnp.int32)
    large = jnp.minimum(large, REL_BUCKETS - 1)
    return jnp.where(n < max_exact, n, large)


def _bias_kernel(relb_ref, bucket_ref, o_ref):
    h = pl.program_id(0)
    for t in range(3):
        bk = bucket_ref[t]
        acc = jnp.zeros(bk.shape, F32)
        for b in range(REL_BUCKETS):
            acc = jnp.where(bk == b, relb_ref[b, h], acc)
        if t == 0:
            key = lax.broadcasted_iota(jnp.int32, bk.shape, 0)
            qry = lax.broadcasted_iota(jnp.int32, bk.shape, 1)
            acc = jnp.where(key <= qry, acc, NEG)
        for c in range(o_ref.shape[2]):
            o_ref[0, t, c] = acc[:, c * 128:(c + 1) * 128]


def _bias_tiles(rel_bias):
    assert REL_MAX_DIST <= MOBA_BLOCK
    key = jnp.arange(MOBA_BLOCK, dtype=jnp.int32)[:, None]
    qry = jnp.arange(MOBA_BLOCK, dtype=jnp.int32)[None, :]
    buckets = jnp.stack([_rel_bucket(t * MOBA_BLOCK + qry - key) for t in range(3)])
    return pl.pallas_call(
        _bias_kernel,
        grid=(ATTN_HEADS,),
        in_specs=[pl.BlockSpec(memory_space=pltpu.SMEM),
                  pl.BlockSpec((3, MOBA_BLOCK, MOBA_BLOCK), lambda h: (0, 0, 0))],
        out_specs=pl.BlockSpec((1, 3, MOBA_BLOCK // 128, MOBA_BLOCK, 128), lambda h: (h, 0, 0, 0, 0)),
        out_shape=jax.ShapeDtypeStruct((ATTN_HEADS, 3, MOBA_BLOCK // 128, MOBA_BLOCK, 128), F32),
        compiler_params=_params(("parallel",)),
        name="attn_bias_tiles",
    )(rel_bias.astype(F32), buckets)


ATTN_HEADS_PER_STEP = 8


def _attn_kernel(q_ref, k_ref, vt_ref, bias_ref, o_ref, kmean_ref, sel_ref, acc_ref, *, nb):
    i = pl.program_id(2)
    scale = HEAD_DIM ** -0.5
    heads = range(ATTN_HEADS_PER_STEP)
    hsl = [slice(g * HEAD_DIM, (g + 1) * HEAD_DIM) for g in heads]
    q_chunks = acc_ref.shape[1]
    qsl = [slice(c * 128, (c + 1) * 128) for c in range(q_chunks)]

    @pl.when(i == 0)
    def _():
        for g in heads:
            for n in range(nb):
                kb = k_ref[n * MOBA_BLOCK:(n + 1) * MOBA_BLOCK, hsl[g]].astype(F32)
                kmean_ref[g, n:n + 1, :] = jnp.mean(kb, axis=0, keepdims=True)

    qs = [q_ref[:, hsl[g]] for g in heads]
    for g in heads:
        gate = lax.dot_general(kmean_ref[g].astype(BF16), qs[g], _NT_DIMS,
                               preferred_element_type=F32)
        blk = lax.broadcasted_iota(jnp.int32, gate.shape, 0)
        rank = jnp.zeros(gate.shape, F32)
        for m in range(nb):
            gm = gate[m:m + 1, :]
            beats = jnp.where(gm > gate, 1.0, jnp.where(gm == gate, jnp.where(m < blk, 1.0, 0.0), 0.0))
            rank = rank + beats * jnp.where(m < i, 1.0, 0.0)
        sel_ref[g] = jnp.where(blk < i, jnp.where(rank < MOBA_TOPK, 1.0, 0.0), 0.0)

    def scores(g, j, tile):
        koff = pl.multiple_of(j * MOBA_BLOCK, MOBA_BLOCK)
        kj = k_ref[pl.ds(koff, MOBA_BLOCK), hsl[g]]
        s = lax.dot_general(kj, qs[g], _NT_DIMS, preferred_element_type=F32)
        bias = jnp.concatenate([bias_ref[g, tile, c] for c in range(q_chunks)], axis=1)
        return s * scale + bias

    s0 = [scores(g, i, 0) for g in heads]
    m0 = [jnp.max(s0[g], axis=0, keepdims=True) for g in heads]
    p0 = [jnp.exp(s0[g] - m0[g]) for g in heads]
    l0 = [jnp.sum(p0[g], axis=0, keepdims=True) for g in heads]
    for g in heads:
        pv = jnp.dot(vt_ref[i, hsl[g], :], p0[g].astype(BF16), preferred_element_type=F32)
        for c in range(q_chunks):
            acc_ref[g, c] = pv[:, qsl[c]]

    def body(j, carry):
        ms, ls = carry
        tile = jnp.minimum(i - j, 2)
        ss = [scores(g, j, tile) for g in heads]
        ss = [jnp.where(sel_ref[g, pl.ds(j, 1), :] > 0.5, ss[g], NEG) for g in heads]
        new_m = [jnp.maximum(ms[g], jnp.max(ss[g], axis=0, keepdims=True)) for g in heads]
        alpha = [jnp.exp(ms[g] - new_m[g]) for g in heads]
        ps = [jnp.exp(ss[g] - new_m[g]) for g in heads]
        new_l = [alpha[g] * ls[g] + jnp.sum(ps[g], axis=0, keepdims=True) for g in heads]
        pv = [jnp.dot(vt_ref[j, hsl[g], :], ps[g].astype(BF16), preferred_element_type=F32) for g in heads]
        for g in heads:
            for c in range(q_chunks):
                acc_ref[g, c] = alpha[g][:, qsl[c]] * acc_ref[g, c] + pv[g][:, qsl[c]]
        return tuple(new_m), tuple(new_l)

    _, l_f = lax.fori_loop(0, i, body, (tuple(m0), tuple(l0)))
    for g in heads:
        for c in range(q_chunks):
            o_ref[qsl[c], hsl[g]] = (acc_ref[g, c] / l_f[g][:, qsl[c]]).T.astype(o_ref.dtype)


def _attention(qk, vt3, bias, batch, seq):
    nb = seq // MOBA_BLOCK
    t = batch * seq
    gh = ATTN_HEADS_PER_STEP
    width = gh * HEAD_DIM
    groups = ATTN_HEADS // gh
    return pl.pallas_call(
        functools.partial(_attn_kernel, nb=nb),
        grid=(batch, groups, nb),
        in_specs=[
            pl.BlockSpec((MOBA_BLOCK, width), lambda b, h, i: (b * nb + i, h)),
            pl.BlockSpec((seq, width), lambda b, h, i: (b, groups + h)),
            pl.BlockSpec((nb, width, MOBA_BLOCK), lambda b, h, i: (b, h, 0)),
            pl.BlockSpec((gh, 3, MOBA_BLOCK // 128, MOBA_BLOCK, 128), lambda b, h, i: (h, 0, 0, 0, 0)),
        ],
        out_specs=pl.BlockSpec((MOBA_BLOCK, width), lambda b, h, i: (b * nb + i, h)),
        out_shape=jax.ShapeDtypeStruct((t, ATTN_WIDTH), BF16),
        scratch_shapes=[pltpu.VMEM((gh, nb, HEAD_DIM), F32),
                        pltpu.VMEM((gh, nb, MOBA_BLOCK), F32),
                        pltpu.VMEM((gh, MOBA_BLOCK // 128, HEAD_DIM, 128), F32)],
        compiler_params=_params(("parallel", "parallel", "arbitrary")),
        name="moba_attention",
    )(qk, qk, vt3, bias)


def _rglru_kernel(xr_ref, yr_ref, cw_ref, cb_ref, wa_ref, wx_ref, ba_ref, bx_ref, lam_ref,
                  o_ref, tail_ref, h_ref):
    c = pl.program_id(1)
    tc = xr_ref.shape[0]

    @pl.when(c == 0)
    def _():
        tail_ref[...] = jnp.zeros(tail_ref.shape, F32)
        h_ref[...] = jnp.zeros(h_ref.shape, F32)

    x = xr_ref[...]
    grp_shape = (tc // 8, 8, x.shape[1])
    x3 = x.reshape(grp_shape)
    tail3 = tail_ref[...].reshape(1, 8, x.shape[1])
    sub = lax.broadcasted_iota(jnp.int32, grp_shape, 1)
    conv = cb_ref[...] + cw_ref[CONV_WIDTH - 1:CONV_WIDTH, :] * x3
    for s in range(1, CONV_WIDTH):
        k = CONV_WIDTH - 1 - s
        rolled = pltpu.roll(x3, s, axis=1)
        rolled_prev = jnp.concatenate([pltpu.roll(tail3, s, axis=1), rolled[:-1]], axis=0)
        conv = conv + cw_ref[k:k + 1, :] * jnp.where(sub >= s, rolled, rolled_prev)
    conv = conv.reshape(x.shape)
    tail_ref[...] = x[tc - 8:, :]

    xb16 = conv.astype(BF16)
    pre_a, pre_x = [], []
    for g in range(LRU_BLOCKS):
        xs = xb16[:, g * LRU_BLOCK_DIM:(g + 1) * LRU_BLOCK_DIM]
        pre_a.append(jnp.dot(xs, wa_ref[g], preferred_element_type=F32))
        pre_x.append(jnp.dot(xs, wx_ref[g], preferred_element_type=F32))
    r_gate = jax.nn.sigmoid(jnp.concatenate(pre_a, axis=1) + ba_ref[...])
    i_gate = jax.nn.sigmoid(jnp.concatenate(pre_x, axis=1) + bx_ref[...])
    z = -lam_ref[...]
    softplus = jnp.maximum(z, 0.0) + jnp.log1p(jnp.exp(-jnp.abs(z)))
    log_a = (-LRU_C) * r_gate * softplus
    a = jnp.exp(log_a)
    th = jnp.tanh(log_a)
    y2 = -2.0 * th / (1.0 - th)
    u = jnp.where(y2 > 0.0, y2 * lax.rsqrt(y2), 0.0) * (i_gate * conv)

    grp_shape = (tc // 8, 8, a.shape[1])
    a = a.reshape(grp_shape)
    u = u.reshape(grp_shape)
    sub = lax.broadcasted_iota(jnp.int32, grp_shape, 1)
    s = 1
    while s < 8:
        a_sh = pltpu.roll(a, s, axis=1)
        u_sh = pltpu.roll(u, s, axis=1)
        valid = sub >= s
        u = jnp.where(valid, a * u_sh + u, u)
        a = jnp.where(valid, a * a_sh, a)
        s *= 2
    h = h_ref[...]
    gy = _gelu_tanh(yr_ref[...])
    for grp in range(tc // 8):
        rows = slice(grp * 8, (grp + 1) * 8)
        hh = a[grp] * h + u[grp]
        h = hh[7:8, :]
        o_ref[rows, :] = (hh * gy[rows, :]).astype(o_ref.dtype)
    h_ref[...] = h


def _rglru(r, conv_w, conv_b, wa, wx, ba, bx, lam, batch, seq, tc=256):
    t = batch * seq
    nc = seq // tc
    w = LRU_WIDTH
    row = lambda v: v.reshape(1, w).astype(F32)
    full2 = lambda shape: pl.BlockSpec(shape, lambda b, c: (0,) * len(shape))
    return pl.pallas_call(
        _rglru_kernel,
        grid=(batch, nc),
        in_specs=[pl.BlockSpec((tc, w), lambda b, c: (b * nc + c, 0)),
                  pl.BlockSpec((tc, w), lambda b, c: (b * nc + c, 1)),
                  full2((CONV_WIDTH, w)), full2((1, w)),
                  full2((LRU_BLOCKS, LRU_BLOCK_DIM, LRU_BLOCK_DIM)),
                  full2((LRU_BLOCKS, LRU_BLOCK_DIM, LRU_BLOCK_DIM)),
                  full2((1, w)), full2((1, w)), full2((1, w))],
        out_specs=pl.BlockSpec((tc, w), lambda b, c: (b * nc + c, 0)),
        out_shape=jax.ShapeDtypeStruct((t, w), BF16),
        scratch_shapes=[pltpu.VMEM((8, w), F32), pltpu.VMEM((1, w), F32)],
        compiler_params=_params(("parallel", "arbitrary")),
        name="rglru",
    )(r, r, conv_w.astype(F32), row(conv_b), wa.astype(BF16), wx.astype(BF16),
      row(ba), row(bx), row(lam))


def _merge_kernel(oa_ref, or_ref, g0_ref, g1_ref, wb0_ref, wb1_ref, wo_ref, x_ref, gn_ref,
                  x1_ref, xn_ref):
    pb0 = jnp.dot(oa_ref[...], wb0_ref[...], preferred_element_type=F32)
    pb1 = jnp.dot(or_ref[...], wb1_ref[...], preferred_element_type=F32)
    merged = jax.nn.sigmoid(g0_ref[...]) * pb0 + jax.nn.sigmoid(g1_ref[...]) * pb1
    x1 = x_ref[...] + jnp.dot(merged.astype(BF16), wo_ref[...], preferred_element_type=F32)
    x1_ref[...] = x1
    xn_ref[...] = _rms(x1, gn_ref[...]).astype(xn_ref.dtype)


def _merge_out(o_att, o_rec, gl, wb0, wb1, w_out, x, g_ffn, tm=256):
    t, d = x.shape
    cw = o_att.shape[1]
    resident = lambda shape: pl.BlockSpec(shape, lambda i: (0, 0), pipeline_mode=pl.Buffered(1))
    return pl.pallas_call(
        _merge_kernel,
        grid=(t // tm,),
        in_specs=[pl.BlockSpec((tm, cw), lambda i: (i, 0)),
                  pl.BlockSpec((tm, cw), lambda i: (i, 0)),
                  pl.BlockSpec((tm, d), lambda i: (i, 0)),
                  pl.BlockSpec((tm, d), lambda i: (i, 1)),
                  resident((cw, d)), resident((cw, d)), resident((d, d)),
                  pl.BlockSpec((tm, d), lambda i: (i, 0)),
                  pl.BlockSpec((1, d), lambda i: (0, 0))],
        out_specs=[pl.BlockSpec((tm, d), lambda i: (i, 0)),
                   pl.BlockSpec((tm, d), lambda i: (i, 0))],
        out_shape=[jax.ShapeDtypeStruct((t, d), F32), jax.ShapeDtypeStruct((t, d), BF16)],
        compiler_params=_params(("parallel",)),
        name="merge_out",
    )(o_att, o_rec, gl, gl, wb0, wb1, w_out, x, g_ffn.reshape(1, d).astype(F32))


def _oddeven_merge(lo, hi, r):
    step = r * 2
    if step < hi - lo:
        yield from _oddeven_merge(lo, hi, step)
        yield from _oddeven_merge(lo + r, hi, step)
        yield from [(i, i + r) for i in range(lo + r, hi - r, step)]
    else:
        yield (lo, lo + r)


def _oddeven_merge_sort(lo, hi):
    if hi - lo >= 1:
        mid = lo + (hi - lo) // 2
        yield from _oddeven_merge_sort(lo, mid)
        yield from _oddeven_merge_sort(mid + 1, hi)
        yield from _oddeven_merge(lo, hi, 1)


def _peer_scores_kernel(q_ref, keys_ref, thr_ref, e1_ref, s2_ref, e2_ref, s_ref, vals_ref, cand_ref):
    nhc = keys_ref.shape[0]
    k = PEER_TOPK
    for hc in range(nhc):
        qs = q_ref[:, hc * PEER_NKEYS:(hc + 1) * PEER_NKEYS]
        s_ref[hc] = lax.dot_general(keys_ref[hc], qs, _NT_DIMS, preferred_element_type=F32)

    n_lvl = PEER_NKEYS // 8
    network = list(_oddeven_merge_sort(0, n_lvl - 1))

    def top_values(hc, carry):
        x = s_ref[hc]
        lvl = [x[8 * v:8 * v + 8, :] for v in range(n_lvl)]
        for a, b in network:
            lvl[a], lvl[b] = jnp.maximum(lvl[a], lvl[b]), jnp.minimum(lvl[a], lvl[b])
        for r in range(k + 1):
            m = jnp.max(lvl[0], axis=0, keepdims=True)
            vals_ref[hc, r:r + 1, :] = m
            hit = lvl[0] == m
            for v in range(min(k - r, n_lvl)):
                below = lvl[v + 1] if v + 1 < n_lvl else LOWEST
                lvl[v] = jnp.where(hit, below, lvl[v])
        return carry

    lax.fori_loop(0, nhc, top_values, 0)

    def head_stats(h, carry):
        v1 = vals_ref[2 * h, 0:k + 1, :]
        v2 = vals_ref[2 * h + 1, 0:k + 1, :]
        cand_ref[...] = jnp.full(cand_ref.shape, LOWEST, F32)
        off = 0
        for a in range(k + 1):
            nb = (k + 1) // (a + 1)
            cand_ref[off:off + nb, :] = v1[a:a + 1, :] + v2[0:nb, :]
            off += nb
        cand = cand_ref[...]
        x = cand
        kth = None
        for r in range(k):
            kth = jnp.max(x, axis=0, keepdims=True)
            x = jnp.where(x == kth, LOWEST, x)
        nxt = jnp.max(x, axis=0, keepdims=True)
        m1 = v1[0:1, :]
        m2 = v2[0:1, :]
        tau = 0.5 * (kth + nxt)
        z = jnp.sum(jnp.where(cand >= tau, jnp.exp(cand - (m1 + m2)), 0.0), axis=0, keepdims=True)
        s1 = s_ref[2 * h]
        s2 = s_ref[2 * h + 1]
        thr_ref[h] = tau - s1
        e1_ref[h] = jnp.exp(s1 - m1)
        e2 = jnp.exp(s2 - m2) * (0.5 / z)
        for c in range(s2_ref.shape[1]):
            s2_ref[h, c] = s2[:, c * 128:(c + 1) * 128]
            e2_ref[h, c] = e2[:, c * 128:(c + 1) * 128]
        return carry

    lax.fori_loop(0, nhc // 2, head_stats, 0)


def _peer_scores(q, keys, tm=256):
    t, _ = q.shape
    nhc = keys.shape[0]
    out_spec = pl.BlockSpec((PEER_HEADS, PEER_NKEYS, tm), lambda i: (0, 0, i))
    out_shape = jax.ShapeDtypeStruct((PEER_HEADS, PEER_NKEYS, t), F32)
    chunk_spec = pl.BlockSpec((PEER_HEADS, tm // 128, PEER_NKEYS, 128), lambda i: (0, i, 0, 0))
    chunk_shape = jax.ShapeDtypeStruct((PEER_HEADS, t // 128, PEER_NKEYS, 128), F32)
    n_cand = sum((PEER_TOPK + 1) // (a + 1) for a in range(PEER_TOPK + 1))
    n_cand = -(-n_cand // 8) * 8
    return pl.pallas_call(
        _peer_scores_kernel,
        grid=(t // tm,),
        in_specs=[pl.BlockSpec((tm, nhc * PEER_NKEYS), lambda i: (i, 0)),
                  pl.BlockSpec(keys.shape, lambda i: (0, 0, 0))],
        out_specs=[out_spec, out_spec, chunk_spec, chunk_spec],
        out_shape=[out_shape, out_shape, chunk_shape, chunk_shape],
        scratch_shapes=[pltpu.VMEM((nhc, PEER_NKEYS, tm), F32),
                        pltpu.VMEM((nhc, PEER_TOPK + 8, tm), F32),
                        pltpu.VMEM((n_cand, tm), F32)],
        compiler_params=_params(("parallel",)),
        name="peer_scores",
    )(q, keys)


def _transpose_kernel(x_ref, o_ref):
    o_ref[...] = x_ref[...].T.astype(o_ref.dtype)


def _transposed_tiles(table, tn):
    e, d = table.shape
    return pl.pallas_call(
        _transpose_kernel,
        grid=(e // tn,),
        in_specs=[pl.BlockSpec((tn, d), lambda i: (i, 0))],
        out_specs=pl.BlockSpec((None, d, tn), lambda i: (i, 0, 0)),
        out_shape=jax.ShapeDtypeStruct((e // tn, d, tn), BF16),
        compiler_params=_params(("parallel",)),
        name="expert_value_tiles",
    )(table)


def _peer_dense_kernel(xn_ref, thr_ref, e1_ref, s2_ref, e2_ref, u_ref, vt_ref, acc_ref,
                       xnt_ref, act_ref, p_ref):
    j = pl.program_id(1)
    lane_chunks, tn, _ = act_ref.shape
    rows = tn // PEER_NKEYS

    @pl.when(j == 0)
    def _():
        acc_ref[...] = jnp.zeros(acc_ref.shape, F32)
        xnt_ref[...] = xn_ref[...].astype(F32).T.astype(xnt_ref.dtype)

    act = jnp.dot(u_ref[...], xnt_ref[...], preferred_element_type=F32)
    for lc in range(lane_chunks):
        act_ref[lc] = act[:, lc * 128:(lc + 1) * 128]

    def weigh(r, carry):
        i1 = j * rows + r
        roff = pl.multiple_of(r * PEER_NKEYS, PEER_NKEYS)
        thr_rows = [thr_ref[h, pl.ds(i1, 1), :] for h in range(PEER_HEADS)]
        e1_rows = [e1_ref[h, pl.ds(i1, 1), :] for h in range(PEER_HEADS)]
        for lc in range(lane_chunks):
            lanes = slice(lc * 128, (lc + 1) * 128)
            w = None
            for h in range(PEER_HEADS):
                w_h = jnp.where(s2_ref[h, lc] >= thr_rows[h][:, lanes],
                                e2_ref[h, lc] * e1_rows[h][:, lanes], 0.0)
                w = w_h if w is None else w + w_h
            a = act_ref[lc, pl.ds(roff, PEER_NKEYS), :]
            p_ref[pl.ds(roff, PEER_NKEYS), lanes] = (_gelu_tanh_x2(a) * w).astype(p_ref.dtype)
        return carry

    lax.fori_loop(0, rows, weigh, 0)
    res = jnp.dot(vt_ref[...], p_ref[...], preferred_element_type=F32)
    for lc in range(lane_chunks):
        acc_ref[lc] += res[:, lc * 128:(lc + 1) * 128]


def _peer_dense(xn, thr, e1, s2, e2, u_bf, vt_tiles, tm=512):
    t, d = xn.shape
    n_tiles, _, tn = vt_tiles.shape
    route_spec = pl.BlockSpec((PEER_HEADS, PEER_NKEYS, tm), lambda i, j: (0, 0, i))
    chunk_spec = pl.BlockSpec((PEER_HEADS, tm // 128, PEER_NKEYS, 128), lambda i, j: (0, i, 0, 0))
    return pl.pallas_call(
        _peer_dense_kernel,
        grid=(t // tm, n_tiles),
        in_specs=[pl.BlockSpec((tm, d), lambda i, j: (i, 0)),
                  route_spec, route_spec, chunk_spec, chunk_spec,
                  pl.BlockSpec((tn, d), lambda i, j: (j, 0)),
                  pl.BlockSpec((None, d, tn), lambda i, j: (j, 0, 0))],
        out_specs=pl.BlockSpec((tm // 128, d, 128), lambda i, j: (i, 0, 0)),
        out_shape=jax.ShapeDtypeStruct((t // 128, d, 128), F32),
        scratch_shapes=[pltpu.VMEM((d, tm), BF16), pltpu.VMEM((tm // 128, tn, 128), F32),
                        pltpu.VMEM((tn, tm), BF16)],
        compiler_params=_params(("parallel", "arbitrary")),
        name="peer_dense",
    )(xn, thr, e1, s2, e2, u_bf, vt_tiles)


def _final_kernel(x1_ref, pt_ref, g_ref, y_ref):
    for c in range(pt_ref.shape[0]):
        rows = slice(c * 128, (c + 1) * 128)
        y_ref[rows, :] = _rms(x1_ref[rows, :] + pt_ref[c].T, g_ref[...])


def _residual_norm(x1, peer_chunks, g, tm=256):
    t, d = x1.shape
    return pl.pallas_call(
        _final_kernel,
        grid=(t // tm,),
        in_specs=[pl.BlockSpec((tm, d), lambda i: (i, 0)),
                  pl.BlockSpec((tm // 128, d, 128), lambda i: (i, 0, 0)),
                  pl.BlockSpec((1, d), lambda i: (0, 0))],
        out_specs=pl.BlockSpec((tm, d), lambda i: (i, 0)),
        out_shape=jax.ShapeDtypeStruct((t, d), F32),
        compiler_params=_params(("parallel",)),
        name="residual_norm",
    )(x1, peer_chunks, g.reshape(1, d).astype(F32))


def kernel(x, norm_mix_g, w_in, conv_w, conv_b, lru_wa, lru_ba, lru_wx, lru_bx, lru_lambda,
           w_branch, w_out, rel_bias, norm_ffn_g, peer_wq, peer_keys, peer_u, peer_v, norm_final_g):
    batch, seq, d = x.shape
    t = batch * seq
    assert w_in.shape[0] == 1, "single-layer trunk"
    xt = x.reshape(t, d)
    bias = _bias_tiles(rel_bias)
    a_w = ATTN_WIDTH
    w_l = w_in[0]
    h = _rmsnorm(xt, norm_mix_g[0], BF16)
    qk = _matmul(h, w_l, 0, 2 * a_w, BF16, "proj_qk")
    vt3 = _v_transposed(w_l[:, 2 * a_w:3 * a_w].T.astype(BF16), h)
    r = _matmul(h, w_l, 3 * a_w, 2 * LRU_WIDTH, F32, "proj_rec")
    gl = _matmul(h, w_l, 3 * a_w + 2 * LRU_WIDTH, 2 * d, F32, "proj_gate")
    o_att = _attention(qk, vt3, bias, batch, seq)
    o_rec = _rglru(r, conv_w[0], conv_b[0], lru_wa[0], lru_wx[0], lru_ba[0], lru_bx[0],
                   lru_lambda[0], batch, seq)
    x1, xn = _merge_out(o_att, o_rec, gl, w_branch[0, 0].astype(BF16),
                        w_branch[0, 1].astype(BF16), w_out[0].astype(BF16), xt, norm_ffn_g[0])
    q = _matmul(xn, peer_wq[0], 0, peer_wq.shape[2], BF16, "peer_query")
    keys = peer_keys[0].reshape(PEER_HEADS * 2, PEER_NKEYS, PEER_DKEY // 2).astype(BF16)
    thr, e1, s2, e2 = _peer_scores(q, keys)
    vt_tiles = _transposed_tiles(peer_v[0], PEER_EXPERT_TILE)
    peer_out = _peer_dense(xn, thr, e1, s2, e2, peer_u[0].astype(BF16), vt_tiles)
    y = _residual_norm(x1, peer_out, norm_final_g)
    return y.reshape(batch, seq, d)
```

```python
import functools
import math

import jax
import jax.numpy as jnp
from jax import lax
from jax.experimental import pallas as pl
from jax.experimental.pallas import tpu as pltpu

ATTN_HEADS = 8
HEAD_DIM = 128
ATTN_WIDTH = ATTN_HEADS * HEAD_DIM
MOBA_BLOCK = 256
MOBA_TOPK = 3
LRU_WIDTH = 1024
LRU_BLOCKS = 8
LRU_BLOCK_DIM = LRU_WIDTH // LRU_BLOCKS
CONV_WIDTH = 4
LRU_C = 8.0
REL_BUCKETS = 32
REL_MAX_DIST = 128
PEER_HEADS = 8
PEER_NKEYS = 128
PEER_DKEY = 256
PEER_TOPK = 16
EPS = 1e-6
NEG = -1e30
LOWEST = -3.0e38
PEER_EXPERT_TILE = 1024

V7X_VMEM_LIMIT_BYTES = 56 * 1024 * 1024

F32 = jnp.float32
BF16 = jnp.bfloat16

_NT_DIMS = (((1,), (1,)), ((), ()))


def _params(semantics):
    return pltpu.CompilerParams(dimension_semantics=semantics,
                                vmem_limit_bytes=V7X_VMEM_LIMIT_BYTES)


def _gelu_tanh_x2(x):
    c = math.sqrt(2.0 / math.pi)
    return x * (1.0 + jnp.tanh(x * (c + (0.044715 * c) * (x * x))))


def _gelu_tanh(x):
    return 0.5 * _gelu_tanh_x2(x)


def _rms(x, g):
    ms = jnp.mean(x * x, axis=-1, keepdims=True)
    return x * lax.rsqrt(ms + EPS) * g


def _rmsnorm_kernel(x_ref, g_ref, o_ref):
    o_ref[...] = _rms(x_ref[...], g_ref[...]).astype(o_ref.dtype)


def _rmsnorm(x, g, out_dtype, tm=512):
    t, d = x.shape
    return pl.pallas_call(
        _rmsnorm_kernel,
        grid=(t // tm,),
        in_specs=[pl.BlockSpec((tm, d), lambda i: (i, 0)),
                  pl.BlockSpec((1, d), lambda i: (0, 0))],
        out_specs=pl.BlockSpec((tm, d), lambda i: (i, 0)),
        out_shape=jax.ShapeDtypeStruct((t, d), out_dtype),
        compiler_params=_params(("parallel",)),
        name="rmsnorm",
    )(x, g.reshape(1, d))


def _mm_kernel(a_ref, b_ref, o_ref, w_ref):
    @pl.when(pl.program_id(1) == 0)
    def _():
        w_ref[...] = b_ref[...].astype(w_ref.dtype)

    o_ref[...] = jnp.dot(a_ref[...], w_ref[...], preferred_element_type=F32).astype(o_ref.dtype)


def _matmul(a, b, col_start, n, out_dtype, name, tm=1024, tn=1024):
    m, k = a.shape
    assert col_start % tn == 0 and n % tn == 0
    col_blk = col_start // tn
    return pl.pallas_call(
        _mm_kernel,
        grid=(n // tn, m // tm),
        in_specs=[pl.BlockSpec((tm, k), lambda j, i: (i, 0)),
                  pl.BlockSpec((k, tn), lambda j, i: (0, col_blk + j))],
        out_specs=pl.BlockSpec((tm, tn), lambda j, i: (i, j)),
        out_shape=jax.ShapeDtypeStruct((m, n), out_dtype),
        scratch_shapes=[pltpu.VMEM((k, tn), BF16)],
        compiler_params=_params(("parallel", "arbitrary")),
        name=name,
    )(a, b)


def _vt_kernel(w_ref, h_ref, o_ref):
    res = lax.dot_general(w_ref[...], h_ref[...], _NT_DIMS, preferred_element_type=F32)
    for t in range(o_ref.shape[0]):
        o_ref[t] = res[:, t * MOBA_BLOCK:(t + 1) * MOBA_BLOCK].astype(o_ref.dtype)


def _v_transposed(w_t, h, tm=1024):
    c, k = w_t.shape
    t, _ = h.shape
    nb = tm // MOBA_BLOCK
    return pl.pallas_call(
        _vt_kernel,
        grid=(t // tm,),
        in_specs=[pl.BlockSpec((c, k), lambda i: (0, 0)),
                  pl.BlockSpec((tm, k), lambda i: (i, 0))],
        out_specs=pl.BlockSpec((nb, c, MOBA_BLOCK), lambda i: (i, 0, 0)),
        out_shape=jax.ShapeDtypeStruct((t // MOBA_BLOCK, c, MOBA_BLOCK), BF16),
        compiler_params=_params(("parallel",)),
        name="v_transposed",
    )(w_t, h)


def _rel_bucket(dist):
    n = jnp.maximum(dist, 0)
    max_exact = REL_BUCKETS // 2
    nf = jnp.maximum(n, 1).astype(F32)
    large = max_exact + (jnp.log(nf / max_exact) / math.log(REL_MAX_DIST / max_exact)
                         * (REL_BUCKETS - max_exact)).astype(jnp.int32)
    large = jnp.minimum(large, REL_BUCKETS - 1)
    return jnp.where(n < max_exact, n, large)


def _bias_kernel(relb_ref, bucket_ref, o_ref):
    h = pl.program_id(0)
    for t in range(3):
        bk = bucket_ref[t]
        acc = jnp.zeros(bk.shape, F32)
        for b in range(REL_BUCKETS):
            acc = jnp.where(bk == b, relb_ref[b, h], acc)
        if t == 0:
            key = lax.broadcasted_iota(jnp.int32, bk.shape, 0)
            qry = lax.broadcasted_iota(jnp.int32, bk.shape, 1)
            acc = jnp.where(key <= qry, acc, NEG)
        o_ref[0, t] = acc


def _bias_tiles(rel_bias):
    assert REL_MAX_DIST <= MOBA_BLOCK
    key = jnp.arange(MOBA_BLOCK, dtype=jnp.int32)[:, None]
    qry = jnp.arange(MOBA_BLOCK, dtype=jnp.int32)[None, :]
    buckets = jnp.stack([_rel_bucket(t * MOBA_BLOCK + qry - key) for t in range(3)])
    return pl.pallas_call(
        _bias_kernel,
        grid=(ATTN_HEADS,),
        in_specs=[pl.BlockSpec(memory_space=pltpu.SMEM),
                  pl.BlockSpec((3, MOBA_BLOCK, MOBA_BLOCK), lambda h: (0, 0, 0))],
        out_specs=pl.BlockSpec((1, 3, MOBA_BLOCK, MOBA_BLOCK), lambda h: (h, 0, 0, 0)),
        out_shape=jax.ShapeDtypeStruct((ATTN_HEADS, 3, MOBA_BLOCK, MOBA_BLOCK), F32),
        compiler_params=_params(("parallel",)),
        name="attn_bias_tiles",
    )(rel_bias.astype(F32), buckets)


ATTN_HEADS_PER_STEP = 8


def _attn_kernel(q_ref, k_ref, vt_ref, bias_ref, o_ref, kmean_ref, sel_ref, acc_ref, *, nb):
    i = pl.program_id(2)
    scale = HEAD_DIM ** -0.5
    heads = range(ATTN_HEADS_PER_STEP)
    hsl = [slice(g * HEAD_DIM, (g + 1) * HEAD_DIM) for g in heads]

    @pl.when(i == 0)
    def _():
        for g in heads:
            for n in range(nb):
                kb = k_ref[n * MOBA_BLOCK:(n + 1) * MOBA_BLOCK, hsl[g]].astype(F32)
                kmean_ref[g, n:n + 1, :] = jnp.mean(kb, axis=0, keepdims=True)

    qs = [q_ref[:, hsl[g]] for g in heads]
    for g in heads:
        gate = lax.dot_general(kmean_ref[g].astype(BF16), qs[g], _NT_DIMS,
                               preferred_element_type=F32)
        blk = lax.broadcasted_iota(jnp.int32, gate.shape, 0)
        rank = jnp.zeros(gate.shape, F32)
        for m in range(nb):
            gm = gate[m:m + 1, :]
            beats = jnp.where(gm > gate, 1.0, jnp.where(gm == gate, jnp.where(m < blk, 1.0, 0.0), 0.0))
            rank = rank + beats * jnp.where(m < i, 1.0, 0.0)
        sel_ref[g] = jnp.where(blk < i, jnp.where(rank < MOBA_TOPK, 1.0, 0.0), 0.0)

    def scores(g, j, tile):
        koff = pl.multiple_of(j * MOBA_BLOCK, MOBA_BLOCK)
        kj = k_ref[pl.ds(koff, MOBA_BLOCK), hsl[g]]
        s = lax.dot_general(kj, qs[g], _NT_DIMS, preferred_element_type=F32)
        return s * scale + bias_ref[g, tile]

    s0 = [scores(g, i, 0) for g in heads]
    m0 = [jnp.max(s0[g], axis=0, keepdims=True) for g in heads]
    p0 = [jnp.exp(s0[g] - m0[g]) for g in heads]
    l0 = [jnp.sum(p0[g], axis=0, keepdims=True) for g in heads]
    for g in heads:
        acc_ref[g] = jnp.dot(vt_ref[i, hsl[g], :], p0[g].astype(BF16), preferred_element_type=F32)

    def body(j, carry):
        ms, ls = carry
        tile = jnp.minimum(i - j, 2)
        ss = [scores(g, j, tile) for g in heads]
        ss = [jnp.where(sel_ref[g, pl.ds(j, 1), :] > 0.5, ss[g], NEG) for g in heads]
        new_m = [jnp.maximum(ms[g], jnp.max(ss[g], axis=0, keepdims=True)) for g in heads]
        alpha = [jnp.exp(ms[g] - new_m[g]) for g in heads]
        ps = [jnp.exp(ss[g] - new_m[g]) for g in heads]
        new_l = [alpha[g] * ls[g] + jnp.sum(ps[g], axis=0, keepdims=True) for g in heads]
        pv = [jnp.dot(vt_ref[j, hsl[g], :], ps[g].astype(BF16), preferred_element_type=F32) for g in heads]
        for g in heads:
            acc_ref[g] = alpha[g] * acc_ref[g] + pv[g]
        return tuple(new_m), tuple(new_l)

    _, l_f = lax.fori_loop(0, i, body, (tuple(m0), tuple(l0)))
    for g in heads:
        o_ref[:, hsl[g]] = (acc_ref[g] / l_f[g]).T.astype(o_ref.dtype)


def _attention(qk, vt3, bias, batch, seq):
    nb = seq // MOBA_BLOCK
    t = batch * seq
    gh = ATTN_HEADS_PER_STEP
    width = gh * HEAD_DIM
    groups = ATTN_HEADS // gh
    return pl.pallas_call(
        functools.partial(_attn_kernel, nb=nb),
        grid=(batch, groups, nb),
        in_specs=[
            pl.BlockSpec((MOBA_BLOCK, width), lambda b, h, i: (b * nb + i, h)),
            pl.BlockSpec((seq, width), lambda b, h, i: (b, groups + h)),
            pl.BlockSpec((nb, width, MOBA_BLOCK), lambda b, h, i: (b, h, 0)),
            pl.BlockSpec((gh, 3, MOBA_BLOCK, MOBA_BLOCK), lambda b, h, i: (h, 0, 0, 0)),
        ],
        out_specs=pl.BlockSpec((MOBA_BLOCK, width), lambda b, h, i: (b * nb + i, h)),
        out_shape=jax.ShapeDtypeStruct((t, ATTN_WIDTH), BF16),
        scratch_shapes=[pltpu.VMEM((gh, nb, HEAD_DIM), F32),
                        pltpu.VMEM((gh, nb, MOBA_BLOCK), F32),
                        pltpu.VMEM((gh, HEAD_DIM, MOBA_BLOCK), F32)],
        compiler_params=_params(("parallel", "parallel", "arbitrary")),
        name="moba_attention",
    )(qk, qk, vt3, bias)


def _rglru_kernel(xr_ref, yr_ref, cw_ref, cb_ref, wa_ref, wx_ref, ba_ref, bx_ref, lam_ref,
                  o_ref, tail_ref, h_ref):
    c = pl.program_id(1)
    tc = xr_ref.shape[0]

    @pl.when(c == 0)
    def _():
        tail_ref[...] = jnp.zeros(tail_ref.shape, F32)
        h_ref[...] = jnp.zeros(h_ref.shape, F32)

    x = xr_ref[...]
    grp_shape = (tc // 8, 8, x.shape[1])
    x3 = x.reshape(grp_shape)
    tail3 = tail_ref[...].reshape(1, 8, x.shape[1])
    sub = lax.broadcasted_iota(jnp.int32, grp_shape, 1)
    conv = cb_ref[...] + cw_ref[CONV_WIDTH - 1:CONV_WIDTH, :] * x3
    for s in range(1, CONV_WIDTH):
        k = CONV_WIDTH - 1 - s
        rolled = pltpu.roll(x3, s, axis=1)
        rolled_prev = jnp.concatenate([pltpu.roll(tail3, s, axis=1), rolled[:-1]], axis=0)
        conv = conv + cw_ref[k:k + 1, :] * jnp.where(sub >= s, rolled, rolled_prev)
    conv = conv.reshape(x.shape)
    tail_ref[...] = x[tc - 8:, :]

    xb16 = conv.astype(BF16)
    pre_a, pre_x = [], []
    for g in range(LRU_BLOCKS):
        xs = xb16[:, g * LRU_BLOCK_DIM:(g + 1) * LRU_BLOCK_DIM]
        pre_a.append(jnp.dot(xs, wa_ref[g], preferred_element_type=F32))
        pre_x.append(jnp.dot(xs, wx_ref[g], preferred_element_type=F32))
    r_gate = jax.nn.sigmoid(jnp.concatenate(pre_a, axis=1) + ba_ref[...])
    i_gate = jax.nn.sigmoid(jnp.concatenate(pre_x, axis=1) + bx_ref[...])
    z = -lam_ref[...]
    softplus = jnp.maximum(z, 0.0) + jnp.log1p(jnp.exp(-jnp.abs(z)))
    log_a = (-LRU_C) * r_gate * softplus
    a = jnp.exp(log_a)
    th = jnp.tanh(log_a)
    y2 = -2.0 * th / (1.0 - th)
    u = jnp.where(y2 > 0.0, y2 * lax.rsqrt(y2), 0.0) * (i_gate * conv)

    grp_shape = (tc // 8, 8, a.shape[1])
    a = a.reshape(grp_shape)
    u = u.reshape(grp_shape)
    sub = lax.broadcasted_iota(jnp.int32, grp_shape, 1)
    s = 1
    while s < 8:
        a_sh = pltpu.roll(a, s, axis=1)
        u_sh = pltpu.roll(u, s, axis=1)
        valid = sub >= s
        u = jnp.where(valid, a * u_sh + u, u)
        a = jnp.where(valid, a * a_sh, a)
        s *= 2
    h = h_ref[...]
    gy = _gelu_tanh(yr_ref[...])
    for grp in range(tc // 8):
        rows = slice(grp * 8, (grp + 1) * 8)
        hh = a[grp] * h + u[grp]
        h = hh[7:8, :]
        o_ref[rows, :] = (hh * gy[rows, :]).astype(o_ref.dtype)
    h_ref[...] = h


def _rglru(r, conv_w, conv_b, wa, wx, ba, bx, lam, batch, seq, tc=256):
    t = batch * seq
    nc = seq // tc
    w = LRU_WIDTH
    row = lambda v: v.reshape(1, w).astype(F32)
    full2 = lambda shape: pl.BlockSpec(shape, lambda b, c: (0,) * len(shape))
    return pl.pallas_call(
        _rglru_kernel,
        grid=(batch, nc),
        in_specs=[pl.BlockSpec((tc, w), lambda b, c: (b * nc + c, 0)),
                  pl.BlockSpec((tc, w), lambda b, c: (b * nc + c, 1)),
                  full2((CONV_WIDTH, w)), full2((1, w)),
                  full2((LRU_BLOCKS, LRU_BLOCK_DIM, LRU_BLOCK_DIM)),
                  full2((LRU_BLOCKS, LRU_BLOCK_DIM, LRU_BLOCK_DIM)),
                  full2((1, w)), full2((1, w)), full2((1, w))],
        out_specs=pl.BlockSpec((tc, w), lambda b, c: (b * nc + c, 0)),
        out_shape=jax.ShapeDtypeStruct((t, w), BF16),
        scratch_shapes=[pltpu.VMEM((8, w), F32), pltpu.VMEM((1, w), F32)],
        compiler_params=_params(("parallel", "arbitrary")),
        name="rglru",
    )(r, r, conv_w.astype(F32), row(conv_b), wa.astype(BF16), wx.astype(BF16),
      row(ba), row(bx), row(lam))


def _merge_kernel(oa_ref, or_ref, g0_ref, g1_ref, wb0_ref, wb1_ref, wo_ref, x_ref, gn_ref,
                  x1_ref, xn_ref):
    pb0 = jnp.dot(oa_ref[...], wb0_ref[...], preferred_element_type=F32)
    pb1 = jnp.dot(or_ref[...], wb1_ref[...], preferred_element_type=F32)
    merged = jax.nn.sigmoid(g0_ref[...]) * pb0 + jax.nn.sigmoid(g1_ref[...]) * pb1
    x1 = x_ref[...] + jnp.dot(merged.astype(BF16), wo_ref[...], preferred_element_type=F32)
    x1_ref[...] = x1
    xn_ref[...] = _rms(x1, gn_ref[...]).astype(xn_ref.dtype)


def _merge_out(o_att, o_rec, gl, wb0, wb1, w_out, x, g_ffn, tm=256):
    t, d = x.shape
    cw = o_att.shape[1]
    resident = lambda shape: pl.BlockSpec(shape, lambda i: (0, 0), pipeline_mode=pl.Buffered(1))
    return pl.pallas_call(
        _merge_kernel,
        grid=(t // tm,),
        in_specs=[pl.BlockSpec((tm, cw), lambda i: (i, 0)),
                  pl.BlockSpec((tm, cw), lambda i: (i, 0)),
                  pl.BlockSpec((tm, d), lambda i: (i, 0)),
                  pl.BlockSpec((tm, d), lambda i: (i, 1)),
                  resident((cw, d)), resident((cw, d)), resident((d, d)),
                  pl.BlockSpec((tm, d), lambda i: (i, 0)),
                  pl.BlockSpec((1, d), lambda i: (0, 0))],
        out_specs=[pl.BlockSpec((tm, d), lambda i: (i, 0)),
                   pl.BlockSpec((tm, d), lambda i: (i, 0))],
        out_shape=[jax.ShapeDtypeStruct((t, d), F32), jax.ShapeDtypeStruct((t, d), BF16)],
        compiler_params=_params(("parallel",)),
        name="merge_out",
    )(o_att, o_rec, gl, gl, wb0, wb1, w_out, x, g_ffn.reshape(1, d).astype(F32))


def _oddeven_merge(lo, hi, r):
    step = r * 2
    if step < hi - lo:
        yield from _oddeven_merge(lo, hi, step)
        yield from _oddeven_merge(lo + r, hi, step)
        yield from [(i, i + r) for i in range(lo + r, hi - r, step)]
    else:
        yield (lo, lo + r)


def _oddeven_merge_sort(lo, hi):
    if hi - lo >= 1:
        mid = lo + (hi - lo) // 2
        yield from _oddeven_merge_sort(lo, mid)
        yield from _oddeven_merge_sort(mid + 1, hi)
        yield from _oddeven_merge(lo, hi, 1)


def _peer_scores_kernel(q_ref, keys_ref, thr_ref, e1_ref, s2_ref, e2_ref, s_ref, vals_ref, cand_ref):
    nhc = keys_ref.shape[0]
    k = PEER_TOPK
    for hc in range(nhc):
        qs = q_ref[:, hc * PEER_NKEYS:(hc + 1) * PEER_NKEYS]
        s_ref[hc] = lax.dot_general(keys_ref[hc], qs, _NT_DIMS, preferred_element_type=F32)

    n_lvl = PEER_NKEYS // 8
    network = list(_oddeven_merge_sort(0, n_lvl - 1))

    def top_values(hc, carry):
        x = s_ref[hc]
        lvl = [x[8 * v:8 * v + 8, :] for v in range(n_lvl)]
        for a, b in network:
            lvl[a], lvl[b] = jnp.maximum(lvl[a], lvl[b]), jnp.minimum(lvl[a], lvl[b])
        for r in range(k + 1):
            m = jnp.max(lvl[0], axis=0, keepdims=True)
            vals_ref[hc, r:r + 1, :] = m
            hit = lvl[0] == m
            for v in range(min(k - r, n_lvl)):
                below = lvl[v + 1] if v + 1 < n_lvl else LOWEST
                lvl[v] = jnp.where(hit, below, lvl[v])
        return carry

    lax.fori_loop(0, nhc, top_values, 0)

    def head_stats(h, carry):
        v1 = vals_ref[2 * h, 0:k + 1, :]
        v2 = vals_ref[2 * h + 1, 0:k + 1, :]
        cand_ref[...] = jnp.full(cand_ref.shape, LOWEST, F32)
        off = 0
        for a in range(k + 1):
            nb = (k + 1) // (a + 1)
            cand_ref[off:off + nb, :] = v1[a:a + 1, :] + v2[0:nb, :]
            off += nb
        cand = cand_ref[...]
        x = cand
        kth = None
        for r in range(k):
            kth = jnp.max(x, axis=0, keepdims=True)
            x = jnp.where(x == kth, LOWEST, x)
        nxt = jnp.max(x, axis=0, keepdims=True)
        m1 = v1[0:1, :]
        m2 = v2[0:1, :]
        tau = 0.5 * (kth + nxt)
        z = jnp.sum(jnp.where(cand >= tau, jnp.exp(cand - (m1 + m2)), 0.0), axis=0, keepdims=True)
        s1 = s_ref[2 * h]
        s2 = s_ref[2 * h + 1]
        thr_ref[h] = tau - s1
        e1_ref[h] = jnp.exp(s1 - m1)
        e2 = jnp.exp(s2 - m2) * (0.5 / z)
        for c in range(s2_ref.shape[1]):
            s2_ref[h, c] = s2[:, c * 128:(c + 1) * 128]
            e2_ref[h, c] = e2[:, c * 128:(c + 1) * 128]
        return carry

    lax.fori_loop(0, nhc // 2, head_stats, 0)


def _peer_scores(q, keys, tm=256):
    t, _ = q.shape
    nhc = keys.shape[0]
    out_spec = pl.BlockSpec((PEER_HEADS, PEER_NKEYS, tm), lambda i: (0, 0, i))
    out_shape = jax.ShapeDtypeStruct((PEER_HEADS, PEER_NKEYS, t), F32)
    chunk_spec = pl.BlockSpec((PEER_HEADS, tm // 128, PEER_NKEYS, 128), lambda i: (0, i, 0, 0))
    chunk_shape = jax.ShapeDtypeStruct((PEER_HEADS, t // 128, PEER_NKEYS, 128), F32)
    n_cand = sum((PEER_TOPK + 1) // (a + 1) for a in range(PEER_TOPK + 1))
    n_cand = -(-n_cand // 8) * 8
    return pl.pallas_call(
        _peer_scores_kernel,
        grid=(t // tm,),
        in_specs=[pl.BlockSpec((tm, nhc * PEER_NKEYS), lambda i: (i, 0)),
                  pl.BlockSpec(keys.shape, lambda i: (0, 0, 0))],
        out_specs=[out_spec, out_spec, chunk_spec, chunk_spec],
        out_shape=[out_shape, out_shape, chunk_shape, chunk_shape],
        scratch_shapes=[pltpu.VMEM((nhc, PEER_NKEYS, tm), F32),
                        pltpu.VMEM((nhc, PEER_TOPK + 8, tm), F32),
                        pltpu.VMEM((n_cand, tm), F32)],
        compiler_params=_params(("parallel",)),
        name="peer_scores",
    )(q, keys)


def _transpose_kernel(x_ref, o_ref):
    o_ref[...] = x_ref[...].T.astype(o_ref.dtype)


def _transposed_tiles(table, tn):
    e, d = table.shape
    return pl.pallas_call(
        _transpose_kernel,
        grid=(e // tn,),
        in_specs=[pl.BlockSpec((tn, d), lambda i: (i, 0))],
        out_specs=pl.BlockSpec((None, d, tn), lambda i: (i, 0, 0)),
        out_shape=jax.ShapeDtypeStruct((e // tn, d, tn), BF16),
        compiler_params=_params(("parallel",)),
        name="expert_value_tiles",
    )(table)


def _peer_dense_kernel(xn_ref, thr_ref, e1_ref, s2_ref, e2_ref, u_ref, vt_ref, acc_ref,
                       xnt_ref, act_ref, p_ref):
    j = pl.program_id(1)
    lane_chunks, tn, _ = act_ref.shape
    rows = tn // PEER_NKEYS

    @pl.when(j == 0)
    def _():
        acc_ref[...] = jnp.zeros(acc_ref.shape, F32)
        xnt_ref[...] = xn_ref[...].astype(F32).T.astype(xnt_ref.dtype)

    act = jnp.dot(u_ref[...], xnt_ref[...], preferred_element_type=F32)
    for lc in range(lane_chunks):
        act_ref[lc] = act[:, lc * 128:(lc + 1) * 128]

    def weigh(r, carry):
        i1 = j * rows + r
        roff = pl.multiple_of(r * PEER_NKEYS, PEER_NKEYS)
        thr_rows = [thr_ref[h, pl.ds(i1, 1), :] for h in range(PEER_HEADS)]
        e1_rows = [e1_ref[h, pl.ds(i1, 1), :] for h in range(PEER_HEADS)]
        for lc in range(lane_chunks):
            lanes = slice(lc * 128, (lc + 1) * 128)
            w = None
            for h in range(PEER_HEADS):
                w_h = jnp.where(s2_ref[h, lc] >= thr_rows[h][:, lanes],
                                e2_ref[h, lc] * e1_rows[h][:, lanes], 0.0)
                w = w_h if w is None else w + w_h
            a = act_ref[lc, pl.ds(roff, PEER_NKEYS), :]
            p_ref[pl.ds(roff, PEER_NKEYS), lanes] = (_gelu_tanh_x2(a) * w).astype(p_ref.dtype)
        return carry

    lax.fori_loop(0, rows, weigh, 0)
    res = jnp.dot(vt_ref[...], p_ref[...], preferred_element_type=F32)
    for lc in range(lane_chunks):
        acc_ref[lc] += res[:, lc * 128:(lc + 1) * 128]


def _peer_dense(xn, thr, e1, s2, e2, u_bf, vt_tiles, tm=512):
    t, d = xn.shape
    n_tiles, _, tn = vt_tiles.shape
    route_spec = pl.BlockSpec((PEER_HEADS, PEER_NKEYS, tm), lambda i, j: (0, 0, i))
    chunk_spec = pl.BlockSpec((PEER_HEADS, tm // 128, PEER_NKEYS, 128), lambda i, j: (0, i, 0, 0))
    return pl.pallas_call(
        _peer_dense_kernel,
        grid=(t // tm, n_tiles),
        in_specs=[pl.BlockSpec((tm, d), lambda i, j: (i, 0)),
                  route_spec, route_spec, chunk_spec, chunk_spec,
                  pl.BlockSpec((tn, d), lambda i, j: (j, 0)),
                  pl.BlockSpec((None, d, tn), lambda i, j: (j, 0, 0))],
        out_specs=pl.BlockSpec((tm // 128, d, 128), lambda i, j: (i, 0, 0)),
        out_shape=jax.ShapeDtypeStruct((t // 128, d, 128), F32),
        scratch_shapes=[pltpu.VMEM((d, tm), BF16), pltpu.VMEM((tm // 128, tn, 128), F32),
                        pltpu.VMEM((tn, tm), BF16)],
        compiler_params=_params(("parallel", "arbitrary")),
        name="peer_dense",
    )(xn, thr, e1, s2, e2, u_bf, vt_tiles)


def _final_kernel(x1_ref, pt_ref, g_ref, y_ref):
    for c in range(pt_ref.shape[0]):
        rows = slice(c * 128, (c + 1) * 128)
        y_ref[rows, :] = _rms(x1_ref[rows, :] + pt_ref[c].T, g_ref[...])


def _residual_norm(x1, peer_chunks, g, tm=256):
    t, d = x1.shape
    return pl.pallas_call(
        _final_kernel,
        grid=(t // tm,),
        in_specs=[pl.BlockSpec((tm, d), lambda i: (i, 0)),
                  pl.BlockSpec((tm // 128, d, 128), lambda i: (i, 0, 0)),
                  pl.BlockSpec((1, d), lambda i: (0, 0))],
        out_specs=pl.BlockSpec((tm, d), lambda i: (i, 0)),
        out_shape=jax.ShapeDtypeStruct((t, d), F32),
        compiler_params=_params(("parallel",)),
        name="residual_norm",
    )(x1, peer_chunks, g.reshape(1, d).astype(F32))


def kernel(x, norm_mix_g, w_in, conv_w, conv_b, lru_wa, lru_ba, lru_wx, lru_bx, lru_lambda,
           w_branch, w_out, rel_bias, norm_ffn_g, peer_wq, peer_keys, peer_u, peer_v, norm_final_g):
    batch, seq, d = x.shape
    t = batch * seq
    assert w_in.shape[0] == 1, "single-layer trunk"
    xt = x.reshape(t, d)
    bias = _bias_tiles(rel_bias)
    a_w = ATTN_WIDTH
    w_l = w_in[0]
    h = _rmsnorm(xt, norm_mix_g[0], BF16)
    qk = _matmul(h, w_l, 0, 2 * a_w, BF16, "proj_qk", tm=512, tn=1024)
    vt3 = _v_transposed(w_l[:, 2 * a_w:3 * a_w].T.astype(BF16), h)
    r = _matmul(h, w_l, 3 * a_w, 2 * LRU_WIDTH, F32, "proj_rec", tm=1024, tn=512)
    gl = _matmul(h, w_l, 3 * a_w + 2 * LRU_WIDTH, 2 * d, F32, "proj_gate")
    o_att = _attention(qk, vt3, bias, batch, seq)
    o_rec = _rglru(r, conv_w[0], conv_b[0], lru_wa[0], lru_wx[0], lru_ba[0], lru_bx[0],
                   lru_lambda[0], batch, seq)
    x1, xn = _merge_out(o_att, o_rec, gl, w_branch[0, 0].astype(BF16),
                        w_branch[0, 1].astype(BF16), w_out[0].astype(BF16), xt, norm_ffn_g[0])
    q = _matmul(xn, peer_wq[0], 0, peer_wq.shape[2], BF16, "peer_query", tm=2048, tn=512)
    keys = peer_keys[0].reshape(PEER_HEADS * 2, PEER_NKEYS, PEER_DKEY // 2).astype(BF16)
    thr, e1, s2, e2 = _peer_scores(q, keys)
    vt_tiles = _transposed_tiles(peer_v[0], PEER_EXPERT_TILE)
    peer_out = _peer_dense(xn, thr, e1, s2, e2, peer_u[0].astype(BF16), vt_tiles)
    y = _residual_norm(x1, peer_out, norm_final_g)
    return y.reshape(batch, seq, d)
```

```python
import functools
import math

import jax
import jax.numpy as jnp
from jax import lax
from jax.experimental import pallas as pl
from jax.experimental.pallas import tpu as pltpu

ATTN_HEADS = 8
HEAD_DIM = 128
ATTN_WIDTH = ATTN_HEADS * HEAD_DIM
MOBA_BLOCK = 256
MOBA_TOPK = 3
LRU_WIDTH = 1024
LRU_BLOCKS = 8
LRU_BLOCK_DIM = LRU_WIDTH // LRU_BLOCKS
CONV_WIDTH = 4
LRU_C = 8.0
REL_BUCKETS = 32
REL_MAX_DIST = 128
PEER_HEADS = 8
PEER_NKEYS = 128
PEER_DKEY = 256
PEER_TOPK = 16
EPS = 1e-6
NEG = -1e30
LOWEST = -3.0e38
PEER_EXPERT_TILE = 1024

V7X_VMEM_LIMIT_BYTES = 56 * 1024 * 1024

F32 = jnp.float32
BF16 = jnp.bfloat16

_NT_DIMS = (((1,), (1,)), ((), ()))


def _params(semantics):
    return pltpu.CompilerParams(dimension_semantics=semantics,
                                vmem_limit_bytes=V7X_VMEM_LIMIT_BYTES)


def _gelu_tanh_x2(x):
    c = math.sqrt(2.0 / math.pi)
    return x * (1.0 + jnp.tanh(x * (c + (0.044715 * c) * (x * x))))


def _gelu_tanh(x):
    return 0.5 * _gelu_tanh_x2(x)


def _rms(x, g):
    ms = jnp.mean(x * x, axis=-1, keepdims=True)
    return x * lax.rsqrt(ms + EPS) * g


def _rmsnorm_kernel(x_ref, g_ref, o_ref):
    o_ref[...] = _rms(x_ref[...], g_ref[...]).astype(o_ref.dtype)


def _rmsnorm(x, g, out_dtype, tm=1024):
    t, d = x.shape
    return pl.pallas_call(
        _rmsnorm_kernel,
        grid=(t // tm,),
        in_specs=[pl.BlockSpec((tm, d), lambda i: (i, 0)),
                  pl.BlockSpec((1, d), lambda i: (0, 0))],
        out_specs=pl.BlockSpec((tm, d), lambda i: (i, 0)),
        out_shape=jax.ShapeDtypeStruct((t, d), out_dtype),
        compiler_params=_params(("parallel",)),
        name="rmsnorm",
    )(x, g.reshape(1, d))


def _mm_kernel(a_ref, b_ref, o_ref, w_ref):
    @pl.when(pl.program_id(1) == 0)
    def _():
        w_ref[...] = b_ref[...].astype(w_ref.dtype)

    o_ref[...] = jnp.dot(a_ref[...], w_ref[...], preferred_element_type=F32).astype(o_ref.dtype)


def _matmul(a, b, col_start, n, out_dtype, name, tm=1024, tn=1024):
    m, k = a.shape
    assert col_start % tn == 0 and n % tn == 0
    col_blk = col_start // tn
    return pl.pallas_call(
        _mm_kernel,
        grid=(n // tn, m // tm),
        in_specs=[pl.BlockSpec((tm, k), lambda j, i: (i, 0)),
                  pl.BlockSpec((k, tn), lambda j, i: (0, col_blk + j))],
        out_specs=pl.BlockSpec((tm, tn), lambda j, i: (i, j)),
        out_shape=jax.ShapeDtypeStruct((m, n), out_dtype),
        scratch_shapes=[pltpu.VMEM((k, tn), BF16)],
        compiler_params=_params(("parallel", "arbitrary")),
        name=name,
    )(a, b)


def _vt_kernel(w_ref, h_ref, o_ref):
    res = lax.dot_general(w_ref[...], h_ref[...], _NT_DIMS, preferred_element_type=F32)
    for t in range(o_ref.shape[0]):
        o_ref[t] = res[:, t * MOBA_BLOCK:(t + 1) * MOBA_BLOCK].astype(o_ref.dtype)


def _v_transposed(w_t, h, tm=1024):
    c, k = w_t.shape
    t, _ = h.shape
    nb = tm // MOBA_BLOCK
    return pl.pallas_call(
        _vt_kernel,
        grid=(t // tm,),
        in_specs=[pl.BlockSpec((c, k), lambda i: (0, 0)),
                  pl.BlockSpec((tm, k), lambda i: (i, 0))],
        out_specs=pl.BlockSpec((nb, c, MOBA_BLOCK), lambda i: (i, 0, 0)),
        out_shape=jax.ShapeDtypeStruct((t // MOBA_BLOCK, c, MOBA_BLOCK), BF16),
        compiler_params=_params(("parallel",)),
        name="v_transposed",
    )(w_t, h)


def _rel_bucket(dist):
    n = jnp.maximum(dist, 0)
    max_exact = REL_BUCKETS // 2
    nf = jnp.maximum(n, 1).astype(F32)
    large = max_exact + (jnp.log(nf / max_exact) / math.log(REL_MAX_DIST / max_exact)
                         * (REL_BUCKETS - max_exact)).astype(jnp.int32)
    large = jnp.minimum(large, REL_BUCKETS - 1)
    return jnp.where(n < max_exact, n, large)


def _bias_kernel(relb_ref, bucket_ref, o_ref):
    h = pl.program_id(0)
    for t in range(3):
        bk = bucket_ref[t]
        acc = jnp.zeros(bk.shape, F32)
        for b in range(REL_BUCKETS):
            acc = jnp.where(bk == b, relb_ref[b, h], acc)
        if t == 0:
            key = lax.broadcasted_iota(jnp.int32, bk.shape, 0)
            qry = lax.broadcasted_iota(jnp.int32, bk.shape, 1)
            acc = jnp.where(key <= qry, acc, NEG)
        o_ref[0, t] = acc


def _bias_tiles(rel_bias):
    assert REL_MAX_DIST <= MOBA_BLOCK
    key = jnp.arange(MOBA_BLOCK, dtype=jnp.int32)[:, None]
    qry = jnp.arange(MOBA_BLOCK, dtype=jnp.int32)[None, :]
    buckets = jnp.stack([_rel_bucket(t * MOBA_BLOCK + qry - key) for t in range(3)])
    return pl.pallas_call(
        _bias_kernel,
        grid=(ATTN_HEADS,),
        in_specs=[pl.BlockSpec(memory_space=pltpu.SMEM),
                  pl.BlockSpec((3, MOBA_BLOCK, MOBA_BLOCK), lambda h: (0, 0, 0))],
        out_specs=pl.BlockSpec((1, 3, MOBA_BLOCK, MOBA_BLOCK), lambda h: (h, 0, 0, 0)),
        out_shape=jax.ShapeDtypeStruct((ATTN_HEADS, 3, MOBA_BLOCK, MOBA_BLOCK), F32),
        compiler_params=_params(("parallel",)),
        name="attn_bias_tiles",
    )(rel_bias.astype(F32), buckets)


ATTN_HEADS_PER_STEP = 8


def _attn_kernel(q_ref, k_ref, vt_ref, bias_ref, o_ref, kmean_ref, sel_ref, acc_ref, *, nb):
    i = pl.program_id(2)
    scale = HEAD_DIM ** -0.5
    heads = range(ATTN_HEADS_PER_STEP)
    hsl = [slice(g * HEAD_DIM, (g + 1) * HEAD_DIM) for g in heads]

    @pl.when(i == 0)
    def _():
        for g in heads:
            for n in range(nb):
                kb = k_ref[n * MOBA_BLOCK:(n + 1) * MOBA_BLOCK, hsl[g]].astype(F32)
                kmean_ref[g, n:n + 1, :] = jnp.mean(kb, axis=0, keepdims=True)

    qs = [q_ref[:, hsl[g]] for g in heads]
    for g in heads:
        gate = lax.dot_general(kmean_ref[g].astype(BF16), qs[g], _NT_DIMS,
                               preferred_element_type=F32)
        blk = lax.broadcasted_iota(jnp.int32, gate.shape, 0)
        rank = jnp.zeros(gate.shape, F32)
        for m in range(nb):
            gm = gate[m:m + 1, :]
            beats = jnp.where(gm > gate, 1.0, jnp.where(gm == gate, jnp.where(m < blk, 1.0, 0.0), 0.0))
            rank = rank + beats * jnp.where(m < i, 1.0, 0.0)
        sel_ref[g] = jnp.where(blk < i, jnp.where(rank < MOBA_TOPK, 1.0, 0.0), 0.0)

    def scores(g, j, tile):
        koff = pl.multiple_of(j * MOBA_BLOCK, MOBA_BLOCK)
        kj = k_ref[pl.ds(koff, MOBA_BLOCK), hsl[g]]
        s = lax.dot_general(kj, qs[g], _NT_DIMS, preferred_element_type=F32)
        return s * scale + bias_ref[g, tile]

    s0 = [scores(g, i, 0) for g in heads]
    m0 = [jnp.max(s0[g], axis=0, keepdims=True) for g in heads]
    p0 = [jnp.exp(s0[g] - m0[g]) for g in heads]
    l0 = [jnp.sum(p0[g], axis=0, keepdims=True) for g in heads]
    for g in heads:
        acc_ref[g] = jnp.dot(vt_ref[i, hsl[g], :], p0[g].astype(BF16), preferred_element_type=F32)

    def body(j, carry):
        ms, ls = carry
        tile = jnp.minimum(i - j, 2)
        ss = [scores(g, j, tile) for g in heads]
        ss = [jnp.where(sel_ref[g, pl.ds(j, 1), :] > 0.5, ss[g], NEG) for g in heads]
        new_m = [jnp.maximum(ms[g], jnp.max(ss[g], axis=0, keepdims=True)) for g in heads]
        alpha = [jnp.exp(ms[g] - new_m[g]) for g in heads]
        ps = [jnp.exp(ss[g] - new_m[g]) for g in heads]
        new_l = [alpha[g] * ls[g] + jnp.sum(ps[g], axis=0, keepdims=True) for g in heads]
        pv = [jnp.dot(vt_ref[j, hsl[g], :], ps[g].astype(BF16), preferred_element_type=F32) for g in heads]
        for g in heads:
            acc_ref[g] = alpha[g] * acc_ref[g] + pv[g]
        return tuple(new_m), tuple(new_l)

    _, l_f = lax.fori_loop(0, i, body, (tuple(m0), tuple(l0)))
    for g in heads:
        o_ref[:, hsl[g]] = (acc_ref[g] / l_f[g]).T.astype(o_ref.dtype)


def _attention(qk, vt3, bias, batch, seq):
    nb = seq // MOBA_BLOCK
    t = batch * seq
    gh = ATTN_HEADS_PER_STEP
    width = gh * HEAD_DIM
    groups = ATTN_HEADS // gh
    return pl.pallas_call(
        functools.partial(_attn_kernel, nb=nb),
        grid=(batch, groups, nb),
        in_specs=[
            pl.BlockSpec((MOBA_BLOCK, width), lambda b, h, i: (b * nb + i, h)),
            pl.BlockSpec((seq, width), lambda b, h, i: (b, groups + h)),
            pl.BlockSpec((nb, width, MOBA_BLOCK), lambda b, h, i: (b, h, 0)),
            pl.BlockSpec((gh, 3, MOBA_BLOCK, MOBA_BLOCK), lambda b, h, i: (h, 0, 0, 0)),
        ],
        out_specs=pl.BlockSpec((MOBA_BLOCK, width), lambda b, h, i: (b * nb + i, h)),
        out_shape=jax.ShapeDtypeStruct((t, ATTN_WIDTH), BF16),
        scratch_shapes=[pltpu.VMEM((gh, nb, HEAD_DIM), F32),
                        pltpu.VMEM((gh, nb, MOBA_BLOCK), F32),
                        pltpu.VMEM((gh, HEAD_DIM, MOBA_BLOCK), F32)],
        compiler_params=_params(("parallel", "parallel", "arbitrary")),
        name="moba_attention",
    )(qk, qk, vt3, bias)


def _rglru_kernel(xr_ref, yr_ref, cw_ref, cb_ref, wa_ref, wx_ref, ba_ref, bx_ref, lam_ref,
                  o_ref, tail_ref, h_ref):
    c = pl.program_id(1)
    tc = xr_ref.shape[0]

    @pl.when(c == 0)
    def _():
        tail_ref[...] = jnp.zeros(tail_ref.shape, F32)
        h_ref[...] = jnp.zeros(h_ref.shape, F32)

    x = xr_ref[...]
    grp_shape = (tc // 8, 8, x.shape[1])
    x3 = x.reshape(grp_shape)
    tail3 = tail_ref[...].reshape(1, 8, x.shape[1])
    sub = lax.broadcasted_iota(jnp.int32, grp_shape, 1)
    conv = cb_ref[...] + cw_ref[CONV_WIDTH - 1:CONV_WIDTH, :] * x3
    for s in range(1, CONV_WIDTH):
        k = CONV_WIDTH - 1 - s
        rolled = pltpu.roll(x3, s, axis=1)
        rolled_prev = jnp.concatenate([pltpu.roll(tail3, s, axis=1), rolled[:-1]], axis=0)
        conv = conv + cw_ref[k:k + 1, :] * jnp.where(sub >= s, rolled, rolled_prev)
    conv = conv.reshape(x.shape)
    tail_ref[...] = x[tc - 8:, :]

    xb16 = conv.astype(BF16)
    pre_a, pre_x = [], []
    for g in range(LRU_BLOCKS):
        xs = xb16[:, g * LRU_BLOCK_DIM:(g + 1) * LRU_BLOCK_DIM]
        pre_a.append(jnp.dot(xs, wa_ref[g], preferred_element_type=F32))
        pre_x.append(jnp.dot(xs, wx_ref[g], preferred_element_type=F32))
    r_gate = jax.nn.sigmoid(jnp.concatenate(pre_a, axis=1) + ba_ref[...])
    i_gate = jax.nn.sigmoid(jnp.concatenate(pre_x, axis=1) + bx_ref[...])
    z = -lam_ref[...]
    softplus = jnp.maximum(z, 0.0) + jnp.log1p(jnp.exp(-jnp.abs(z)))
    log_a = (-LRU_C) * r_gate * softplus
    a = jnp.exp(log_a)
    th = jnp.tanh(log_a)
    y2 = -2.0 * th / (1.0 - th)
    u = jnp.where(y2 > 0.0, y2 * lax.rsqrt(y2), 0.0) * (i_gate * conv)

    grp_shape = (tc // 8, 8, a.shape[1])
    a = a.reshape(grp_shape)
    u = u.reshape(grp_shape)
    sub = lax.broadcasted_iota(jnp.int32, grp_shape, 1)
    s = 1
    while s < 8:
        a_sh = pltpu.roll(a, s, axis=1)
        u_sh = pltpu.roll(u, s, axis=1)
        valid = sub >= s
        u = jnp.where(valid, a * u_sh + u, u)
        a = jnp.where(valid, a * a_sh, a)
        s *= 2
    h = h_ref[...]
    gy = _gelu_tanh(yr_ref[...])
    for grp in range(tc // 8):
        rows = slice(grp * 8, (grp + 1) * 8)
        hh = a[grp] * h + u[grp]
        h = hh[7:8, :]
        o_ref[rows, :] = (hh * gy[rows, :]).astype(o_ref.dtype)
    h_ref[...] = h


def _rglru(r, conv_w, conv_b, wa, wx, ba, bx, lam, batch, seq, tc=256):
    t = batch * seq
    nc = seq // tc
    w = LRU_WIDTH
    row = lambda v: v.reshape(1, w).astype(F32)
    full2 = lambda shape: pl.BlockSpec(shape, lambda b, c: (0,) * len(shape))
    return pl.pallas_call(
        _rglru_kernel,
        grid=(batch, nc),
        in_specs=[pl.BlockSpec((tc, w), lambda b, c: (b * nc + c, 0)),
                  pl.BlockSpec((tc, w), lambda b, c: (b * nc + c, 1)),
                  full2((CONV_WIDTH, w)), full2((1, w)),
                  full2((LRU_BLOCKS, LRU_BLOCK_DIM, LRU_BLOCK_DIM)),
                  full2((LRU_BLOCKS, LRU_BLOCK_DIM, LRU_BLOCK_DIM)),
                  full2((1, w)), full2((1, w)), full2((1, w))],
        out_specs=pl.BlockSpec((tc, w), lambda b, c: (b * nc + c, 0)),
        out_shape=jax.ShapeDtypeStruct((t, w), BF16),
        scratch_shapes=[pltpu.VMEM((8, w), F32), pltpu.VMEM((1, w), F32)],
        compiler_params=_params(("parallel", "arbitrary")),
        name="rglru",
    )(r, r, conv_w.astype(F32), row(conv_b), wa.astype(BF16), wx.astype(BF16),
      row(ba), row(bx), row(lam))


def _merge_kernel(oa_ref, or_ref, g0_ref, g1_ref, wb0_ref, wb1_ref, wo_ref, x_ref, gn_ref,
                  x1_ref, xn_ref):
    pb0 = jnp.dot(oa_ref[...], wb0_ref[...], preferred_element_type=F32)
    pb1 = jnp.dot(or_ref[...], wb1_ref[...], preferred_element_type=F32)
    merged = jax.nn.sigmoid(g0_ref[...]) * pb0 + jax.nn.sigmoid(g1_ref[...]) * pb1
    x1 = x_ref[...] + jnp.dot(merged.astype(BF16), wo_ref[...], preferred_element_type=F32)
    x1_ref[...] = x1
    xn_ref[...] = _rms(x1, gn_ref[...]).astype(xn_ref.dtype)


def _merge_out(o_att, o_rec, gl, wb0, wb1, w_out, x, g_ffn, tm=256):
    t, d = x.shape
    cw = o_att.shape[1]
    resident = lambda shape: pl.BlockSpec(shape, lambda i: (0, 0), pipeline_mode=pl.Buffered(1))
    return pl.pallas_call(
        _merge_kernel,
        grid=(t // tm,),
        in_specs=[pl.BlockSpec((tm, cw), lambda i: (i, 0)),
                  pl.BlockSpec((tm, cw), lambda i: (i, 0)),
                  pl.BlockSpec((tm, d), lambda i: (i, 0)),
                  pl.BlockSpec((tm, d), lambda i: (i, 1)),
                  resident((cw, d)), resident((cw, d)), resident((d, d)),
                  pl.BlockSpec((tm, d), lambda i: (i, 0)),
                  pl.BlockSpec((1, d), lambda i: (0, 0))],
        out_specs=[pl.BlockSpec((tm, d), lambda i: (i, 0)),
                   pl.BlockSpec((tm, d), lambda i: (i, 0))],
        out_shape=[jax.ShapeDtypeStruct((t, d), F32), jax.ShapeDtypeStruct((t, d), BF16)],
        compiler_params=_params(("parallel",)),
        name="merge_out",
    )(o_att, o_rec, gl, gl, wb0, wb1, w_out, x, g_ffn.reshape(1, d).astype(F32))


def _oddeven_merge(lo, hi, r):
    step = r * 2
    if step < hi - lo:
        yield from _oddeven_merge(lo, hi, step)
        yield from _oddeven_merge(lo + r, hi, step)
        yield from [(i, i + r) for i in range(lo + r, hi - r, step)]
    else:
        yield (lo, lo + r)


def _oddeven_merge_sort(lo, hi):
    if hi - lo >= 1:
        mid = lo + (hi - lo) // 2
        yield from _oddeven_merge_sort(lo, mid)
        yield from _oddeven_merge_sort(mid + 1, hi)
        yield from _oddeven_merge(lo, hi, 1)


def _peer_scores_kernel(q_ref, keys_ref, thr_ref, e1_ref, s2_ref, e2_ref, s_ref, vals_ref, cand_ref):
    nhc = keys_ref.shape[0]
    k = PEER_TOPK
    for hc in range(nhc):
        qs = q_ref[:, hc * PEER_NKEYS:(hc + 1) * PEER_NKEYS]
        s_ref[hc] = lax.dot_general(keys_ref[hc], qs, _NT_DIMS, preferred_element_type=F32)

    n_lvl = PEER_NKEYS // 8
    network = list(_oddeven_merge_sort(0, n_lvl - 1))

    def top_values(hc, carry):
        x = s_ref[hc]
        lvl = [x[8 * v:8 * v + 8, :] for v in range(n_lvl)]
        for a, b in network:
            lvl[a], lvl[b] = jnp.maximum(lvl[a], lvl[b]), jnp.minimum(lvl[a], lvl[b])
        for r in range(k + 1):
            m = jnp.max(lvl[0], axis=0, keepdims=True)
            vals_ref[hc, r:r + 1, :] = m
            hit = lvl[0] == m
            for v in range(min(k - r, n_lvl)):
                below = lvl[v + 1] if v + 1 < n_lvl else LOWEST
                lvl[v] = jnp.where(hit, below, lvl[v])
        return carry

    lax.fori_loop(0, nhc, top_values, 0)

    def head_stats(h, carry):
        v1 = vals_ref[2 * h, 0:k + 1, :]
        v2 = vals_ref[2 * h + 1, 0:k + 1, :]
        cand_ref[...] = jnp.full(cand_ref.shape, LOWEST, F32)
        off = 0
        for a in range(k + 1):
            nb = (k + 1) // (a + 1)
            cand_ref[off:off + nb, :] = v1[a:a + 1, :] + v2[0:nb, :]
            off += nb
        cand = cand_ref[...]
        x = cand
        kth = None
        for r in range(k):
            kth = jnp.max(x, axis=0, keepdims=True)
            x = jnp.where(x == kth, LOWEST, x)
        nxt = jnp.max(x, axis=0, keepdims=True)
        m1 = v1[0:1, :]
        m2 = v2[0:1, :]
        tau = 0.5 * (kth + nxt)
        z = jnp.sum(jnp.where(cand >= tau, jnp.exp(cand - (m1 + m2)), 0.0), axis=0, keepdims=True)
        s1 = s_ref[2 * h]
        s2 = s_ref[2 * h + 1]
        thr_ref[h] = tau - s1
        e1_ref[h] = jnp.exp(s1 - m1)
        e2 = jnp.exp(s2 - m2) * (0.5 / z)
        for c in range(s2_ref.shape[1]):
            s2_ref[h, c] = s2[:, c * 128:(c + 1) * 128]
            e2_ref[h, c] = e2[:, c * 128:(c + 1) * 128]
        return carry

    lax.fori_loop(0, nhc // 2, head_stats, 0)


def _peer_scores(q, keys, tm=256):
    t, _ = q.shape
    nhc = keys.shape[0]
    out_spec = pl.BlockSpec((PEER_HEADS, PEER_NKEYS, tm), lambda i: (0, 0, i))
    out_shape = jax.ShapeDtypeStruct((PEER_HEADS, PEER_NKEYS, t), F32)
    chunk_spec = pl.BlockSpec((PEER_HEADS, tm // 128, PEER_NKEYS, 128), lambda i: (0, i, 0, 0))
    chunk_shape = jax.ShapeDtypeStruct((PEER_HEADS, t // 128, PEER_NKEYS, 128), F32)
    n_cand = sum((PEER_TOPK + 1) // (a + 1) for a in range(PEER_TOPK + 1))
    n_cand = -(-n_cand // 8) * 8
    return pl.pallas_call(
        _peer_scores_kernel,
        grid=(t // tm,),
        in_specs=[pl.BlockSpec((tm, nhc * PEER_NKEYS), lambda i: (i, 0)),
                  pl.BlockSpec(keys.shape, lambda i: (0, 0, 0))],
        out_specs=[out_spec, out_spec, chunk_spec, chunk_spec],
        out_shape=[out_shape, out_shape, chunk_shape, chunk_shape],
        scratch_shapes=[pltpu.VMEM((nhc, PEER_NKEYS, tm), F32),
                        pltpu.VMEM((nhc, PEER_TOPK + 8, tm), F32),
                        pltpu.VMEM((n_cand, tm), F32)],
        compiler_params=_params(("parallel",)),
        name="peer_scores",
    )(q, keys)


def _transpose_kernel(x_ref, o_ref):
    o_ref[...] = x_ref[...].T.astype(o_ref.dtype)


def _transposed_tiles(table, tn):
    e, d = table.shape
    return pl.pallas_call(
        _transpose_kernel,
        grid=(e // tn,),
        in_specs=[pl.BlockSpec((tn, d), lambda i: (i, 0))],
        out_specs=pl.BlockSpec((None, d, tn), lambda i: (i, 0, 0)),
        out_shape=jax.ShapeDtypeStruct((e // tn, d, tn), BF16),
        compiler_params=_params(("parallel",)),
        name="expert_value_tiles",
    )(table)


def _peer_dense_kernel(xn_ref, thr_ref, e1_ref, s2_ref, e2_ref, u_ref, vt_ref, acc_ref,
                       xnt_ref, act_ref, p_ref):
    j = pl.program_id(1)
    lane_chunks, tn, _ = act_ref.shape
    rows = tn // PEER_NKEYS

    @pl.when(j == 0)
    def _():
        acc_ref[...] = jnp.zeros(acc_ref.shape, F32)
        xnt_ref[...] = xn_ref[...].astype(F32).T.astype(xnt_ref.dtype)

    act = jnp.dot(u_ref[...], xnt_ref[...], preferred_element_type=F32)
    for lc in range(lane_chunks):
        act_ref[lc] = act[:, lc * 128:(lc + 1) * 128]

    def weigh(r, carry):
        i1 = j * rows + r
        roff = pl.multiple_of(r * PEER_NKEYS, PEER_NKEYS)
        thr_rows = [thr_ref[h, pl.ds(i1, 1), :] for h in range(PEER_HEADS)]
        e1_rows = [e1_ref[h, pl.ds(i1, 1), :] for h in range(PEER_HEADS)]
        for lc in range(lane_chunks):
            lanes = slice(lc * 128, (lc + 1) * 128)
            w = None
            for h in range(PEER_HEADS):
                w_h = jnp.where(s2_ref[h, lc] >= thr_rows[h][:, lanes],
                                e2_ref[h, lc] * e1_rows[h][:, lanes], 0.0)
                w = w_h if w is None else w + w_h
            a = act_ref[lc, pl.ds(roff, PEER_NKEYS), :]
            p_ref[pl.ds(roff, PEER_NKEYS), lanes] = (_gelu_tanh_x2(a) * w).astype(p_ref.dtype)
        return carry

    lax.fori_loop(0, rows, weigh, 0)
    res = jnp.dot(vt_ref[...], p_ref[...], preferred_element_type=F32)
    for lc in range(lane_chunks):
        acc_ref[lc] += res[:, lc * 128:(lc + 1) * 128]


def _peer_dense(xn, thr, e1, s2, e2, u_bf, vt_tiles, tm=512):
    t, d = xn.shape
    n_tiles, _, tn = vt_tiles.shape
    route_spec = pl.BlockSpec((PEER_HEADS, PEER_NKEYS, tm), lambda i, j: (0, 0, i))
    chunk_spec = pl.BlockSpec((PEER_HEADS, tm // 128, PEER_NKEYS, 128), lambda i, j: (0, i, 0, 0))
    return pl.pallas_call(
        _peer_dense_kernel,
        grid=(t // tm, n_tiles),
        in_specs=[pl.BlockSpec((tm, d), lambda i, j: (i, 0)),
                  route_spec, route_spec, chunk_spec, chunk_spec,
                  pl.BlockSpec((tn, d), lambda i, j: (j, 0)),
                  pl.BlockSpec((None, d, tn), lambda i, j: (j, 0, 0))],
        out_specs=pl.BlockSpec((tm // 128, d, 128), lambda i, j: (i, 0, 0)),
        out_shape=jax.ShapeDtypeStruct((t // 128, d, 128), F32),
        scratch_shapes=[pltpu.VMEM((d, tm), BF16), pltpu.VMEM((tm // 128, tn, 128), F32),
                        pltpu.VMEM((tn, tm), BF16)],
        compiler_params=_params(("parallel", "arbitrary")),
        name="peer_dense",
    )(xn, thr, e1, s2, e2, u_bf, vt_tiles)


def _final_kernel(x1_ref, pt_ref, g_ref, y_ref):
    for c in range(pt_ref.shape[0]):
        rows = slice(c * 128, (c + 1) * 128)
        y_ref[rows, :] = _rms(x1_ref[rows, :] + pt_ref[c].T, g_ref[...])


def _residual_norm(x1, peer_chunks, g, tm=512):
    t, d = x1.shape
    return pl.pallas_call(
        _final_kernel,
        grid=(t // tm,),
        in_specs=[pl.BlockSpec((tm, d), lambda i: (i, 0)),
                  pl.BlockSpec((tm // 128, d, 128), lambda i: (i, 0, 0)),
                  pl.BlockSpec((1, d), lambda i: (0, 0))],
        out_specs=pl.BlockSpec((tm, d), lambda i: (i, 0)),
        out_shape=jax.ShapeDtypeStruct((t, d), F32),
        compiler_params=_params(("parallel",)),
        name="residual_norm",
    )(x1, peer_chunks, g.reshape(1, d).astype(F32))


def kernel(x, norm_mix_g, w_in, conv_w, conv_b, lru_wa, lru_ba, lru_wx, lru_bx, lru_lambda,
           w_branch, w_out, rel_bias, norm_ffn_g, peer_wq, peer_keys, peer_u, peer_v, norm_final_g):
    batch, seq, d = x.shape
    t = batch * seq
    assert w_in.shape[0] == 1, "single-layer trunk"
    xt = x.reshape(t, d)
    bias = _bias_tiles(rel_bias)
    a_w = ATTN_WIDTH
    w_l = w_in[0]
    h = _rmsnorm(xt, norm_mix_g[0], BF16)
    qk = _matmul(h, w_l, 0, 2 * a_w, BF16, "proj_qk")
    vt3 = _v_transposed(w_l[:, 2 * a_w:3 * a_w].T.astype(BF16), h)
    r = _matmul(h, w_l, 3 * a_w, 2 * LRU_WIDTH, F32, "proj_rec")
    gl = _matmul(h, w_l, 3 * a_w + 2 * LRU_WIDTH, 2 * d, F32, "proj_gate")
    o_att = _attention(qk, vt3, bias, batch, seq)
    o_rec = _rglru(r, conv_w[0], conv_b[0], lru_wa[0], lru_wx[0], lru_ba[0], lru_bx[0],
                   lru_lambda[0], batch, seq)
    x1, xn = _merge_out(o_att, o_rec, gl, w_branch[0, 0].astype(BF16),
                        w_branch[0, 1].astype(BF16), w_out[0].astype(BF16), xt, norm_ffn_g[0])
    q = _matmul(xn, peer_wq[0], 0, peer_wq.shape[2], BF16, "peer_query")
    keys = peer_keys[0].reshape(PEER_HEADS * 2, PEER_NKEYS, PEER_DKEY // 2).astype(BF16)
    thr, e1, s2, e2 = _peer_scores(q, keys)
    vt_tiles = _transposed_tiles(peer_v[0], PEER_EXPERT_TILE)
    peer_out = _peer_dense(xn, thr, e1, s2, e2, peer_u[0].astype(BF16), vt_tiles)
    y = _residual_norm(x1, peer_out, norm_final_g)
    return y.reshape(batch, seq, d)
```

```python
import functools
import math

import jax
import jax.numpy as jnp
from jax import lax
from jax.experimental import pallas as pl
from jax.experimental.pallas import tpu as pltpu

ATTN_HEADS = 8
HEAD_DIM = 128
ATTN_WIDTH = ATTN_HEADS * HEAD_DIM
MOBA_BLOCK = 256
MOBA_TOPK = 3
LRU_WIDTH = 1024
LRU_BLOCKS = 8
LRU_BLOCK_DIM = LRU_WIDTH // LRU_BLOCKS
CONV_WIDTH = 4
LRU_C = 8.0
REL_BUCKETS = 32
REL_MAX_DIST = 128
PEER_HEADS = 8
PEER_NKEYS = 128
PEER_DKEY = 256
PEER_TOPK = 16
EPS = 1e-6
NEG = -1e30
LOWEST = -3.0e38
PEER_EXPERT_TILE = 1024

V7X_VMEM_LIMIT_BYTES = 56 * 1024 * 1024

F32 = jnp.float32
BF16 = jnp.bfloat16

_NT_DIMS = (((1,), (1,)), ((), ()))


def _params(semantics):
    return pltpu.CompilerParams(dimension_semantics=semantics,
                                vmem_limit_bytes=V7X_VMEM_LIMIT_BYTES)


def _gelu_tanh_x2(x):
    c = math.sqrt(2.0 / math.pi)
    return x * (1.0 + jnp.tanh(x * (c + (0.044715 * c) * (x * x))))


def _gelu_tanh(x):
    return 0.5 * _gelu_tanh_x2(x)


def _rms(x, g):
    ms = jnp.mean(x * x, axis=-1, keepdims=True)
    return x * lax.rsqrt(ms + EPS) * g


def _rmsnorm_kernel(x_ref, g_ref, o_ref):
    o_ref[...] = _rms(x_ref[...], g_ref[...]).astype(o_ref.dtype)


def _rmsnorm(x, g, out_dtype, tm=1024):
    t, d = x.shape
    return pl.pallas_call(
        _rmsnorm_kernel,
        grid=(t // tm,),
        in_specs=[pl.BlockSpec((tm, d), lambda i: (i, 0)),
                  pl.BlockSpec((1, d), lambda i: (0, 0))],
        out_specs=pl.BlockSpec((tm, d), lambda i: (i, 0)),
        out_shape=jax.ShapeDtypeStruct((t, d), out_dtype),
        compiler_params=_params(("parallel",)),
        name="rmsnorm",
    )(x, g.reshape(1, d))


def _mm_kernel(a_ref, b_ref, o_ref, w_ref):
    @pl.when(pl.program_id(1) == 0)
    def _():
        w_ref[...] = b_ref[...].astype(w_ref.dtype)

    o_ref[...] = jnp.dot(a_ref[...], w_ref[...], preferred_element_type=F32).astype(o_ref.dtype)


def _matmul(a, b, col_start, n, out_dtype, name, tm=1024, tn=1024):
    m, k = a.shape
    assert col_start % tn == 0 and n % tn == 0
    col_blk = col_start // tn
    return pl.pallas_call(
        _mm_kernel,
        grid=(n // tn, m // tm),
        in_specs=[pl.BlockSpec((tm, k), lambda j, i: (i, 0)),
                  pl.BlockSpec((k, tn), lambda j, i: (0, col_blk + j))],
        out_specs=pl.BlockSpec((tm, tn), lambda j, i: (i, j)),
        out_shape=jax.ShapeDtypeStruct((m, n), out_dtype),
        scratch_shapes=[pltpu.VMEM((k, tn), BF16)],
        compiler_params=_params(("parallel", "arbitrary")),
        name=name,
    )(a, b)


def _vt_kernel(w_ref, h_ref, o_ref):
    res = lax.dot_general(w_ref[...], h_ref[...], _NT_DIMS, preferred_element_type=F32)
    for t in range(o_ref.shape[0]):
        o_ref[t] = res[:, t * MOBA_BLOCK:(t + 1) * MOBA_BLOCK].astype(o_ref.dtype)


def _v_transposed(w_t, h, tm=1024):
    c, k = w_t.shape
    t, _ = h.shape
    nb = tm // MOBA_BLOCK
    return pl.pallas_call(
        _vt_kernel,
        grid=(t // tm,),
        in_specs=[pl.BlockSpec((c, k), lambda i: (0, 0)),
                  pl.BlockSpec((tm, k), lambda i: (i, 0))],
        out_specs=pl.BlockSpec((nb, c, MOBA_BLOCK), lambda i: (i, 0, 0)),
        out_shape=jax.ShapeDtypeStruct((t // MOBA_BLOCK, c, MOBA_BLOCK), BF16),
        compiler_params=_params(("parallel",)),
        name="v_transposed",
    )(w_t, h)


def _rel_bucket(dist):
    n = jnp.maximum(dist, 0)
    max_exact = REL_BUCKETS // 2
    nf = jnp.maximum(n, 1).astype(F32)
    large = max_exact + (jnp.log(nf / max_exact) / math.log(REL_MAX_DIST / max_exact)
                         * (REL_BUCKETS - max_exact)).astype(jnp.int32)
    large = jnp.minimum(large, REL_BUCKETS - 1)
    return jnp.where(n < max_exact, n, large)


def _bias_kernel(relb_ref, bucket_ref, o_ref):
    h = pl.program_id(0)
    for t in range(3):
        bk = bucket_ref[t]
        acc = jnp.zeros(bk.shape, F32)
        for b in range(REL_BUCKETS):
            acc = jnp.where(bk == b, relb_ref[b, h], acc)
        if t == 0:
            key = lax.broadcasted_iota(jnp.int32, bk.shape, 0)
            qry = lax.broadcasted_iota(jnp.int32, bk.shape, 1)
            acc = jnp.where(key <= qry, acc, NEG)
        o_ref[0, t] = acc


def _bias_tiles(rel_bias):
    assert REL_MAX_DIST <= MOBA_BLOCK
    key = jnp.arange(MOBA_BLOCK, dtype=jnp.int32)[:, None]
    qry = jnp.arange(MOBA_BLOCK, dtype=jnp.int32)[None, :]
    buckets = jnp.stack([_rel_bucket(t * MOBA_BLOCK + qry - key) for t in range(3)])
    return pl.pallas_call(
        _bias_kernel,
        grid=(ATTN_HEADS,),
        in_specs=[pl.BlockSpec(memory_space=pltpu.SMEM),
                  pl.BlockSpec((3, MOBA_BLOCK, MOBA_BLOCK), lambda h: (0, 0, 0))],
        out_specs=pl.BlockSpec((1, 3, MOBA_BLOCK, MOBA_BLOCK), lambda h: (h, 0, 0, 0)),
        out_shape=jax.ShapeDtypeStruct((ATTN_HEADS, 3, MOBA_BLOCK, MOBA_BLOCK), F32),
        compiler_params=_params(("parallel",)),
        name="attn_bias_tiles",
    )(rel_bias.astype(F32), buckets)


ATTN_HEADS_PER_STEP = 8


def _attn_kernel(q_ref, k_ref, vt_ref, bias_ref, o_ref, kmean_ref, sel_ref, acc_ref, *, nb):
    i = pl.program_id(2)
    scale = HEAD_DIM ** -0.5
    heads = range(ATTN_HEADS_PER_STEP)
    hsl = [slice(g * HEAD_DIM, (g + 1) * HEAD_DIM) for g in heads]

    @pl.when(i == 0)
    def _():
        for g in heads:
            for n in range(nb):
                kb = k_ref[n * MOBA_BLOCK:(n + 1) * MOBA_BLOCK, hsl[g]].astype(F32)
                kmean_ref[g, n:n + 1, :] = jnp.mean(kb, axis=0, keepdims=True)

    qs = [q_ref[:, hsl[g]] for g in heads]
    for g in heads:
        gate = lax.dot_general(kmean_ref[g].astype(BF16), qs[g], _NT_DIMS,
                               preferred_element_type=F32)
        blk = lax.broadcasted_iota(jnp.int32, gate.shape, 0)
        rank = jnp.zeros(gate.shape, F32)
        for m in range(nb):
            gm = gate[m:m + 1, :]
            beats = jnp.where(gm > gate, 1.0, jnp.where(gm == gate, jnp.where(m < blk, 1.0, 0.0), 0.0))
            rank = rank + beats * jnp.where(m < i, 1.0, 0.0)
        sel_ref[g] = jnp.where(blk < i, jnp.where(rank < MOBA_TOPK, 1.0, 0.0), 0.0)

    def scores(g, j, tile):
        koff = pl.multiple_of(j * MOBA_BLOCK, MOBA_BLOCK)
        kj = k_ref[pl.ds(koff, MOBA_BLOCK), hsl[g]]
        s = lax.dot_general(kj, qs[g], _NT_DIMS, preferred_element_type=F32)
        return s * scale + bias_ref[g, tile]

    s0 = [scores(g, i, 0) for g in heads]
    m0 = [jnp.max(s0[g], axis=0, keepdims=True) for g in heads]
    p0 = [jnp.exp(s0[g] - m0[g]) for g in heads]
    l0 = [jnp.sum(p0[g], axis=0, keepdims=True) for g in heads]
    for g in heads:
        acc_ref[g] = jnp.dot(vt_ref[i, hsl[g], :], p0[g].astype(BF16), preferred_element_type=F32)

    def body(j, carry):
        ms, ls = carry
        tile = jnp.minimum(i - j, 2)
        ss = [scores(g, j, tile) for g in heads]
        ss = [jnp.where(sel_ref[g, pl.ds(j, 1), :] > 0.5, ss[g], NEG) for g in heads]
        new_m = [jnp.maximum(ms[g], jnp.max(ss[g], axis=0, keepdims=True)) for g in heads]
        alpha = [jnp.exp(ms[g] - new_m[g]) for g in heads]
        ps = [jnp.exp(ss[g] - new_m[g]) for g in heads]
        new_l = [alpha[g] * ls[g] + jnp.sum(ps[g], axis=0, keepdims=True) for g in heads]
        pv = [jnp.dot(vt_ref[j, hsl[g], :], ps[g].astype(BF16), preferred_element_type=F32) for g in heads]
        for g in heads:
            acc_ref[g] = alpha[g] * acc_ref[g] + pv[g]
        return tuple(new_m), tuple(new_l)

    _, l_f = lax.fori_loop(0, i, body, (tuple(m0), tuple(l0)))
    for g in heads:
        o_ref[:, hsl[g]] = (acc_ref[g] / l_f[g]).T.astype(o_ref.dtype)


def _attention(qk, vt3, bias, batch, seq):
    nb = seq // MOBA_BLOCK
    t = batch * seq
    gh = ATTN_HEADS_PER_STEP
    width = gh * HEAD_DIM
    groups = ATTN_HEADS // gh
    return pl.pallas_call(
        functools.partial(_attn_kernel, nb=nb),
        grid=(batch, groups, nb),
        in_specs=[
            pl.BlockSpec((MOBA_BLOCK, width), lambda b, h, i: (b * nb + i, h)),
            pl.BlockSpec((seq, width), lambda b, h, i: (b, groups + h)),
            pl.BlockSpec((nb, width, MOBA_BLOCK), lambda b, h, i: (b, h, 0)),
            pl.BlockSpec((gh, 3, MOBA_BLOCK, MOBA_BLOCK), lambda b, h, i: (h, 0, 0, 0)),
        ],
        out_specs=pl.BlockSpec((MOBA_BLOCK, width), lambda b, h, i: (b * nb + i, h)),
        out_shape=jax.ShapeDtypeStruct((t, ATTN_WIDTH), BF16),
        scratch_shapes=[pltpu.VMEM((gh, nb, HEAD_DIM), F32),
                        pltpu.VMEM((gh, nb, MOBA_BLOCK), F32),
                        pltpu.VMEM((gh, HEAD_DIM, MOBA_BLOCK), F32)],
        compiler_params=_params(("parallel", "parallel", "arbitrary")),
        name="moba_attention",
    )(qk, qk, vt3, bias)


def _rglru_kernel(xr_ref, yr_ref, cw_ref, cb_ref, wa_ref, wx_ref, ba_ref, bx_ref, lam_ref,
                  o_ref, tail_ref, h_ref):
    c = pl.program_id(1)
    tc = xr_ref.shape[0]

    @pl.when(c == 0)
    def _():
        tail_ref[...] = jnp.zeros(tail_ref.shape, F32)
        h_ref[...] = jnp.zeros(h_ref.shape, F32)

    x = xr_ref[...]
    grp_shape = (tc // 8, 8, x.shape[1])
    x3 = x.reshape(grp_shape)
    tail3 = tail_ref[...].reshape(1, 8, x.shape[1])
    sub = lax.broadcasted_iota(jnp.int32, grp_shape, 1)
    conv = cb_ref[...] + cw_ref[CONV_WIDTH - 1:CONV_WIDTH, :] * x3
    for s in range(1, CONV_WIDTH):
        k = CONV_WIDTH - 1 - s
        rolled = pltpu.roll(x3, s, axis=1)
        rolled_prev = jnp.concatenate([pltpu.roll(tail3, s, axis=1), rolled[:-1]], axis=0)
        conv = conv + cw_ref[k:k + 1, :] * jnp.where(sub >= s, rolled, rolled_prev)
    conv = conv.reshape(x.shape)
    tail_ref[...] = x[tc - 8:, :]

    xb16 = conv.astype(BF16)
    pre_a, pre_x = [], []
    for g in range(LRU_BLOCKS):
        xs = xb16[:, g * LRU_BLOCK_DIM:(g + 1) * LRU_BLOCK_DIM]
        pre_a.append(jnp.dot(xs, wa_ref[g], preferred_element_type=F32))
        pre_x.append(jnp.dot(xs, wx_ref[g], preferred_element_type=F32))
    r_gate = jax.nn.sigmoid(jnp.concatenate(pre_a, axis=1) + ba_ref[...])
    i_gate = jax.nn.sigmoid(jnp.concatenate(pre_x, axis=1) + bx_ref[...])
    z = -lam_ref[...]
    softplus = jnp.maximum(z, 0.0) + jnp.log1p(jnp.exp(-jnp.abs(z)))
    log_a = (-LRU_C) * r_gate * softplus
    a = jnp.exp(log_a)
    th = jnp.tanh(log_a)
    y2 = -2.0 * th / (1.0 - th)
    u = jnp.where(y2 > 0.0, y2 * lax.rsqrt(y2), 0.0) * (i_gate * conv)

    grp_shape = (tc // 8, 8, a.shape[1])
    a = a.reshape(grp_shape)
    u = u.reshape(grp_shape)
    sub = lax.broadcasted_iota(jnp.int32, grp_shape, 1)
    s = 1
    while s < 8:
        a_sh = pltpu.roll(a, s, axis=1)
        u_sh = pltpu.roll(u, s, axis=1)
        valid = sub >= s
        u = jnp.where(valid, a * u_sh + u, u)
        a = jnp.where(valid, a * a_sh, a)
        s *= 2
    h = h_ref[...]
    gy = _gelu_tanh(yr_ref[...])
    for grp in range(tc // 8):
        rows = slice(grp * 8, (grp + 1) * 8)
        hh = a[grp] * h + u[grp]
        h = hh[7:8, :]
        o_ref[rows, :] = (hh * gy[rows, :]).astype(o_ref.dtype)
    h_ref[...] = h


def _rglru(r, conv_w, conv_b, wa, wx, ba, bx, lam, batch, seq, tc=256):
    t = batch * seq
    nc = seq // tc
    w = LRU_WIDTH
    row = lambda v: v.reshape(1, w).astype(F32)
    full2 = lambda shape: pl.BlockSpec(shape, lambda b, c: (0,) * len(shape))
    return pl.pallas_call(
        _rglru_kernel,
        grid=(batch, nc),
        in_specs=[pl.BlockSpec((tc, w), lambda b, c: (b * nc + c, 0)),
                  pl.BlockSpec((tc, w), lambda b, c: (b * nc + c, 1)),
                  full2((CONV_WIDTH, w)), full2((1, w)),
                  full2((LRU_BLOCKS, LRU_BLOCK_DIM, LRU_BLOCK_DIM)),
                  full2((LRU_BLOCKS, LRU_BLOCK_DIM, LRU_BLOCK_DIM)),
                  full2((1, w)), full2((1, w)), full2((1, w))],
        out_specs=pl.BlockSpec((tc, w), lambda b, c: (b * nc + c, 0)),
        out_shape=jax.ShapeDtypeStruct((t, w), BF16),
        scratch_shapes=[pltpu.VMEM((8, w), F32), pltpu.VMEM((1, w), F32)],
        compiler_params=_params(("parallel", "arbitrary")),
        name="rglru",
    )(r, r, conv_w.astype(F32), row(conv_b), wa.astype(BF16), wx.astype(BF16),
      row(ba), row(bx), row(lam))


def _merge_kernel(oa_ref, or_ref, g0_ref, g1_ref, wb0_ref, wb1_ref, wo_ref, x_ref, gn_ref,
                  x1_ref, xn_ref):
    pb0 = jnp.dot(oa_ref[...], wb0_ref[...], preferred_element_type=F32)
    pb1 = jnp.dot(or_ref[...], wb1_ref[...], preferred_element_type=F32)
    merged = jax.nn.sigmoid(g0_ref[...]) * pb0 + jax.nn.sigmoid(g1_ref[...]) * pb1
    x1 = x_ref[...] + jnp.dot(merged.astype(BF16), wo_ref[...], preferred_element_type=F32)
    x1_ref[...] = x1
    xn_ref[...] = _rms(x1, gn_ref[...]).astype(xn_ref.dtype)


def _merge_out(o_att, o_rec, gl, wb0, wb1, w_out, x, g_ffn, tm=256):
    t, d = x.shape
    gate_blk = gl.shape[1] // d - 2
    cw = o_att.shape[1]
    resident = lambda shape: pl.BlockSpec(shape, lambda i: (0, 0), pipeline_mode=pl.Buffered(1))
    return pl.pallas_call(
        _merge_kernel,
        grid=(t // tm,),
        in_specs=[pl.BlockSpec((tm, cw), lambda i: (i, 0)),
                  pl.BlockSpec((tm, cw), lambda i: (i, 0)),
                  pl.BlockSpec((tm, d), lambda i: (i, gate_blk)),
                  pl.BlockSpec((tm, d), lambda i: (i, gate_blk + 1)),
                  resident((cw, d)), resident((cw, d)), resident((d, d)),
                  pl.BlockSpec((tm, d), lambda i: (i, 0)),
                  pl.BlockSpec((1, d), lambda i: (0, 0))],
        out_specs=[pl.BlockSpec((tm, d), lambda i: (i, 0)),
                   pl.BlockSpec((tm, d), lambda i: (i, 0))],
        out_shape=[jax.ShapeDtypeStruct((t, d), F32), jax.ShapeDtypeStruct((t, d), BF16)],
        compiler_params=_params(("parallel",)),
        name="merge_out",
    )(o_att, o_rec, gl, gl, wb0, wb1, w_out, x, g_ffn.reshape(1, d).astype(F32))


def _oddeven_merge(lo, hi, r):
    step = r * 2
    if step < hi - lo:
        yield from _oddeven_merge(lo, hi, step)
        yield from _oddeven_merge(lo + r, hi, step)
        yield from [(i, i + r) for i in range(lo + r, hi - r, step)]
    else:
        yield (lo, lo + r)


def _oddeven_merge_sort(lo, hi):
    if hi - lo >= 1:
        mid = lo + (hi - lo) // 2
        yield from _oddeven_merge_sort(lo, mid)
        yield from _oddeven_merge_sort(mid + 1, hi)
        yield from _oddeven_merge(lo, hi, 1)


def _peer_scores_kernel(q_ref, keys_ref, thr_ref, e1_ref, s2_ref, e2_ref, s_ref, vals_ref, cand_ref):
    nhc = keys_ref.shape[0]
    k = PEER_TOPK
    for hc in range(nhc):
        qs = q_ref[:, hc * PEER_NKEYS:(hc + 1) * PEER_NKEYS]
        s_ref[hc] = lax.dot_general(keys_ref[hc], qs, _NT_DIMS, preferred_element_type=F32)

    n_lvl = PEER_NKEYS // 8
    network = list(_oddeven_merge_sort(0, n_lvl - 1))

    def top_values(hc, carry):
        x = s_ref[hc]
        lvl = [x[8 * v:8 * v + 8, :] for v in range(n_lvl)]
        for a, b in network:
            lvl[a], lvl[b] = jnp.maximum(lvl[a], lvl[b]), jnp.minimum(lvl[a], lvl[b])
        for r in range(k + 1):
            m = jnp.max(lvl[0], axis=0, keepdims=True)
            vals_ref[hc, r:r + 1, :] = m
            hit = lvl[0] == m
            for v in range(min(k - r, n_lvl)):
                below = lvl[v + 1] if v + 1 < n_lvl else LOWEST
                lvl[v] = jnp.where(hit, below, lvl[v])
        return carry

    lax.fori_loop(0, nhc, top_values, 0)

    def head_stats(h, carry):
        v1 = vals_ref[2 * h, 0:k + 1, :]
        v2 = vals_ref[2 * h + 1, 0:k + 1, :]
        cand_ref[...] = jnp.full(cand_ref.shape, LOWEST, F32)
        off = 0
        for a in range(k + 1):
            nb = (k + 1) // (a + 1)
            cand_ref[off:off + nb, :] = v1[a:a + 1, :] + v2[0:nb, :]
            off += nb
        cand = cand_ref[...]
        x = cand
        kth = None
        for r in range(k):
            kth = jnp.max(x, axis=0, keepdims=True)
            x = jnp.where(x == kth, LOWEST, x)
        nxt = jnp.max(x, axis=0, keepdims=True)
        m1 = v1[0:1, :]
        m2 = v2[0:1, :]
        tau = 0.5 * (kth + nxt)
        z = jnp.sum(jnp.where(cand >= tau, jnp.exp(cand - (m1 + m2)), 0.0), axis=0, keepdims=True)
        s1 = s_ref[2 * h]
        s2 = s_ref[2 * h + 1]
        thr_ref[h] = tau - s1
        e1_ref[h] = jnp.exp(s1 - m1)
        e2 = jnp.exp(s2 - m2) * (0.5 / z)
        for c in range(s2_ref.shape[1]):
            s2_ref[h, c] = s2[:, c * 128:(c + 1) * 128]
            e2_ref[h, c] = e2[:, c * 128:(c + 1) * 128]
        return carry

    lax.fori_loop(0, nhc // 2, head_stats, 0)


def _peer_scores(q, keys, tm=256):
    t, _ = q.shape
    nhc = keys.shape[0]
    out_spec = pl.BlockSpec((PEER_HEADS, PEER_NKEYS, tm), lambda i: (0, 0, i))
    out_shape = jax.ShapeDtypeStruct((PEER_HEADS, PEER_NKEYS, t), F32)
    chunk_spec = pl.BlockSpec((PEER_HEADS, tm // 128, PEER_NKEYS, 128), lambda i: (0, i, 0, 0))
    chunk_shape = jax.ShapeDtypeStruct((PEER_HEADS, t // 128, PEER_NKEYS, 128), F32)
    n_cand = sum((PEER_TOPK + 1) // (a + 1) for a in range(PEER_TOPK + 1))
    n_cand = -(-n_cand // 8) * 8
    return pl.pallas_call(
        _peer_scores_kernel,
        grid=(t // tm,),
        in_specs=[pl.BlockSpec((tm, nhc * PEER_NKEYS), lambda i: (i, 0)),
                  pl.BlockSpec(keys.shape, lambda i: (0, 0, 0))],
        out_specs=[out_spec, out_spec, chunk_spec, chunk_spec],
        out_shape=[out_shape, out_shape, chunk_shape, chunk_shape],
        scratch_shapes=[pltpu.VMEM((nhc, PEER_NKEYS, tm), F32),
                        pltpu.VMEM((nhc, PEER_TOPK + 8, tm), F32),
                        pltpu.VMEM((n_cand, tm), F32)],
        compiler_params=_params(("parallel",)),
        name="peer_scores",
    )(q, keys)


def _transpose_kernel(x_ref, o_ref):
    o_ref[...] = x_ref[...].T.astype(o_ref.dtype)


def _transposed_tiles(table, tn):
    e, d = table.shape
    return pl.pallas_call(
        _transpose_kernel,
        grid=(e // tn,),
        in_specs=[pl.BlockSpec((tn, d), lambda i: (i, 0))],
        out_specs=pl.BlockSpec((None, d, tn), lambda i: (i, 0, 0)),
        out_shape=jax.ShapeDtypeStruct((e // tn, d, tn), BF16),
        compiler_params=_params(("parallel",)),
        name="expert_value_tiles",
    )(table)


def _peer_dense_kernel(xn_ref, thr_ref, e1_ref, s2_ref, e2_ref, u_ref, vt_ref, acc_ref,
                       xnt_ref, act_ref, p_ref):
    j = pl.program_id(1)
    lane_chunks, tn, _ = act_ref.shape
    rows = tn // PEER_NKEYS

    @pl.when(j == 0)
    def _():
        acc_ref[...] = jnp.zeros(acc_ref.shape, F32)
        xnt_ref[...] = xn_ref[...].astype(F32).T.astype(xnt_ref.dtype)

    act = jnp.dot(u_ref[...], xnt_ref[...], preferred_element_type=F32)
    for lc in range(lane_chunks):
        act_ref[lc] = act[:, lc * 128:(lc + 1) * 128]

    def weigh(r, carry):
        i1 = j * rows + r
        roff = pl.multiple_of(r * PEER_NKEYS, PEER_NKEYS)
        thr_rows = [thr_ref[h, pl.ds(i1, 1), :] for h in range(PEER_HEADS)]
        e1_rows = [e1_ref[h, pl.ds(i1, 1), :] for h in range(PEER_HEADS)]
        for lc in range(lane_chunks):
            lanes = slice(lc * 128, (lc + 1) * 128)
            w = None
            for h in range(PEER_HEADS):
                w_h = jnp.where(s2_ref[h, lc] >= thr_rows[h][:, lanes],
                                e2_ref[h, lc] * e1_rows[h][:, lanes], 0.0)
                w = w_h if w is None else w + w_h
            a = act_ref[lc, pl.ds(roff, PEER_NKEYS), :]
            p_ref[pl.ds(roff, PEER_NKEYS), lanes] = (_gelu_tanh_x2(a) * w).astype(p_ref.dtype)
        return carry

    lax.fori_loop(0, rows, weigh, 0)
    res = jnp.dot(vt_ref[...], p_ref[...], preferred_element_type=F32)
    for lc in range(lane_chunks):
        acc_ref[lc] += res[:, lc * 128:(lc + 1) * 128]


def _peer_dense(xn, thr, e1, s2, e2, u_bf, vt_tiles, tm=512):
    t, d = xn.shape
    n_tiles, _, tn = vt_tiles.shape
    route_spec = pl.BlockSpec((PEER_HEADS, PEER_NKEYS, tm), lambda i, j: (0, 0, i))
    chunk_spec = pl.BlockSpec((PEER_HEADS, tm // 128, PEER_NKEYS, 128), lambda i, j: (0, i, 0, 0))
    return pl.pallas_call(
        _peer_dense_kernel,
        grid=(t // tm, n_tiles),
        in_specs=[pl.BlockSpec((tm, d), lambda i, j: (i, 0)),
                  route_spec, route_spec, chunk_spec, chunk_spec,
                  pl.BlockSpec((tn, d), lambda i, j: (j, 0)),
                  pl.BlockSpec((None, d, tn), lambda i, j: (j, 0, 0))],
        out_specs=pl.BlockSpec((tm // 128, d, 128), lambda i, j: (i, 0, 0)),
        out_shape=jax.ShapeDtypeStruct((t // 128, d, 128), F32),
        scratch_shapes=[pltpu.VMEM((d, tm), BF16), pltpu.VMEM((tm // 128, tn, 128), F32),
                        pltpu.VMEM((tn, tm), BF16)],
        compiler_params=_params(("parallel", "arbitrary")),
        name="peer_dense",
    )(xn, thr, e1, s2, e2, u_bf, vt_tiles)


def _final_kernel(x1_ref, pt_ref, g_ref, y_ref):
    for c in range(pt_ref.shape[0]):
        rows = slice(c * 128, (c + 1) * 128)
        y_ref[rows, :] = _rms(x1_ref[rows, :] + pt_ref[c].T, g_ref[...])


def _residual_norm(x1, peer_chunks, g, tm=512):
    t, d = x1.shape
    return pl.pallas_call(
        _final_kernel,
        grid=(t // tm,),
        in_specs=[pl.BlockSpec((tm, d), lambda i: (i, 0)),
                  pl.BlockSpec((tm // 128, d, 128), lambda i: (i, 0, 0)),
                  pl.BlockSpec((1, d), lambda i: (0, 0))],
        out_specs=pl.BlockSpec((tm, d), lambda i: (i, 0)),
        out_shape=jax.ShapeDtypeStruct((t, d), F32),
        compiler_params=_params(("parallel",)),
        name="residual_norm",
    )(x1, peer_chunks, g.reshape(1, d).astype(F32))


def kernel(x, norm_mix_g, w_in, conv_w, conv_b, lru_wa, lru_ba, lru_wx, lru_bx, lru_lambda,
           w_branch, w_out, rel_bias, norm_ffn_g, peer_wq, peer_keys, peer_u, peer_v, norm_final_g):
    batch, seq, d = x.shape
    t = batch * seq
    assert w_in.shape[0] == 1, "single-layer trunk"
    xt = x.reshape(t, d)
    bias = _bias_tiles(rel_bias)
    a_w = ATTN_WIDTH
    w_l = w_in[0]
    h = _rmsnorm(xt, norm_mix_g[0], BF16)
    qk = _matmul(h, w_l, 0, 2 * a_w, BF16, "proj_qk")
    vt3 = _v_transposed(w_l[:, 2 * a_w:3 * a_w].T.astype(BF16), h)
    assert 2 * LRU_WIDTH == d
    r = gl = _matmul(h, w_l, 3 * a_w, 2 * LRU_WIDTH + 2 * d, F32, "proj_rec_gate")
    o_att = _attention(qk, vt3, bias, batch, seq)
    o_rec = _rglru(r, conv_w[0], conv_b[0], lru_wa[0], lru_wx[0], lru_ba[0], lru_bx[0],
                   lru_lambda[0], batch, seq)
    x1, xn = _merge_out(o_att, o_rec, gl, w_branch[0, 0].astype(BF16),
                        w_branch[0, 1].astype(BF16), w_out[0].astype(BF16), xt, norm_ffn_g[0])
    q = _matmul(xn, peer_wq[0], 0, peer_wq.shape[2], BF16, "peer_query")
    keys = peer_keys[0].reshape(PEER_HEADS * 2, PEER_NKEYS, PEER_DKEY // 2).astype(BF16)
    thr, e1, s2, e2 = _peer_scores(q, keys)
    vt_tiles = _transposed_tiles(peer_v[0], PEER_EXPERT_TILE)
    peer_out = _peer_dense(xn, thr, e1, s2, e2, peer_u[0].astype(BF16), vt_tiles)
    y = _residual_norm(x1, peer_out, norm_final_g)
    return y.reshape(batch, seq, d)
```

```python
import functools
import math

import jax
import jax.numpy as jnp
from jax import lax
from jax.experimental import pallas as pl
from jax.experimental.pallas import tpu as pltpu

ATTN_HEADS = 8
HEAD_DIM = 128
ATTN_WIDTH = ATTN_HEADS * HEAD_DIM
MOBA_BLOCK = 256
MOBA_TOPK = 3
LRU_WIDTH = 1024
LRU_BLOCKS = 8
LRU_BLOCK_DIM = LRU_WIDTH // LRU_BLOCKS
CONV_WIDTH = 4
LRU_C = 8.0
REL_BUCKETS = 32
REL_MAX_DIST = 128
PEER_HEADS = 8
PEER_NKEYS = 128
PEER_DKEY = 256
PEER_TOPK = 16
EPS = 1e-6
NEG = -1e30
LOWEST = -3.0e38
PEER_EXPERT_TILE = 1024

V7X_VMEM_LIMIT_BYTES = 56 * 1024 * 1024

F32 = jnp.float32
BF16 = jnp.bfloat16

_NT_DIMS = (((1,), (1,)), ((), ()))


def _params(semantics):
    return pltpu.CompilerParams(dimension_semantics=semantics,
                                vmem_limit_bytes=V7X_VMEM_LIMIT_BYTES)


def _gelu_tanh_x2(x):
    c = math.sqrt(2.0 / math.pi)
    return x * (1.0 + jnp.tanh(x * (c + (0.044715 * c) * (x * x))))


def _gelu_tanh(x):
    return 0.5 * _gelu_tanh_x2(x)


def _rms(x, g):
    ms = jnp.mean(x * x, axis=-1, keepdims=True)
    return x * lax.rsqrt(ms + EPS) * g


def _rmsnorm_kernel(x_ref, g_ref, o_ref):
    o_ref[...] = _rms(x_ref[...], g_ref[...]).astype(o_ref.dtype)


def _rmsnorm(x, g, out_dtype, tm=1024):
    t, d = x.shape
    return pl.pallas_call(
        _rmsnorm_kernel,
        grid=(t // tm,),
        in_specs=[pl.BlockSpec((tm, d), lambda i: (i, 0)),
                  pl.BlockSpec((1, d), lambda i: (0, 0))],
        out_specs=pl.BlockSpec((tm, d), lambda i: (i, 0)),
        out_shape=jax.ShapeDtypeStruct((t, d), out_dtype),
        compiler_params=_params(("parallel",)),
        name="rmsnorm",
    )(x, g.reshape(1, d))


def _mm_kernel(a_ref, b_ref, o_ref, w_ref):
    @pl.when(pl.program_id(1) == 0)
    def _():
        w_ref[...] = b_ref[...].astype(w_ref.dtype)

    o_ref[...] = jnp.dot(a_ref[...], w_ref[...], preferred_element_type=F32).astype(o_ref.dtype)


def _matmul(a, b, col_start, n, out_dtype, name, tm=1024, tn=1024):
    m, k = a.shape
    assert col_start % tn == 0 and n % tn == 0
    col_blk = col_start // tn
    return pl.pallas_call(
        _mm_kernel,
        grid=(n // tn, m // tm),
        in_specs=[pl.BlockSpec((tm, k), lambda j, i: (i, 0)),
                  pl.BlockSpec((k, tn), lambda j, i: (0, col_blk + j))],
        out_specs=pl.BlockSpec((tm, tn), lambda j, i: (i, j)),
        out_shape=jax.ShapeDtypeStruct((m, n), out_dtype),
        scratch_shapes=[pltpu.VMEM((k, tn), BF16)],
        compiler_params=_params(("parallel", "arbitrary")),
        name=name,
    )(a, b)


def _vt_kernel(w_ref, h_ref, o_ref):
    res = lax.dot_general(w_ref[...], h_ref[...], _NT_DIMS, preferred_element_type=F32)
    for t in range(o_ref.shape[0]):
        o_ref[t] = res[:, t * MOBA_BLOCK:(t + 1) * MOBA_BLOCK].astype(o_ref.dtype)


def _v_transposed(w_t, h, tm=1024):
    c, k = w_t.shape
    t, _ = h.shape
    nb = tm // MOBA_BLOCK
    return pl.pallas_call(
        _vt_kernel,
        grid=(t // tm,),
        in_specs=[pl.BlockSpec((c, k), lambda i: (0, 0)),
                  pl.BlockSpec((tm, k), lambda i: (i, 0))],
        out_specs=pl.BlockSpec((nb, c, MOBA_BLOCK), lambda i: (i, 0, 0)),
        out_shape=jax.ShapeDtypeStruct((t // MOBA_BLOCK, c, MOBA_BLOCK), BF16),
        compiler_params=_params(("parallel",)),
        name="v_transposed",
    )(w_t, h)


def _rel_bucket(dist):
    n = jnp.maximum(dist, 0)
    max_exact = REL_BUCKETS // 2
    nf = jnp.maximum(n, 1).astype(F32)
    large = max_exact + (jnp.log(nf / max_exact) / math.log(REL_MAX_DIST / max_exact)
                         * (REL_BUCKETS - max_exact)).astype(jnp.int32)
    large = jnp.minimum(large, REL_BUCKETS - 1)
    return jnp.where(n < max_exact, n, large)


def _bias_kernel(relb_ref, bucket_ref, o_ref):
    h = pl.program_id(0)
    for t in range(3):
        bk = bucket_ref[t]
        acc = jnp.zeros(bk.shape, F32)
        for b in range(REL_BUCKETS):
            acc = jnp.where(bk == b, relb_ref[b, h], acc)
        if t == 0:
            key = lax.broadcasted_iota(jnp.int32, bk.shape, 0)
            qry = lax.broadcasted_iota(jnp.int32, bk.shape, 1)
            acc = jnp.where(key <= qry, acc, NEG)
        o_ref[0, t] = acc


def _bias_tiles(rel_bias):
    assert REL_MAX_DIST <= MOBA_BLOCK
    key = jnp.arange(MOBA_BLOCK, dtype=jnp.int32)[:, None]
    qry = jnp.arange(MOBA_BLOCK, dtype=jnp.int32)[None, :]
    buckets = jnp.stack([_rel_bucket(t * MOBA_BLOCK + qry - key) for t in range(3)])
    return pl.pallas_call(
        _bias_kernel,
        grid=(ATTN_HEADS,),
        in_specs=[pl.BlockSpec(memory_space=pltpu.SMEM),
                  pl.BlockSpec((3, MOBA_BLOCK, MOBA_BLOCK), lambda h: (0, 0, 0))],
        out_specs=pl.BlockSpec((1, 3, MOBA_BLOCK, MOBA_BLOCK), lambda h: (h, 0, 0, 0)),
        out_shape=jax.ShapeDtypeStruct((ATTN_HEADS, 3, MOBA_BLOCK, MOBA_BLOCK), F32),
        compiler_params=_params(("parallel",)),
        name="attn_bias_tiles",
    )(rel_bias.astype(F32), buckets)


ATTN_HEADS_PER_STEP = 8


def _attn_kernel(q_ref, k_ref, vt_ref, bias_ref, o_ref, kmean_ref, sel_ref, acc_ref, *, nb):
    i = pl.program_id(2)
    scale = HEAD_DIM ** -0.5
    heads = range(ATTN_HEADS_PER_STEP)
    hsl = [slice(g * HEAD_DIM, (g + 1) * HEAD_DIM) for g in heads]

    @pl.when(i == 0)
    def _():
        for g in heads:
            for n in range(nb):
                kb = k_ref[n * MOBA_BLOCK:(n + 1) * MOBA_BLOCK, hsl[g]].astype(F32)
                kmean_ref[g, n:n + 1, :] = jnp.mean(kb, axis=0, keepdims=True)

    qs = [q_ref[:, hsl[g]] for g in heads]
    for g in heads:
        gate = lax.dot_general(kmean_ref[g].astype(BF16), qs[g], _NT_DIMS,
                               preferred_element_type=F32)
        blk = lax.broadcasted_iota(jnp.int32, gate.shape, 0)
        rank = jnp.zeros(gate.shape, F32)
        for m in range(nb):
            gm = gate[m:m + 1, :]
            beats = jnp.where(gm > gate, 1.0, jnp.where(gm == gate, jnp.where(m < blk, 1.0, 0.0), 0.0))
            rank = rank + beats * jnp.where(m < i, 1.0, 0.0)
        sel_ref[g] = jnp.where(blk < i, jnp.where(rank < MOBA_TOPK, 1.0, 0.0), 0.0)

    def scores(g, j, tile):
        koff = pl.multiple_of(j * MOBA_BLOCK, MOBA_BLOCK)
        kj = k_ref[pl.ds(koff, MOBA_BLOCK), hsl[g]]
        s = lax.dot_general(kj, qs[g], _NT_DIMS, preferred_element_type=F32)
        return s * scale + bias_ref[g, tile]

    s0 = [scores(g, i, 0) for g in heads]
    m0 = [jnp.max(s0[g], axis=0, keepdims=True) for g in heads]
    p0 = [jnp.exp(s0[g] - m0[g]) for g in heads]
    l0 = [jnp.sum(p0[g], axis=0, keepdims=True) for g in heads]
    for g in heads:
        acc_ref[g] = jnp.dot(vt_ref[i, hsl[g], :], p0[g].astype(BF16), preferred_element_type=F32)

    def body(j, carry):
        ms, ls = carry
        tile = jnp.minimum(i - j, 2)
        ss = [scores(g, j, tile) for g in heads]
        ss = [jnp.where(sel_ref[g, pl.ds(j, 1), :] > 0.5, ss[g], NEG) for g in heads]
        new_m = [jnp.maximum(ms[g], jnp.max(ss[g], axis=0, keepdims=True)) for g in heads]
        alpha = [jnp.exp(ms[g] - new_m[g]) for g in heads]
        ps = [jnp.exp(ss[g] - new_m[g]) for g in heads]
        new_l = [alpha[g] * ls[g] + jnp.sum(ps[g], axis=0, keepdims=True) for g in heads]
        pv = [jnp.dot(vt_ref[j, hsl[g], :], ps[g].astype(BF16), preferred_element_type=F32) for g in heads]
        for g in heads:
            acc_ref[g] = alpha[g] * acc_ref[g] + pv[g]
        return tuple(new_m), tuple(new_l)

    _, l_f = lax.fori_loop(0, i, body, (tuple(m0), tuple(l0)))
    for g in heads:
        o_ref[:, hsl[g]] = (acc_ref[g] / l_f[g]).T.astype(o_ref.dtype)


def _attention(qk, vt3, bias, batch, seq):
    nb = seq // MOBA_BLOCK
    t = batch * seq
    gh = ATTN_HEADS_PER_STEP
    width = gh * HEAD_DIM
    groups = ATTN_HEADS // gh
    return pl.pallas_call(
        functools.partial(_attn_kernel, nb=nb),
        grid=(batch, groups, nb),
        in_specs=[
            pl.BlockSpec((MOBA_BLOCK, width), lambda b, h, i: (b * nb + i, h)),
            pl.BlockSpec((seq, width), lambda b, h, i: (b, groups + h)),
            pl.BlockSpec((nb, width, MOBA_BLOCK), lambda b, h, i: (b, h, 0)),
            pl.BlockSpec((gh, 3, MOBA_BLOCK, MOBA_BLOCK), lambda b, h, i: (h, 0, 0, 0)),
        ],
        out_specs=pl.BlockSpec((MOBA_BLOCK, width), lambda b, h, i: (b * nb + i, h)),
        out_shape=jax.ShapeDtypeStruct((t, ATTN_WIDTH), BF16),
        scratch_shapes=[pltpu.VMEM((gh, nb, HEAD_DIM), F32),
                        pltpu.VMEM((gh, nb, MOBA_BLOCK), F32),
                        pltpu.VMEM((gh, HEAD_DIM, MOBA_BLOCK), F32)],
        compiler_params=_params(("parallel", "parallel", "arbitrary")),
        name="moba_attention",
    )(qk, qk, vt3, bias)


def _rglru_kernel(xr_ref, yr_ref, cw_ref, cb_ref, wa_ref, wx_ref, ba_ref, bx_ref, lam_ref,
                  o_ref, tail_ref, h_ref):
    c = pl.program_id(1)
    tc = xr_ref.shape[0]

    @pl.when(c == 0)
    def _():
        tail_ref[...] = jnp.zeros(tail_ref.shape, F32)
        h_ref[...] = jnp.zeros(h_ref.shape, F32)

    x = xr_ref[...]
    grp_shape = (tc // 8, 8, x.shape[1])
    x3 = x.reshape(grp_shape)
    tail3 = tail_ref[...].reshape(1, 8, x.shape[1])
    sub = lax.broadcasted_iota(jnp.int32, grp_shape, 1)
    conv = cb_ref[...] + cw_ref[CONV_WIDTH - 1:CONV_WIDTH, :] * x3
    for s in range(1, CONV_WIDTH):
        k = CONV_WIDTH - 1 - s
        rolled = pltpu.roll(x3, s, axis=1)
        rolled_prev = jnp.concatenate([pltpu.roll(tail3, s, axis=1), rolled[:-1]], axis=0)
        conv = conv + cw_ref[k:k + 1, :] * jnp.where(sub >= s, rolled, rolled_prev)
    conv = conv.reshape(x.shape)
    tail_ref[...] = x[tc - 8:, :]

    xb16 = conv.astype(BF16)
    pre_a, pre_x = [], []
    for g in range(LRU_BLOCKS):
        xs = xb16[:, g * LRU_BLOCK_DIM:(g + 1) * LRU_BLOCK_DIM]
        pre_a.append(jnp.dot(xs, wa_ref[g], preferred_element_type=F32))
        pre_x.append(jnp.dot(xs, wx_ref[g], preferred_element_type=F32))
    r_gate = jax.nn.sigmoid(jnp.concatenate(pre_a, axis=1) + ba_ref[...])
    i_gate = jax.nn.sigmoid(jnp.concatenate(pre_x, axis=1) + bx_ref[...])
    z = -lam_ref[...]
    softplus = jnp.maximum(z, 0.0) + jnp.log1p(jnp.exp(-jnp.abs(z)))
    log_a = (-LRU_C) * r_gate * softplus
    a = jnp.exp(log_a)
    th = jnp.tanh(log_a)
    y2 = -2.0 * th / (1.0 - th)
    u = jnp.where(y2 > 0.0, y2 * lax.rsqrt(y2), 0.0) * (i_gate * conv)

    grp_shape = (tc // 8, 8, a.shape[1])
    a = a.reshape(grp_shape)
    u = u.reshape(grp_shape)
    sub = lax.broadcasted_iota(jnp.int32, grp_shape, 1)
    s = 1
    while s < 8:
        a_sh = pltpu.roll(a, s, axis=1)
        u_sh = pltpu.roll(u, s, axis=1)
        valid = sub >= s
        u = jnp.where(valid, a * u_sh + u, u)
        a = jnp.where(valid, a * a_sh, a)
        s *= 2
    h = h_ref[...]
    gy = _gelu_tanh(yr_ref[...])
    for grp in range(tc // 8):
        rows = slice(grp * 8, (grp + 1) * 8)
        hh = a[grp] * h + u[grp]
        h = hh[7:8, :]
        o_ref[rows, :] = (hh * gy[rows, :]).astype(o_ref.dtype)
    h_ref[...] = h


def _rglru(r, conv_w, conv_b, wa, wx, ba, bx, lam, batch, seq, tc=256):
    t = batch * seq
    nc = seq // tc
    w = LRU_WIDTH
    row = lambda v: v.reshape(1, w).astype(F32)
    full2 = lambda shape: pl.BlockSpec(shape, lambda b, c: (0,) * len(shape))
    return pl.pallas_call(
        _rglru_kernel,
        grid=(batch, nc),
        in_specs=[pl.BlockSpec((tc, w), lambda b, c: (b * nc + c, 0)),
                  pl.BlockSpec((tc, w), lambda b, c: (b * nc + c, 1)),
                  full2((CONV_WIDTH, w)), full2((1, w)),
                  full2((LRU_BLOCKS, LRU_BLOCK_DIM, LRU_BLOCK_DIM)),
                  full2((LRU_BLOCKS, LRU_BLOCK_DIM, LRU_BLOCK_DIM)),
                  full2((1, w)), full2((1, w)), full2((1, w))],
        out_specs=pl.BlockSpec((tc, w), lambda b, c: (b * nc + c, 0)),
        out_shape=jax.ShapeDtypeStruct((t, w), BF16),
        scratch_shapes=[pltpu.VMEM((8, w), F32), pltpu.VMEM((1, w), F32)],
        compiler_params=_params(("parallel", "arbitrary")),
        name="rglru",
    )(r, r, conv_w.astype(F32), row(conv_b), wa.astype(BF16), wx.astype(BF16),
      row(ba), row(bx), row(lam))


def _merge_kernel(oa_ref, or_ref, g0_ref, g1_ref, wb0_ref, wb1_ref, wo_ref, x_ref, gn_ref,
                  x1_ref, xn_ref):
    pb0 = jnp.dot(oa_ref[...], wb0_ref[...], preferred_element_type=F32)
    pb1 = jnp.dot(or_ref[...], wb1_ref[...], preferred_element_type=F32)
    merged = jax.nn.sigmoid(g0_ref[...]) * pb0 + jax.nn.sigmoid(g1_ref[...]) * pb1
    x1 = x_ref[...] + jnp.dot(merged.astype(BF16), wo_ref[...], preferred_element_type=F32)
    x1_ref[...] = x1
    xn_ref[...] = _rms(x1, gn_ref[...]).astype(xn_ref.dtype)


def _merge_out(o_att, o_rec, gl, wb0, wb1, w_out, x, g_ffn, tm=256):
    t, d = x.shape
    gate_blk = gl.shape[1] // d - 2
    cw = o_att.shape[1]
    resident = lambda shape: pl.BlockSpec(shape, lambda i: (0, 0), pipeline_mode=pl.Buffered(1))
    return pl.pallas_call(
        _merge_kernel,
        grid=(t // tm,),
        in_specs=[pl.BlockSpec((tm, cw), lambda i: (i, 0)),
                  pl.BlockSpec((tm, cw), lambda i: (i, 0)),
                  pl.BlockSpec((tm, d), lambda i: (i, gate_blk)),
                  pl.BlockSpec((tm, d), lambda i: (i, gate_blk + 1)),
                  resident((cw, d)), resident((cw, d)), resident((d, d)),
                  pl.BlockSpec((tm, d), lambda i: (i, 0)),
                  pl.BlockSpec((1, d), lambda i: (0, 0))],
        out_specs=[pl.BlockSpec((tm, d), lambda i: (i, 0)),
                   pl.BlockSpec((tm, d), lambda i: (i, 0))],
        out_shape=[jax.ShapeDtypeStruct((t, d), F32), jax.ShapeDtypeStruct((t, d), BF16)],
        compiler_params=_params(("parallel",)),
        name="merge_out",
    )(o_att, o_rec, gl, gl, wb0, wb1, w_out, x, g_ffn.reshape(1, d).astype(F32))


def _oddeven_merge(lo, hi, r):
    step = r * 2
    if step < hi - lo:
        yield from _oddeven_merge(lo, hi, step)
        yield from _oddeven_merge(lo + r, hi, step)
        yield from [(i, i + r) for i in range(lo + r, hi - r, step)]
    else:
        yield (lo, lo + r)


def _oddeven_merge_sort(lo, hi):
    if hi - lo >= 1:
        mid = lo + (hi - lo) // 2
        yield from _oddeven_merge_sort(lo, mid)
        yield from _oddeven_merge_sort(mid + 1, hi)
        yield from _oddeven_merge(lo, hi, 1)


def _peer_scores_kernel(q_ref, keys_ref, thr_ref, e1_ref, s2_ref, e2_ref, s_ref, vals_ref, cand_ref):
    nhc = keys_ref.shape[0]
    k = PEER_TOPK
    for hc in range(nhc):
        qs = q_ref[:, hc * PEER_NKEYS:(hc + 1) * PEER_NKEYS]
        s_ref[hc] = lax.dot_general(keys_ref[hc], qs, _NT_DIMS, preferred_element_type=F32)

    n_lvl = PEER_NKEYS // 8
    network = list(_oddeven_merge_sort(0, n_lvl - 1))

    def top_values(hc, carry):
        x = s_ref[hc]
        lvl = [x[8 * v:8 * v + 8, :] for v in range(n_lvl)]
        for a, b in network:
            lvl[a], lvl[b] = jnp.maximum(lvl[a], lvl[b]), jnp.minimum(lvl[a], lvl[b])
        for r in range(k + 1):
            m = jnp.max(lvl[0], axis=0, keepdims=True)
            vals_ref[hc, r:r + 1, :] = m
            hit = lvl[0] == m
            for v in range(min(k - r, n_lvl)):
                below = lvl[v + 1] if v + 1 < n_lvl else LOWEST
                lvl[v] = jnp.where(hit, below, lvl[v])
        return carry

    lax.fori_loop(0, nhc, top_values, 0)

    def head_stats(h, carry):
        v1 = vals_ref[2 * h, 0:k + 1, :]
        v2 = vals_ref[2 * h + 1, 0:k + 1, :]
        cand_ref[...] = jnp.full(cand_ref.shape, LOWEST, F32)
        off = 0
        for a in range(k + 1):
            nb = (k + 1) // (a + 1)
            cand_ref[off:off + nb, :] = v1[a:a + 1, :] + v2[0:nb, :]
            off += nb
        cand = cand_ref[...]
        x = cand
        kth = None
        for r in range(k):
            kth = jnp.max(x, axis=0, keepdims=True)
            x = jnp.where(x == kth, LOWEST, x)
        nxt = jnp.max(x, axis=0, keepdims=True)
        m1 = v1[0:1, :]
        m2 = v2[0:1, :]
        tau = 0.5 * (kth + nxt)
        z = jnp.sum(jnp.where(cand >= tau, jnp.exp(cand - (m1 + m2)), 0.0), axis=0, keepdims=True)
        s1 = s_ref[2 * h]
        s2 = s_ref[2 * h + 1]
        thr_ref[h] = tau - s1
        e1_ref[h] = jnp.exp(s1 - m1)
        e2 = jnp.exp(s2 - m2) * (0.5 / z)
        for c in range(s2_ref.shape[1]):
            s2_ref[h, c] = s2[:, c * 128:(c + 1) * 128]
            e2_ref[h, c] = e2[:, c * 128:(c + 1) * 128]
        return carry

    lax.fori_loop(0, nhc // 2, head_stats, 0)


def _peer_scores(q, keys, tm=256):
    t, _ = q.shape
    nhc = keys.shape[0]
    out_spec = pl.BlockSpec((PEER_HEADS, PEER_NKEYS, tm), lambda i: (0, 0, i))
    out_shape = jax.ShapeDtypeStruct((PEER_HEADS, PEER_NKEYS, t), F32)
    chunk_spec = pl.BlockSpec((PEER_HEADS, tm // 128, PEER_NKEYS, 128), lambda i: (0, i, 0, 0))
    chunk_shape = jax.ShapeDtypeStruct((PEER_HEADS, t // 128, PEER_NKEYS, 128), F32)
    n_cand = sum((PEER_TOPK + 1) // (a + 1) for a in range(PEER_TOPK + 1))
    n_cand = -(-n_cand // 8) * 8
    return pl.pallas_call(
        _peer_scores_kernel,
        grid=(t // tm,),
        in_specs=[pl.BlockSpec((tm, nhc * PEER_NKEYS), lambda i: (i, 0)),
                  pl.BlockSpec(keys.shape, lambda i: (0, 0, 0))],
        out_specs=[out_spec, out_spec, chunk_spec, chunk_spec],
        out_shape=[out_shape, out_shape, chunk_shape, chunk_shape],
        scratch_shapes=[pltpu.VMEM((nhc, PEER_NKEYS, tm), F32),
                        pltpu.VMEM((nhc, PEER_TOPK + 8, tm), F32),
                        pltpu.VMEM((n_cand, tm), F32)],
        compiler_params=_params(("parallel",)),
        name="peer_scores",
    )(q, keys)


def _expert_tiles_kernel(u_ref, v_ref, ub_ref, vt_ref):
    ub_ref[...] = u_ref[...].astype(ub_ref.dtype)
    vt_ref[...] = v_ref[...].T.astype(vt_ref.dtype)


def _expert_tiles(u_table, v_table, tn):
    e, d = v_table.shape
    return pl.pallas_call(
        _expert_tiles_kernel,
        grid=(e // tn,),
        in_specs=[pl.BlockSpec((tn, d), lambda i: (i, 0)),
                  pl.BlockSpec((tn, d), lambda i: (i, 0))],
        out_specs=[pl.BlockSpec((tn, d), lambda i: (i, 0)),
                   pl.BlockSpec((None, d, tn), lambda i: (i, 0, 0))],
        out_shape=[jax.ShapeDtypeStruct((e, d), BF16),
                   jax.ShapeDtypeStruct((e // tn, d, tn), BF16)],
        compiler_params=_params(("parallel",)),
        name="expert_tiles",
    )(u_table, v_table)


def _peer_dense_kernel(xn_ref, thr_ref, e1_ref, s2_ref, e2_ref, u_ref, vt_ref, acc_ref,
                       xnt_ref, act_ref, p_ref):
    j = pl.program_id(1)
    lane_chunks, tn, _ = act_ref.shape
    rows = tn // PEER_NKEYS

    @pl.when(j == 0)
    def _():
        acc_ref[...] = jnp.zeros(acc_ref.shape, F32)
        xnt_ref[...] = xn_ref[...].astype(F32).T.astype(xnt_ref.dtype)

    act = jnp.dot(u_ref[...], xnt_ref[...], preferred_element_type=F32)
    for lc in range(lane_chunks):
        act_ref[lc] = act[:, lc * 128:(lc + 1) * 128]

    def weigh(r, carry):
        i1 = j * rows + r
        roff = pl.multiple_of(r * PEER_NKEYS, PEER_NKEYS)
        thr_rows = [thr_ref[h, pl.ds(i1, 1), :] for h in range(PEER_HEADS)]
        e1_rows = [e1_ref[h, pl.ds(i1, 1), :] for h in range(PEER_HEADS)]
        for lc in range(lane_chunks):
            lanes = slice(lc * 128, (lc + 1) * 128)
            w = None
            for h in range(PEER_HEADS):
                w_h = jnp.where(s2_ref[h, lc] >= thr_rows[h][:, lanes],
                                e2_ref[h, lc] * e1_rows[h][:, lanes], 0.0)
                w = w_h if w is None else w + w_h
            a = act_ref[lc, pl.ds(roff, PEER_NKEYS), :]
            p_ref[pl.ds(roff, PEER_NKEYS), lanes] = (_gelu_tanh_x2(a) * w).astype(p_ref.dtype)
        return carry

    lax.fori_loop(0, rows, weigh, 0)
    res = jnp.dot(vt_ref[...], p_ref[...], preferred_element_type=F32)
    for lc in range(lane_chunks):
        acc_ref[lc] += res[:, lc * 128:(lc + 1) * 128]


def _peer_dense(xn, thr, e1, s2, e2, u_bf, vt_tiles, tm=512):
    t, d = xn.shape
    n_tiles, _, tn = vt_tiles.shape
    route_spec = pl.BlockSpec((PEER_HEADS, PEER_NKEYS, tm), lambda i, j: (0, 0, i))
    chunk_spec = pl.BlockSpec((PEER_HEADS, tm // 128, PEER_NKEYS, 128), lambda i, j: (0, i, 0, 0))
    return pl.pallas_call(
        _peer_dense_kernel,
        grid=(t // tm, n_tiles),
        in_specs=[pl.BlockSpec((tm, d), lambda i, j: (i, 0)),
                  route_spec, route_spec, chunk_spec, chunk_spec,
                  pl.BlockSpec((tn, d), lambda i, j: (j, 0)),
                  pl.BlockSpec((None, d, tn), lambda i, j: (j, 0, 0))],
        out_specs=pl.BlockSpec((tm // 128, d, 128), lambda i, j: (i, 0, 0)),
        out_shape=jax.ShapeDtypeStruct((t // 128, d, 128), F32),
        scratch_shapes=[pltpu.VMEM((d, tm), BF16), pltpu.VMEM((tm // 128, tn, 128), F32),
                        pltpu.VMEM((tn, tm), BF16)],
        compiler_params=_params(("parallel", "arbitrary")),
        name="peer_dense",
    )(xn, thr, e1, s2, e2, u_bf, vt_tiles)


def _final_kernel(x1_ref, pt_ref, g_ref, y_ref):
    for c in range(pt_ref.shape[0]):
        rows = slice(c * 128, (c + 1) * 128)
        y_ref[rows, :] = _rms(x1_ref[rows, :] + pt_ref[c].T, g_ref[...])


def _residual_norm(x1, peer_chunks, g, tm=512):
    t, d = x1.shape
    return pl.pallas_call(
        _final_kernel,
        grid=(t // tm,),
        in_specs=[pl.BlockSpec((tm, d), lambda i: (i, 0)),
                  pl.BlockSpec((tm // 128, d, 128), lambda i: (i, 0, 0)),
                  pl.BlockSpec((1, d), lambda i: (0, 0))],
        out_specs=pl.BlockSpec((tm, d), lambda i: (i, 0)),
        out_shape=jax.ShapeDtypeStruct((t, d), F32),
        compiler_params=_params(("parallel",)),
        name="residual_norm",
    )(x1, peer_chunks, g.reshape(1, d).astype(F32))


def kernel(x, norm_mix_g, w_in, conv_w, conv_b, lru_wa, lru_ba, lru_wx, lru_bx, lru_lambda,
           w_branch, w_out, rel_bias, norm_ffn_g, peer_wq, peer_keys, peer_u, peer_v, norm_final_g):
    batch, seq, d = x.shape
    t = batch * seq
    assert w_in.shape[0] == 1, "single-layer trunk"
    xt = x.reshape(t, d)
    bias = _bias_tiles(rel_bias)
    a_w = ATTN_WIDTH
    w_l = w_in[0]
    h = _rmsnorm(xt, norm_mix_g[0], BF16)
    qk = _matmul(h, w_l, 0, 2 * a_w, BF16, "proj_qk")
    vt3 = _v_transposed(w_l[:, 2 * a_w:3 * a_w].T.astype(BF16), h)
    assert 2 * LRU_WIDTH == d
    r = gl = _matmul(h, w_l, 3 * a_w, 2 * LRU_WIDTH + 2 * d, F32, "proj_rec_gate")
    o_att = _attention(qk, vt3, bias, batch, seq)
    o_rec = _rglru(r, conv_w[0], conv_b[0], lru_wa[0], lru_wx[0], lru_ba[0], lru_bx[0],
                   lru_lambda[0], batch, seq)
    x1, xn = _merge_out(o_att, o_rec, gl, w_branch[0, 0].astype(BF16),
                        w_branch[0, 1].astype(BF16), w_out[0].astype(BF16), xt, norm_ffn_g[0])
    q = _matmul(xn, peer_wq[0], 0, peer_wq.shape[2], BF16, "peer_query")
    keys = peer_keys[0].reshape(PEER_HEADS * 2, PEER_NKEYS, PEER_DKEY // 2).astype(BF16)
    thr, e1, s2, e2 = _peer_scores(q, keys)
    u_bf, vt_tiles = _expert_tiles(peer_u[0], peer_v[0], PEER_EXPERT_TILE)
    peer_out = _peer_dense(xn, thr, e1, s2, e2, u_bf, vt_tiles)
    y = _residual_norm(x1, peer_out, norm_final_g)
    return y.reshape(batch, seq, d)
```
